```python
import math
import jax, jax.numpy as jnp
from jax import lax
import numpy as np

D_MODEL = 1024
BATCH = 32
SEQ = 256
DEPTH = 1
DEC_BATCH = 2
DEC_SEQ = 4096
PAST_LEN = 256

GRID_W = 64
D_MIX = D_MODEL
D_HY = D_MIX // 2
D_RET = D_MIX - D_HY
N_RET_HEADS = 4
RET_DK = D_RET // N_RET_HEADS
RET_DV = D_RET // N_RET_HEADS
RET_CHUNK = 128
ROPE_BASE = 10000.0
HY_ORDER = 2
HY_SHORT = 3
HY_EMB = 33
HY_BANDS = (HY_EMB - 1) // 2
HY_FILT_W = 64
HY_DECAY_TARGET = 1e-2
HY_FAST_PCT = 0.3
HY_SLOW_PCT = 1.5
D_IN = 3 * D_HY + 4 * D_RET
N_KEYS = 128
N_EXPERTS = N_KEYS * N_KEYS
PEER_HEADS = 8
PEER_TOPK = 16
PEER_DQ = 256
PEER_BLOCK = 128
N_MOD = 6

kernel_name = 'hyena_retnet_peer_hybrid_step'

F32 = jnp.float32


def _rmsnorm(x, g, eps=1e-6):
    xf = x.astype(F32)
    y = xf * lax.rsqrt(jnp.mean(xf * xf, axis=-1, keepdims=True) + eps)
    return (y * g.astype(F32)).astype(x.dtype)


def _modulate(h, shift, scale):
    return h * (1.0 + scale[:, None, :]) + shift[:, None, :]


def _short_conv(u, w, b):
    L = u.shape[1]
    pad = HY_SHORT // 2
    up = jnp.pad(u, ((0, 0), (pad, HY_SHORT - 1 - pad), (0, 0)))
    out = b
    for j in range(HY_SHORT):
        out = out + up[:, j:j + L] * w[j]
    return out


def _hyena_filters(L, w1, b1, w2, b2, w3, b3, w4, freq):
    t = jnp.linspace(0.0, 1.0, L, dtype=F32)[:, None]
    w = 2.0 * math.pi * jnp.arange(L, dtype=F32)[:, None] / L
    f = jnp.linspace(1e-4, HY_BANDS - 1, HY_BANDS, dtype=F32)[None, :]
    z = jnp.concatenate([t, jnp.cos(f * w), -jnp.sin(f * w)], axis=-1).astype(w1.dtype)
    h = jnp.sin(freq * (z @ w1 + b1))
    h = jnp.sin(freq * (h @ w2 + b2))
    h = jnp.sin(freq * (h @ w3 + b3))
    h = (h @ w4).astype(F32).reshape(L, 2, HY_ORDER, D_HY)
    min_decay = math.log(HY_DECAY_TARGET) / HY_SLOW_PCT
    max_decay = math.log(HY_DECAY_TARGET) / HY_FAST_PCT
    deltas = jnp.abs(jnp.linspace(min_decay, max_decay, D_HY, dtype=F32))
    window = jnp.exp(-t * deltas[None, :])
    h = h * window[:, None, None, :]
    h_fwd, h_bwd = h[:, 0], h[:, 1]
    return jnp.concatenate([h_fwd, jnp.zeros((1, HY_ORDER, D_HY), F32), h_bwd[1:][::-1]], axis=0)


def _long_conv(u, k, bias):
    L = u.shape[1]
    uf = jnp.fft.rfft(u.astype(F32), n=2 * L, axis=1)
    kf = jnp.fft.rfft(k, n=2 * L, axis=0)
    y = jnp.fft.irfft(uf * kf[None], n=2 * L, axis=1)[:, :L]
    return (y + u.astype(F32) * bias.astype(F32)).astype(u.dtype)


def _hyena_mixer(p, conv_w, conv_b, w1, b1, w2, b2, w3, b3, w4, freq, fbias):
    L = p.shape[1]
    u = _short_conv(p, conv_w, conv_b)
    v, x1, x2 = jnp.split(u, 3, axis=-1)
    k = _hyena_filters(L, w1, b1, w2, b2, w3, b3, w4, freq)
    z = x1 * _long_conv(v, k[:, 0], fbias[0])
    z = x2 * _long_conv(z, k[:, 1], fbias[1])
    return z


def _axial_rope(x):
    L = x.shape[2]
    rows = L // GRID_W
    r, c = jnp.meshgrid(jnp.arange(rows, dtype=F32), jnp.arange(GRID_W, dtype=F32), indexing='ij')
    r = r.reshape(-1)
    c = c.reshape(-1)
    nf = RET_DK // 4
    inv = ROPE_BASE ** (-jnp.arange(nf, dtype=F32) / nf)
    ang = jnp.concatenate([r[:, None] * inv, c[:, None] * inv], axis=-1)
    cos = jnp.concatenate([jnp.cos(ang), jnp.cos(ang)], axis=-1)
    sin = jnp.concatenate([jnp.sin(ang), jnp.sin(ang)], axis=-1)
    half = RET_DK // 2
    rot = jnp.concatenate([-x[..., half:], x[..., :half]], axis=-1)
    return x * cos + rot * sin


def _retention_chunkwise(q, k, v, log_gamma, state0):
    B, H, L, dk = q.shape
    dv = v.shape[-1]
    C = RET_CHUNK
    n = L // C
    lg = log_gamma.astype(F32)
    pos = jnp.arange(C, dtype=F32)
    diff = pos[:, None] - pos[None, :]
    decay_mask = jnp.where(diff[None] >= 0, jnp.exp(jnp.maximum(diff, 0.0)[None] * lg[:, None, None]), 0.0)
    xi = jnp.exp((pos[None, :] + 1.0) * lg[:, None])
    zeta = jnp.exp((C - 1.0 - pos[None, :]) * lg[:, None])
    chunk_decay = jnp.exp(C * lg)
    qc = q.reshape(B, H, n, C, dk)
    kc = k.reshape(B, H, n, C, dk)
    vc = v.reshape(B, H, n, C, dv)
    scores = jnp.einsum('bhcnd,bhcmd->bhcnm', qc, kc) * decay_mask[None, :, None]
    intra = jnp.einsum('bhcnm,bhcme->bhcne', scores, vc)
    kz = kc * zeta[None, :, None, :, None]

    def step(S, xs):
        q_i, kz_i, v_i = xs
        cross = jnp.einsum('bhnd,bhde->bhne', q_i, S) * xi[None, :, :, None]
        S = S * chunk_decay[None, :, None, None] + jnp.einsum('bhmd,bhme->bhde', kz_i, v_i)
        return S, cross

    xs = (jnp.moveaxis(qc, 2, 0), jnp.moveaxis(kz, 2, 0), jnp.moveaxis(vc, 2, 0))
    S_final, cross = lax.scan(step, state0.astype(F32), xs)
    out = (intra + jnp.moveaxis(cross, 0, 2)).reshape(B, H, L, dv)
    return out, S_final


def _retention_mixer(q, k, v, g, decay_f, decay_b, s0_f, s0_b, norm_g, latent):
    B, L, _ = q.shape

    def heads(t):
        return t.reshape(B, L, N_RET_HEADS, -1).transpose(0, 2, 1, 3).astype(F32)

    qh, kh, vh = heads(q), heads(k), heads(v)
    if latent:
        qh = _axial_rope(qh)
        kh = _axial_rope(kh)
    kh = kh * (RET_DK ** -0.5)
    lg_f = -jnp.exp(decay_f.astype(F32))
    lg_b = -jnp.exp(decay_b.astype(F32))
    o_f, S_f = _retention_chunkwise(qh, kh, vh, lg_f, s0_f)
    o_b, S_b = _retention_chunkwise(qh[:, :, ::-1], kh[:, :, ::-1], vh[:, :, ::-1], lg_b, s0_b)
    o = (o_f + o_b[:, :, ::-1]).transpose(0, 2, 1, 3)
    o = _rmsnorm(o, norm_g.reshape(N_RET_HEADS, RET_DV)).reshape(B, L, D_RET).astype(q.dtype)
    o = o * jax.nn.silu(g)
    return o, S_f.astype(q.dtype), S_b.astype(q.dtype)


def _peer(h, w_q, sub_keys, u_tab, v_tab):
    B, L, D = h.shape
    T = B * L
    hf = h.reshape(T, D)
    q = (hf @ w_q).reshape(T, PEER_HEADS, 2, PEER_DQ // 2).astype(F32)
    s = jnp.einsum('thpd,hpkd->thpk', q, sub_keys.astype(F32))
    v1, i1 = lax.top_k(s[:, :, 0], PEER_TOPK)
    v2, i2 = lax.top_k(s[:, :, 1], PEER_TOPK)
    cand = (v1[..., :, None] + v2[..., None, :]).reshape(T, PEER_HEADS, PEER_TOPK * PEER_TOPK)
    cidx = (i1[..., :, None] * N_KEYS + i2[..., None, :]).reshape(T, PEER_HEADS, PEER_TOPK * PEER_TOPK)
    top_s, pos = lax.top_k(cand, PEER_TOPK)
    idx = jnp.take_along_axis(cidx, pos, axis=-1).reshape(T, PEER_HEADS * PEER_TOPK)
    gates = jax.nn.softmax(top_s, axis=-1).reshape(T, PEER_HEADS * PEER_TOPK).astype(h.dtype)
    nb = T // PEER_BLOCK

    def block(args):
        xb, ib, gb = args
        u = jnp.take(u_tab, ib, axis=0)
        a = jax.nn.gelu(jnp.einsum('td,tkd->tk', xb, u))
        ve = jnp.take(v_tab, ib, axis=0)
        return jnp.einsum('tk,tkd->td', gb * a, ve)

    out = lax.map(block, (hf.reshape(nb, PEER_BLOCK, D), idx.reshape(nb, PEER_BLOCK, -1), gates.reshape(nb, PEER_BLOCK, -1)))
    return out.reshape(B, L, D)


def _trunk_layer(x, cond, s0_f, s0_b, latent, w_ada, b_ada, norm1_g, w_in, hy_conv_w, hy_conv_b,
                 hy_w1, hy_b1, hy_w2, hy_b2, hy_w3, hy_b3, hy_w4, hy_freq, hy_bias, hy_norm_g,
                 ret_decay_fwd, ret_decay_bwd, ret_norm_g, w_out, norm2_g, peer_wq, peer_keys, peer_u, peer_v):
    mod = jax.nn.silu(cond) @ w_ada + b_ada
    sh1, sc1, gt1, sh2, sc2, gt2 = jnp.split(mod, N_MOD, axis=-1)
    h = _modulate(_rmsnorm(x, norm1_g), sh1, sc1)
    p = h @ w_in
    hy_in = p[..., :3 * D_HY]
    rq, rk, rv, rg = jnp.split(p[..., 3 * D_HY:], 4, axis=-1)
    y_hy = _hyena_mixer(hy_in, hy_conv_w, hy_conv_b, hy_w1, hy_b1, hy_w2, hy_b2, hy_w3, hy_b3, hy_w4, hy_freq, hy_bias)
    y_ret, S_f, S_b = _retention_mixer(rq, rk, rv, rg, ret_decay_fwd, ret_decay_bwd, s0_f, s0_b, ret_norm_g, latent)
    y = jnp.concatenate([_rmsnorm(y_hy, hy_norm_g), y_ret], axis=-1) @ w_out
    x = x + gt1[:, None, :] * y
    h2 = _modulate(_rmsnorm(x, norm2_g), sh2, sc2)
    x = x + gt2[:, None, :] * _peer(h2, peer_wq, peer_keys, peer_u, peer_v)
    return x, S_f, S_b


def setup_inputs(seed: int = 0) -> dict:
    key = jax.random.key(seed)
    ks = iter(jax.random.split(key, 48))

    def nrm(shape, scale):
        return jax.random.normal(next(ks), shape, F32) * scale

    heads = jnp.arange(N_RET_HEADS, dtype=F32)
    decay0 = jnp.log(-jnp.log(1.0 - 2.0 ** (-5.0 - heads)))
    st = (DEC_BATCH, DEPTH, N_RET_HEADS, RET_DK, RET_DV)
    return {
        'x_prompt': nrm((BATCH, SEQ, D_MODEL), 1.0),
        'x_sample': nrm((DEC_BATCH, DEC_SEQ, D_MODEL), 1.0),
        'state_ret_fwd': nrm(st, 1.0),
        'state_ret_bwd': nrm(st, 1.0),
        'c': nrm((DEC_BATCH, D_MODEL), 1.0),
        'c_ctx': nrm((D_MODEL,), 1.0),
        'w_ada': nrm((DEPTH, D_MODEL, N_MOD * D_MODEL), D_MODEL ** -0.5),
        'b_ada': nrm((DEPTH, N_MOD * D_MODEL), 0.01),
        'norm1_g': 1.0 + nrm((DEPTH, D_MODEL), 0.01),
        'w_in': nrm((DEPTH, D_MODEL, D_IN), D_MODEL ** -0.5),
        'hy_conv_w': nrm((DEPTH, HY_SHORT, 3 * D_HY), HY_SHORT ** -0.5),
        'hy_conv_b': nrm((DEPTH, 3 * D_HY), 0.01),
        'hy_w1': nrm((DEPTH, HY_EMB, HY_FILT_W), HY_EMB ** -0.5),
        'hy_b1': nrm((DEPTH, HY_FILT_W), 0.01),
        'hy_w2': nrm((DEPTH, HY_FILT_W, HY_FILT_W), HY_FILT_W ** -0.5),
        'hy_b2': nrm((DEPTH, HY_FILT_W), 0.01),
        'hy_w3': nrm((DEPTH, HY_FILT_W, HY_FILT_W), HY_FILT_W ** -0.5),
        'hy_b3': nrm((DEPTH, HY_FILT_W), 0.01),
        'hy_w4': nrm((DEPTH, HY_FILT_W, 2 * HY_ORDER * D_HY), 0.1 * HY_FILT_W ** -0.5),
        'hy_freq': 1.0 + nrm((DEPTH, HY_FILT_W), 0.01),
        'hy_bias': nrm((DEPTH, HY_ORDER, D_HY), 0.1),
        'hy_norm_g': 1.0 + nrm((DEPTH, D_HY), 0.01),
        'ret_decay_fwd': decay0[None, :] + nrm((DEPTH, N_RET_HEADS), 0.01),
        'ret_decay_bwd': decay0[None, :] + nrm((DEPTH, N_RET_HEADS), 0.01),
        'ret_norm_g': 1.0 + nrm((DEPTH, D_RET), 0.01),
        'w_out': nrm((DEPTH, D_MIX, D_MODEL), D_MIX ** -0.5),
        'norm2_g': 1.0 + nrm((DEPTH, D_MODEL), 0.01),
        'peer_wq': nrm((DEPTH, D_MODEL, PEER_HEADS * PEER_DQ), D_MODEL ** -0.5),
        'peer_keys': nrm((DEPTH, PEER_HEADS, 2, N_KEYS, PEER_DQ // 2), (PEER_DQ // 2) ** -0.5),
        'peer_u': nrm((DEPTH, N_EXPERTS, D_MODEL), D_MODEL ** -0.5),
        'peer_v': nrm((DEPTH, N_EXPERTS, D_MODEL), 0.5),
        'final_g': 1.0 + nrm((D_MODEL,), 0.01),
    }


def reference(x_prompt, x_sample, state_ret_fwd, state_ret_bwd, c, c_ctx, w_ada, b_ada, norm1_g, w_in,
              hy_conv_w, hy_conv_b, hy_w1, hy_b1, hy_w2, hy_b2, hy_w3, hy_b3, hy_w4, hy_freq, hy_bias,
              hy_norm_g, ret_decay_fwd, ret_decay_bwd, ret_norm_g, w_out, norm2_g, peer_wq, peer_keys,
              peer_u, peer_v, final_g):
    B_ctx = x_prompt.shape[0]
    x_ctx = x_prompt
    x_lat = x_sample
    cond_ctx = c_ctx[None, :]
    zero_state = jnp.zeros((B_ctx, N_RET_HEADS, RET_DK, RET_DV), x_prompt.dtype)
    new_f, new_b = [], []
    for l in range(DEPTH):
        lw = (w_ada[l], b_ada[l], norm1_g[l], w_in[l], hy_conv_w[l], hy_conv_b[l], hy_w1[l], hy_b1[l],
              hy_w2[l], hy_b2[l], hy_w3[l], hy_b3[l], hy_w4[l], hy_freq[l], hy_bias[l], hy_norm_g[l],
              ret_decay_fwd[l], ret_decay_bwd[l], ret_norm_g[l], w_out[l], norm2_g[l], peer_wq[l],
              peer_keys[l], peer_u[l], peer_v[l])
        x_ctx, S_f, S_b = _trunk_layer(x_ctx, cond_ctx, zero_state, zero_state, False, *lw)
        new_f.append(S_f)
        new_b.append(S_b)
        x_lat, _, _ = _trunk_layer(x_lat, c, state_ret_fwd[:, l], state_ret_bwd[:, l], True, *lw)
    y_prompt = _rmsnorm(x_ctx, final_g)
    y_sample = _rmsnorm(x_lat, final_g)
    new_state_ret_fwd = jnp.stack(new_f, axis=1)
    new_state_ret_bwd = jnp.stack(new_b, axis=1)
    return (y_prompt, y_sample, new_state_ret_fwd, new_state_ret_bwd)
```

```python
import functools
import math

import numpy as np
import jax
import jax.numpy as jnp
from jax import lax
from jax.experimental import pallas as pl
from jax.experimental.pallas import tpu as pltpu

F32 = jnp.float32
BF16 = jnp.bfloat16

LANES = 128
SUBLANES = 8
VMEM_LIMIT = 56 * 1024 * 1024

EPS = 1e-6
GRID_W = 64
N_RET_HEADS = 4
RET_CHUNK = 128
ROPE_BASE = 10000.0
HY_EMB = 33
HY_BANDS = (HY_EMB - 1) // 2
HY_DECAY_TARGET = 1e-2
HY_FAST_PCT = 0.3
HY_SLOW_PCT = 1.5
N_KEYS = 128
PEER_HEADS = 8
PEER_TOPK = 16
FFT_P = 256
FFT_N = 2 * FFT_P
TOK_VREG = SUBLANES * LANES
TILE_PITCH = 136


def _cparams(sem, vmem=VMEM_LIMIT):
    return pltpu.CompilerParams(dimension_semantics=sem, vmem_limit_bytes=vmem)


def _dot(a, b):
    return jnp.dot(a, b, preferred_element_type=F32)


def _split(a):
    hi = a.astype(BF16)
    lo = (a - hi.astype(F32)).astype(BF16)
    return hi, lo


def _dot3(a, b):
    ah, al = _split(a)
    bh, bl = _split(b)
    return _dot(ah, bh) + (_dot(al, bh) + _dot(ah, bl))


def _rms(x, g):
    return x * lax.rsqrt(jnp.mean(x * x, axis=-1, keepdims=True) + EPS) * g


def _ada_kernel(c_ref, w_ref, b_ref, o_ref):
    c = c_ref[...]
    s = c * jax.nn.sigmoid(c)
    o_ref[...] = _dot3(s, w_ref[...]) + b_ref[...]


def _ada(cond8, w_ada, b_ada):
    d, n = w_ada.shape
    tn = 1536
    return pl.pallas_call(
        _ada_kernel,
        out_shape=jax.ShapeDtypeStruct((cond8.shape[0], n), F32),
        grid=(n // tn,),
        in_specs=[pl.BlockSpec((cond8.shape[0], d), lambda j: (0, 0)),
                  pl.BlockSpec((d, tn), lambda j: (0, j)),
                  pl.BlockSpec((1, tn), lambda j: (0, j))],
        out_specs=pl.BlockSpec((cond8.shape[0], tn), lambda j: (0, j)),
        compiler_params=_cparams(("parallel",)),
        name="ada",
    )(cond8, w_ada, b_ada.reshape(1, n))


def _inproj_kernel(x_ref, sh_ref, sc_ref, g_ref, w_ref, hy_ref, q_ref, k_ref, v_ref, gg_ref, *, d_hy3, d_ret):
    h = _rms(x_ref[...], g_ref[...]) * (1.0 + sc_ref[...]) + sh_ref[...]
    hb = h.astype(BF16)
    hy_ref[...] = _dot(hb, w_ref[:, 0:d_hy3])
    for i, o_ref in enumerate((q_ref, k_ref, v_ref, gg_ref)):
        o_ref[...] = _dot(hb, w_ref[:, d_hy3 + i * d_ret:d_hy3 + (i + 1) * d_ret])


def _inproj(x, sh, sc, g, w_bf, grp, tm, d_hy3, d_ret):
    t, d = x.shape
    row = lambda i: (i, 0)
    modspec = pl.BlockSpec((None, 1, d), lambda i: (grp(i), 0, 0))
    outs = [jax.ShapeDtypeStruct((t, d_hy3), F32)] + [jax.ShapeDtypeStruct((t, d_ret), F32)] * 4
    return pl.pallas_call(
        functools.partial(_inproj_kernel, d_hy3=d_hy3, d_ret=d_ret),
        out_shape=outs,
        grid=(t // tm,),
        in_specs=[pl.BlockSpec((tm, d), row), modspec, modspec,
                  pl.BlockSpec((1, d), lambda i: (0, 0)),
                  pl.BlockSpec(w_bf.shape, lambda i: (0, 0))],
        out_specs=[pl.BlockSpec((tm, d_hy3), row)] + [pl.BlockSpec((tm, d_ret), row)] * 4,
        compiler_params=_cparams(("parallel",)),
        name="inproj",
    )(x, sh, sc, g, w_bf)


def _kf_kernel(frow_ref, w1_ref, b1_ref, w2_ref, b2_ref, w3_ref, b3_ref, fr_ref, w4_ref, dl_ref, o_ref, *, seq):
    r = pl.program_id(0)
    rows = o_ref.shape[0]
    i = r * rows + lax.broadcasted_iota(jnp.int32, (rows, LANES), 0)
    pos = jnp.abs(i - seq).astype(F32)
    t = pos / (seq - 1.0)
    ang = frow_ref[...] * ((2.0 * math.pi) * pos / seq)
    lane = lax.broadcasted_iota(jnp.int32, (rows, LANES), 1)
    z = jnp.where(lane == 0, t,
                  jnp.where(lane <= HY_BANDS, jnp.cos(ang),
                            jnp.where(lane <= 2 * HY_BANDS, -jnp.sin(ang), 0.0)))
    fr = fr_ref[...]
    h = jnp.sin(fr * (_dot3(z, w1_ref[...]) + b1_ref[...]))
    h = jnp.sin(fr * (_dot3(h, w2_ref[...]) + b2_ref[...]))
    h = jnp.sin(fr * (_dot3(h, w3_ref[...]) + b3_ref[...]))
    h4 = _dot3(h, w4_ref[...])
    nc = o_ref.shape[1]
    iw = r * rows + lax.broadcasted_iota(jnp.int32, (rows, nc), 0)
    tw = jnp.abs(iw - seq).astype(F32) / (seq - 1.0)
    win = jnp.exp(-tw * dl_ref[...])
    o_ref[...] = jnp.where(iw == 0, 0.0, h4 * win)


def _spec_kernel(k0_ref, k1_ref, a_ref, o_ref):
    a = a_ref[...]
    o_ref[...] = (_dot(a[:, :FFT_P], k0_ref[...].astype(BF16))
                  + _dot(a[:, FFT_P:], k1_ref[...].astype(BF16)))


def _dft_mats():
    f = np.arange(FFT_P)[:, None].astype(np.float64)
    t = np.arange(FFT_P)[None, :].astype(np.float64)
    j = np.arange(FFT_N)[None, :].astype(np.float64)
    w = 2.0 * np.pi / FFT_N
    fwd = np.concatenate([np.cos(w * f * t), -np.sin(w * f * t)], axis=0)
    fwd[FFT_P] = np.cos(np.pi * t[0])
    scale = np.full((1, FFT_P), 2.0 / FFT_N)
    scale[0, 0] = 1.0 / FFT_N
    inv = np.concatenate([np.cos(w * t.T * f.T) * scale, -np.sin(w * t.T * f.T) * scale], axis=1)
    inv[:, FFT_P] = np.cos(np.pi * t[0]) / FFT_N
    sgn = np.where(np.arange(FFT_P) % 2 == 0, 1.0, -1.0)[:, None]
    re = sgn * np.cos(w * f * j)
    im = -sgn * np.sin(w * f * j)
    nyq = np.cos(np.pi * j[0])
    re2 = re.copy()
    re2[0] = nyq
    im[0] = 0.0
    flt = np.concatenate([re, re2, im], axis=0)
    flt[:, 0] = 0.0
    return (jnp.asarray(fwd, F32).astype(BF16), jnp.asarray(inv, F32).astype(BF16),
            jnp.asarray(flt, F32).astype(BF16))


def _hyena_spectra(seq, fw, flt_mat):
    frow, w1, b1, w2, b2, w3, b3, fr, w4, dl = fw
    nblk = 2 * seq // FFT_P
    nc = w4.shape[-1]
    full = lambda a: pl.BlockSpec(a.shape, lambda r: (0,) * a.ndim)
    kf = pl.pallas_call(
        functools.partial(_kf_kernel, seq=seq),
        out_shape=jax.ShapeDtypeStruct((2 * seq, nc), F32),
        grid=(nblk,),
        in_specs=[full(frow), full(w1), full(b1), full(w2), full(b2), full(w3), full(b3), full(fr),
                  pl.BlockSpec((None,) + w4.shape[1:], lambda r: (jnp.where(r >= nblk // 2, 0, 1), 0, 0)),
                  full(dl)],
        out_specs=pl.BlockSpec((FFT_P, nc), lambda r: (r, 0)),
        compiler_params=_cparams(("parallel",)),
        name="hyena_filter",
    )(frow, w1, b1, w2, b2, w3, b3, fr, w4, dl)
    return pl.pallas_call(
        _spec_kernel,
        out_shape=jax.ShapeDtypeStruct((nblk - 1, 3 * FFT_P, nc), F32),
        grid=(nblk - 1,),
        in_specs=[pl.BlockSpec((FFT_P, nc), lambda w: (w, 0)),
                  pl.BlockSpec((FFT_P, nc), lambda w: (w + 1, 0)),
                  pl.BlockSpec(flt_mat.shape, lambda w: (0, 0))],
        out_specs=pl.BlockSpec((None, 3 * FFT_P, nc), lambda w: (w, 0, 0)),
        compiler_params=_cparams(("parallel",)),
        name="hyena_spectra",
    )(kf, kf, flt_mat)


def _hy_kernel(u_ref, x_ref, cwu_ref, cbu_ref, cwx_ref, cbx_ref, bias_ref, g_ref, fwd_ref, inv_ref,
               o_ref, u_scr, gate_scr, uf_scr, y_scr, *, n, conv_u):
    seq = n * FFT_P
    cb = o_ref.shape[1]

    def sconv(v, w_ref, b_ref):
        row = lax.broadcasted_iota(jnp.int32, v.shape, 0)
        prev = jnp.where(row == 0, 0.0, pltpu.roll(v, 1, 0))
        nxt = jnp.where(row == seq - 1, 0.0, pltpu.roll(v, seq - 1, 0))
        return b_ref[...] + prev * w_ref[0:1, :] + v * w_ref[1:2, :] + nxt * w_ref[2:3, :]

    u_scr[...] = sconv(u_ref[...], cwu_ref, cbu_ref) if conv_u else u_ref[...]
    gate_scr[...] = sconv(x_ref[...], cwx_ref, cbx_ref)
    fwd = fwd_ref[...]
    for b in range(n):
        uf_scr[b] = _dot(fwd, u_scr[b * FFT_P:(b + 1) * FFT_P, :].astype(BF16))

    rc = 32

    def a_body(a, carry):
        for c in range(FFT_P // rc):
            r0 = c * rc

            def b_body(b, acc):
                yre, yim = acc
                w = a - b + (n - 1)
                gre = g_ref[w, pl.ds(r0, rc), :]
                gre2 = g_ref[w, pl.ds(FFT_P + r0, rc), :]
                gim = g_ref[w, pl.ds(2 * FFT_P + r0, rc), :]
                ure = uf_scr[b, pl.ds(r0, rc), :]
                uim = uf_scr[b, pl.ds(FFT_P + r0, rc), :]
                return yre + gre * ure - gim * uim, yim + gre2 * uim + gim * ure

            zero = jnp.zeros((rc, cb), F32)
            yre, yim = lax.fori_loop(0, n, b_body, (zero, zero))
            y_scr[pl.ds(r0, rc), :] = yre
            y_scr[pl.ds(FFT_P + r0, rc), :] = yim
        y = _dot(inv_ref[...], y_scr[...].astype(BF16))
        rows = pl.ds(pl.multiple_of(a * FFT_P, FFT_P), FFT_P)
        ua = u_scr[rows, :]
        o_ref[rows, :] = gate_scr[rows, :] * (y + ua * bias_ref[...])
        return carry

    lax.fori_loop(0, n, a_body, 0)


def _hyena_order(u_arr, u_row0, u_col0, x_arr, x_row0, x_col0, conv_w, conv_b, bias_row, spectra, spec_col0,
                 fwd, inv, nbatch, seq, cb, conv_u):
    n = seq // FFT_P
    d_hy = bias_row.shape[1]
    ncb = d_hy // cb
    nwin = spectra.shape[0]
    grid = (ncb, nbatch)
    in_specs = [
        pl.BlockSpec((seq, cb), lambda c, b: (u_row0 + b, u_col0 + c)),
        pl.BlockSpec((seq, cb), lambda c, b: (x_row0 + b, x_col0 + c)),
        pl.BlockSpec((3, cb), lambda c, b: (0, u_col0 + c)),
        pl.BlockSpec((1, cb), lambda c, b: (0, u_col0 + c)),
        pl.BlockSpec((3, cb), lambda c, b: (0, x_col0 + c)),
        pl.BlockSpec((1, cb), lambda c, b: (0, x_col0 + c)),
        pl.BlockSpec((1, cb), lambda c, b: (0, c)),
        pl.BlockSpec((nwin, 3 * FFT_P, cb), lambda c, b: (0, 0, spec_col0 + c)),
        pl.BlockSpec(fwd.shape, lambda c, b: (0, 0)),
        pl.BlockSpec(inv.shape, lambda c, b: (0, 0)),
    ]
    return pl.pallas_call(
        functools.partial(_hy_kernel, n=n, conv_u=conv_u),
        out_shape=jax.ShapeDtypeStruct((nbatch * seq, d_hy), F32),
        grid=grid,
        in_specs=in_specs,
        out_specs=pl.BlockSpec((seq, cb), lambda c, b: (b, c)),
        scratch_shapes=[pltpu.VMEM((seq, cb), F32), pltpu.VMEM((seq, cb), F32),
                        pltpu.VMEM((n, FFT_N, cb), F32), pltpu.VMEM((FFT_N, cb), F32)],
        compiler_params=_cparams(("parallel", "parallel")),
        name="hyena_conv",
    )(u_arr, x_arr, conv_w, conv_b, conv_w, conv_b, bias_row, spectra, fwd, inv)


def _hyena_mix(hy_in, conv_w, conv_b, bias, spectra, fwd, inv, row0, nbatch, seq, cb):
    ncb = bias.shape[1] // cb
    z1 = _hyena_order(hy_in, row0, 0, hy_in, row0, ncb, conv_w, conv_b, bias[0:1], spectra, 0,
                      fwd, inv, nbatch, seq, cb, True)
    return _hyena_order(z1, 0, 0, hy_in, row0, 2 * ncb, conv_w, conv_b, bias[1:2], spectra, ncb,
                        fwd, inv, nbatch, seq, cb, False)


def _ret_kernel(q_ref, k_ref, v_ref, g_ref, cos_ref, sin_ref, df_ref, db_ref, ng_ref, s0f_ref, s0b_ref,
                y_ref, sf_ref, sb_ref, q_scr, k_scr, sb_scr, *, n, rope, has_state):
    c = RET_CHUNK
    dk = q_ref.shape[1]
    lgf = -jnp.exp(df_ref[...])
    lgb = -jnp.exp(db_ref[...])
    ri = lax.broadcasted_iota(jnp.int32, (c, c), 0)
    ci = lax.broadcasted_iota(jnp.int32, (c, c), 1)
    diff = (ri - ci).astype(F32)
    mask = (jnp.where(diff >= 0, jnp.exp(jnp.maximum(diff, 0.0) * lgf), 0.0)
            + jnp.where(diff <= 0, jnp.exp(jnp.maximum(-diff, 0.0) * lgb), 0.0))
    pos = lax.broadcasted_iota(jnp.int32, (c, dk), 0).astype(F32)
    xi_f = jnp.exp((pos + 1.0) * lgf)
    ze_f = jnp.exp((c - 1.0 - pos) * lgf)
    cd_f = jnp.exp(c * lgf)
    xi_b = jnp.exp((c - pos) * lgb)
    ze_b = jnp.exp(pos * lgb)
    cd_b = jnp.exp(c * lgb)

    q = q_ref[...]
    k = k_ref[...]
    if rope:
        cs = cos_ref[...]
        sn = sin_ref[...]
        q = q * cs + pltpu.roll(q, dk // 2, 1) * sn
        k = k * cs + pltpu.roll(k, dk // 2, 1) * sn
    q_scr[...] = q
    k_scr[...] = k * (dk ** -0.5)

    def dot_tn(a, b):
        return lax.dot_general(a, b, (((0,), (0,)), ((), ())), preferred_element_type=F32)

    def chunk(i):
        return pl.ds(pl.multiple_of(i * c, c), c)

    def bwd_body(j, s):
        i = n - 1 - j
        sb_scr[i] = s
        rows = chunk(i)
        kz = (k_scr[rows, :] * ze_b).astype(BF16)
        return s * cd_b + dot_tn(kz, v_ref[rows, :].astype(BF16))

    s0b = s0b_ref[...] if has_state else jnp.zeros((dk, dk), F32)
    sb_ref[...] = lax.fori_loop(0, n, bwd_body, s0b)

    def fwd_body(i, s):
        rows = chunk(i)
        qb = q_scr[rows, :].astype(BF16)
        kc = k_scr[rows, :]
        vb = v_ref[rows, :].astype(BF16)
        sc = lax.dot_general(qb, kc.astype(BF16), (((1,), (1,)), ((), ())), preferred_element_type=F32)
        o = _dot((sc * mask).astype(BF16), vb)
        o = o + _dot(qb, s.astype(BF16)) * xi_f + _dot(qb, sb_scr[i].astype(BF16)) * xi_b
        gt = g_ref[rows, :]
        y_ref[rows, :] = _rms(o, ng_ref[...]) * (gt * jax.nn.sigmoid(gt))
        return s * cd_f + dot_tn((kc * ze_f).astype(BF16), vb)

    s0f = s0f_ref[...] if has_state else jnp.zeros((dk, dk), F32)
    sf_ref[...] = lax.fori_loop(0, n, fwd_body, s0f)


def _retention(q, k, v, g, cos, sin, dec_f, dec_b, norm_g, s0f, s0b, row_blk0, nbatch, seq, rope, has_state):
    nh = N_RET_HEADS
    dk = q.shape[1] // nh
    n = seq // RET_CHUNK
    tok = lambda b, h: (row_blk0 + b, h)
    tspec = pl.BlockSpec((seq, dk), tok)
    rspec = pl.BlockSpec((seq, dk), lambda b, h: (0, 0))
    hspec = pl.BlockSpec((None, 1, dk), lambda b, h: (h, 0, 0))
    sspec = pl.BlockSpec((None, None, None, dk, dk), lambda b, h: (b, 0, h, 0, 0))
    s0spec = sspec if has_state else pl.BlockSpec((None, None, None, dk, dk), lambda b, h: (0, 0, h, 0, 0))
    st_shape = jax.ShapeDtypeStruct((nbatch, 1, nh, dk, dk), F32)
    return pl.pallas_call(
        functools.partial(_ret_kernel, n=n, rope=rope, has_state=has_state),
        out_shape=[jax.ShapeDtypeStruct((nbatch * seq, nh * dk), F32), st_shape, st_shape],
        grid=(nbatch, nh),
        in_specs=[tspec, tspec, tspec, tspec, rspec, rspec, hspec, hspec, hspec, s0spec, s0spec],
        out_specs=[pl.BlockSpec((seq, dk), lambda b, h: (b, h)), sspec, sspec],
        scratch_shapes=[pltpu.VMEM((seq, dk), F32), pltpu.VMEM((seq, dk), F32),
                        pltpu.VMEM((n, dk, dk), F32)],
        compiler_params=_cparams(("parallel", "parallel")),
        name="retention",
    )(q, k, v, g, cos, sin, dec_f, dec_b, norm_g, s0f, s0b)


def _outproj_kernel(yh_ref, yr_ref, x_ref, gt_ref, sh_ref, sc_ref, hg_ref, ng_ref, wo_ref, x1_ref, h2t_ref):
    d_hy = yh_ref.shape[1]
    nh = _rms(yh_ref[...], hg_ref[...]).astype(BF16)
    y = _dot(nh, wo_ref[0:d_hy, :]) + _dot(yr_ref[...].astype(BF16), wo_ref[d_hy:, :])
    x1 = x_ref[...] + gt_ref[...] * y
    x1_ref[...] = x1
    h2 = _rms(x1, ng_ref[...]) * (1.0 + sc_ref[...]) + sh_ref[...]
    h2t_ref[...] = h2.T.astype(BF16)


def _outproj(y_hy, y_ret, x, gt1, sh2, sc2, hy_g, n2_g, wo_bf, grp, tm):
    t, d = x.shape
    row = lambda i: (i, 0)
    modspec = pl.BlockSpec((None, 1, d), lambda i: (grp(i), 0, 0))
    return pl.pallas_call(
        _outproj_kernel,
        out_shape=[jax.ShapeDtypeStruct((t, d), F32), jax.ShapeDtypeStruct((d, t), BF16)],
        grid=(t // tm,),
        in_specs=[pl.BlockSpec((tm, y_hy.shape[1]), row), pl.BlockSpec((tm, y_ret.shape[1]), row),
                  pl.BlockSpec((tm, d), row), modspec, modspec, modspec,
                  pl.BlockSpec((1, y_hy.shape[1]), lambda i: (0, 0)),
                  pl.BlockSpec((1, d), lambda i: (0, 0)),
                  pl.BlockSpec(wo_bf.shape, lambda i: (0, 0))],
        out_specs=[pl.BlockSpec((tm, d), row), pl.BlockSpec((d, tm), lambda i: (0, i))],
        compiler_params=_cparams(("parallel",)),
        name="outproj",
    )(y_hy, y_ret, x, gt1, sh2, sc2, hy_g, n2_g, wo_bf)


def _sort_desc(xs):
    xs = list(xs)
    n = len(xs)
    k = 2
    while k <= n:
        j = k // 2
        while j >= 1:
            for i in range(n):
                l = i ^ j
                if l > i:
                    hi, lo = jnp.maximum(xs[i], xs[l]), jnp.minimum(xs[i], xs[l])
                    xs[i], xs[l] = (hi, lo) if (i & k) == 0 else (lo, hi)
            j //= 2
        k *= 2
    return xs


def _merge_top(a, b):
    n = len(a)
    xs = [jnp.maximum(a[i], b[n - 1 - i]) for i in range(n)]
    j = n // 2
    while j >= 1:
        for i in range(n):
            l = i ^ j
            if l > i:
                xs[i], xs[l] = jnp.maximum(xs[i], xs[l]), jnp.minimum(xs[i], xs[l])
        j //= 2
    return xs


def _top_sorted(load, count, k):
    acc = None
    for g0 in range(0, count, k):
        grp = _sort_desc([load(i) for i in range(g0, g0 + k)])
        acc = grp if acc is None else _merge_top(acc, grp)
    return acc


def _staircase(k):
    return [(a, b) for a in range(k) for b in range(k) if (a + 1) * (b + 1) <= k + 1]


def _peer1_kernel(h2t_ref, wq_ref, keys_ref, s2_ref, e2_ref, thr_ref, e1_ref, q_scr, s1_scr, s2_scr):
    h = pl.program_id(1)
    lt_n = s2_ref.shape[0]
    nk = N_KEYS
    kk = PEER_TOPK

    @pl.when(h == 0)
    def _():
        q_scr[...] = _dot(wq_ref[...], h2t_ref[...]).astype(BF16)

    dq = q_scr.shape[0] // (2 * PEER_HEADS)
    for p, scr in ((0, s1_scr), (1, s2_scr)):
        rows = pl.ds(pl.multiple_of((2 * h + p) * dq, dq), dq)
        s = _dot(keys_ref[2 * h + p], q_scr[rows, :])
        for lt in range(lt_n):
            scr[lt * TILE_PITCH:lt * TILE_PITCH + nk, :] = s[:, lt * LANES:(lt + 1) * LANES]

    ld1 = lambda i: s1_scr[pl.ds(i, lt_n, stride=TILE_PITCH), :]
    ld2 = lambda i: s2_scr[pl.ds(i, lt_n, stride=TILE_PITCH), :]
    v1 = _top_sorted(ld1, nk, kk)
    v2 = _top_sorted(ld2, nk, kk)
    pairs = _staircase(kk)
    cand = [v1[a] + v2[b] for a, b in pairs]
    neg = jnp.full_like(cand[0], -jnp.inf)
    padded = cand + [neg] * (-len(cand) % kk)
    top = _top_sorted(lambda i: padded[i], len(padded), kk)
    tau = top[kk - 1]
    nxt = neg
    for cnd in cand:
        nxt = jnp.maximum(nxt, jnp.where(cnd < tau, cnd, neg))
    thr = 0.5 * (tau + nxt)
    m = v1[0] + v2[0]
    z = jnp.zeros_like(tau)
    for cnd in cand:
        z = z + jnp.where(cnd >= tau, jnp.exp(cnd - m), 0.0)
    inv_z = 1.0 / z

    for i in range(nk):
        x1 = ld1(i)
        thr_ref[i] = thr - x1
        e1_ref[i] = jnp.where(x1 >= v1[kk - 1], jnp.exp(x1 - v1[0]) * inv_z, 0.0)
    for lt in range(lt_n):
        s2 = s2_scr[lt * TILE_PITCH:lt * TILE_PITCH + nk, :]
        s2_ref[lt] = s2
        e2_ref[lt] = jnp.where(s2 >= v2[kk - 1][lt:lt + 1, :], jnp.exp(s2 - v2[0][lt:lt + 1, :]), 0.0)


def _peer1(h2t, wqt_bf, keys_bf):
    d, t = h2t.shape
    nh = PEER_HEADS
    tb = TOK_VREG
    lt_n = tb // LANES
    tile_shape = jax.ShapeDtypeStruct((nh, t // LANES, N_KEYS, LANES), F32)
    vreg_shape = jax.ShapeDtypeStruct((nh, t // tb, N_KEYS, lt_n, LANES), F32)
    tile_spec = pl.BlockSpec((None, lt_n, N_KEYS, LANES), lambda i, h: (h, i, 0, 0))
    vreg_spec = pl.BlockSpec((None, None, N_KEYS, lt_n, LANES), lambda i, h: (h, i, 0, 0, 0))
    return pl.pallas_call(
        _peer1_kernel,
        out_shape=[tile_shape, tile_shape, vreg_shape, vreg_shape],
        grid=(t // tb, nh),
        in_specs=[pl.BlockSpec((d, tb), lambda i, h: (0, i)),
                  pl.BlockSpec(wqt_bf.shape, lambda i, h: (0, 0)),
                  pl.BlockSpec(keys_bf.shape, lambda i, h: (0, 0, 0))],
        out_specs=[tile_spec, tile_spec, vreg_spec, vreg_spec],
        scratch_shapes=[pltpu.VMEM((wqt_bf.shape[0], tb), BF16),
                        pltpu.VMEM((lt_n * TILE_PITCH, LANES), F32),
                        pltpu.VMEM((lt_n * TILE_PITCH, LANES), F32)],
        compiler_params=_cparams(("parallel", "arbitrary")),
        name="peer_select",
    )(h2t, wqt_bf, keys_bf)


def _gelu_tanh(x):
    return 0.5 * x * (1.0 + jnp.tanh(math.sqrt(2.0 / math.pi) * (x + 0.044715 * (x * x * x))))


def _peer2_kernel(h2t_ref, u_ref, vt_ref, s2_ref, e2_ref, thr_ref, e1_ref, ot_ref, at_scr, zt_scr):
    eb = pl.program_id(1)
    lt_n = s2_ref.shape[1]
    ig_n = thr_ref.shape[1]

    @pl.when(eb == 0)
    def _():
        ot_ref[...] = jnp.zeros_like(ot_ref)

    at_scr[...] = _dot(u_ref[...], h2t_ref[...])

    def ig_body(ig, carry):
        rows = pl.ds(pl.multiple_of(ig * N_KEYS, N_KEYS), N_KEYS)
        for lt in range(lt_n):
            cols = slice(lt * LANES, (lt + 1) * LANES)
            w = jnp.zeros((N_KEYS, LANES), F32)
            for hh in range(PEER_HEADS):
                th = thr_ref[hh, ig, lt:lt + 1, :]
                e1 = e1_ref[hh, ig, lt:lt + 1, :]
                w = w + jnp.where(s2_ref[hh, lt] >= th, e2_ref[hh, lt], 0.0) * e1
            zt_scr[rows, cols] = (w * _gelu_tanh(at_scr[rows, cols])).astype(BF16)
        return carry

    lax.fori_loop(0, ig_n, ig_body, 0)
    ot_ref[...] += _dot(vt_ref[...], zt_scr[...])


def _peer2(h2t, u_bf, vt_bf, s2, e2, thr, e1, eblk):
    d, t = h2t.shape
    ne = u_bf.shape[0]
    nh = PEER_HEADS
    tb = TOK_VREG
    lt_n = tb // LANES
    ig_n = eblk // N_KEYS
    tile_spec = pl.BlockSpec((nh, lt_n, N_KEYS, LANES), lambda i, e: (0, i, 0, 0))
    vreg_spec = pl.BlockSpec((nh, None, ig_n, lt_n, LANES), lambda i, e: (0, i, e, 0, 0))
    return pl.pallas_call(
        _peer2_kernel,
        out_shape=jax.ShapeDtypeStruct((d, t), F32),
        grid=(t // tb, ne // eblk),
        in_specs=[pl.BlockSpec((d, tb), lambda i, e: (0, i)),
                  pl.BlockSpec((eblk, d), lambda i, e: (e, 0)),
                  pl.BlockSpec((d, eblk), lambda i, e: (0, e)),
                  tile_spec, tile_spec, vreg_spec, vreg_spec],
        out_specs=pl.BlockSpec((d, tb), lambda i, e: (0, i)),
        scratch_shapes=[pltpu.VMEM((eblk, tb), F32), pltpu.VMEM((eblk, tb), BF16)],
        compiler_params=_cparams(("parallel", "arbitrary")),
        name="peer_experts",
    )(h2t, u_bf, vt_bf, s2, e2, thr, e1)


def _final_kernel(x1_ref, ot_ref, gt_ref, fg_ref, y_ref):
    y_ref[...] = _rms(x1_ref[...] + gt_ref[...] * ot_ref[...].T, fg_ref[...])


def _final(x1, ot, gt2, final_g, grp, tm):
    t, d = x1.shape
    row = lambda i: (i, 0)
    return pl.pallas_call(
        _final_kernel,
        out_shape=jax.ShapeDtypeStruct((t, d), F32),
        grid=(t // tm,),
        in_specs=[pl.BlockSpec((tm, d), row), pl.BlockSpec((d, tm), lambda i: (0, i)),
                  pl.BlockSpec((None, 1, d), lambda i: (grp(i), 0, 0)),
                  pl.BlockSpec((1, d), lambda i: (0, 0))],
        out_specs=pl.BlockSpec((tm, d), row),
        compiler_params=_cparams(("parallel",)),
        name="final_norm",
    )(x1, ot, gt2, final_g)


def _rope_tables(seq, dk):
    rows = seq // GRID_W
    r, c = jnp.meshgrid(jnp.arange(rows, dtype=F32), jnp.arange(GRID_W, dtype=F32), indexing='ij')
    r = r.reshape(-1)
    c = c.reshape(-1)
    nf = dk // 4
    inv = ROPE_BASE ** (-jnp.arange(nf, dtype=F32) / nf)
    ang = jnp.concatenate([r[:, None] * inv, c[:, None] * inv], axis=-1)
    cos = jnp.concatenate([jnp.cos(ang), jnp.cos(ang)], axis=-1)
    sin = jnp.concatenate([-jnp.sin(ang), jnp.sin(ang)], axis=-1)
    return cos, sin


def _pad2(a, rows, cols):
    return jnp.pad(a, ((0, rows - a.shape[0]), (0, cols - a.shape[1])))


def kernel(x_prompt, x_sample, state_ret_fwd, state_ret_bwd, c, c_ctx, w_ada, b_ada, norm1_g, w_in, hy_conv_w, hy_conv_b, hy_w1, hy_b1, hy_w2, hy_b2, hy_w3, hy_b3, hy_w4, hy_freq, hy_bias, hy_norm_g, ret_decay_fwd, ret_decay_bwd, ret_norm_g, w_out, norm2_g, peer_wq, peer_keys, peer_u, peer_v, final_g):
    b_ctx, l_ctx, d = x_prompt.shape
    b_lat, l_lat, _ = x_sample.shape
    depth = w_ada.shape[0]
    assert depth == 1
    t_ctx, t_lat = b_ctx * l_ctx, b_lat * l_lat
    d_hy = hy_norm_g.shape[1]
    d_ret = ret_norm_g.shape[1]
    dk = d_ret // N_RET_HEADS
    tm = 512

    def grp_of(rows_per_block):
        nb_ctx = t_ctx // rows_per_block
        bpb = l_lat // rows_per_block
        return lambda i: jnp.where(i < nb_ctx, 0, 1 + (i - nb_ctx) // bpb)

    x_all = jnp.concatenate([x_prompt.reshape(t_ctx, d), x_sample.reshape(t_lat, d)], axis=0)
    ngrp = 1 + b_lat
    cond = jnp.concatenate([c_ctx[None, :], c], axis=0)
    cond8 = jnp.pad(cond, ((0, SUBLANES - ngrp), (0, 0)))
    mod = _ada(cond8, w_ada[0], b_ada[0])
    sh1, sc1, gt1, sh2, sc2, gt2 = [m.reshape(SUBLANES, 1, d) for m in jnp.split(mod, 6, axis=-1)]

    hy_in, rq, rk, rv, rg = _inproj(x_all, sh1, sc1, norm1_g, w_in[0].astype(BF16), grp_of(tm), tm, 3 * d_hy, d_ret)

    fw = hy_w1.shape[-1]
    f = jnp.linspace(1e-4, HY_BANDS - 1, HY_BANDS, dtype=F32)
    frow = jnp.zeros((1, LANES), F32).at[0, 1:1 + HY_BANDS].set(f).at[0, 1 + HY_BANDS:1 + 2 * HY_BANDS].set(f)
    min_decay = math.log(HY_DECAY_TARGET) / HY_SLOW_PCT
    max_decay = math.log(HY_DECAY_TARGET) / HY_FAST_PCT
    deltas = jnp.abs(jnp.linspace(min_decay, max_decay, d_hy, dtype=F32))
    w4 = hy_w4[0].reshape(fw, 2, 2 * d_hy).transpose(1, 0, 2)
    w4 = jnp.pad(w4, ((0, 0), (0, LANES - fw), (0, 0)))
    filt_w = (frow, _pad2(hy_w1[0], LANES, LANES), _pad2(hy_b1, 1, LANES), _pad2(hy_w2[0], LANES, LANES),
              _pad2(hy_b2, 1, LANES), _pad2(hy_w3[0], LANES, LANES), _pad2(hy_b3, 1, LANES),
              _pad2(hy_freq, 1, LANES), w4, jnp.tile(deltas, 2)[None, :])
    fwd_m, inv_m, flt_m = _dft_mats()
    spec_ctx = _hyena_spectra(l_ctx, filt_w, flt_m)
    spec_lat = _hyena_spectra(l_lat, filt_w, flt_m)
    yhy_ctx = _hyena_mix(hy_in, hy_conv_w[0], hy_conv_b, hy_bias[0], spec_ctx, fwd_m, inv_m, 0, b_ctx, l_ctx, 512)
    yhy_lat = _hyena_mix(hy_in, hy_conv_w[0], hy_conv_b, hy_bias[0], spec_lat, fwd_m, inv_m,
                         t_ctx // l_lat, b_lat, l_lat, 128)
    y_hy = jnp.concatenate([yhy_ctx, yhy_lat], axis=0)

    cos_t, sin_t = _rope_tables(l_lat, dk)
    dec_f = jnp.broadcast_to(ret_decay_fwd[0][:, None, None], (N_RET_HEADS, 1, dk))
    dec_b = jnp.broadcast_to(ret_decay_bwd[0][:, None, None], (N_RET_HEADS, 1, dk))
    ng = ret_norm_g[0].reshape(N_RET_HEADS, 1, dk)
    yr_ctx, sf_new, sb_new = _retention(rq, rk, rv, rg, cos_t, sin_t, dec_f, dec_b, ng, state_ret_fwd, state_ret_bwd,
                                        0, b_ctx, l_ctx, False, False)
    yr_lat, _, _ = _retention(rq, rk, rv, rg, cos_t, sin_t, dec_f, dec_b, ng, state_ret_fwd, state_ret_bwd,
                              t_ctx // l_lat, b_lat, l_lat, True, True)
    y_ret = jnp.concatenate([yr_ctx, yr_lat], axis=0)

    x1, h2t = _outproj(y_hy, y_ret, x_all, gt1, sh2, sc2, hy_norm_g, norm2_g, w_out[0].astype(BF16), grp_of(tm), tm)

    nh = PEER_HEADS
    wqt = peer_wq[0].T.astype(BF16)
    keys = peer_keys[0].reshape(2 * nh, N_KEYS, -1).astype(BF16)
    s2, e2, thr, e1 = _peer1(h2t, wqt, keys)
    ot = _peer2(h2t, peer_u[0].astype(BF16), peer_v[0].T.astype(BF16), s2, e2, thr, e1, 1024)
    y = _final(x1, ot, gt2, final_g.reshape(1, d), grp_of(tm), tm)
    y_prompt = y[:t_ctx].reshape(b_ctx, l_ctx, d)
    y_sample = y[t_ctx:].reshape(b_lat, l_lat, d)
    return (y_prompt, y_sample, sf_new, sb_new)
```

```python
import functools
import math

import numpy as np
import jax
import jax.numpy as jnp
from jax import lax
from jax.experimental import pallas as pl
from jax.experimental.pallas import tpu as pltpu

F32 = jnp.float32
BF16 = jnp.bfloat16

LANES = 128
SUBLANES = 8
VMEM_LIMIT = 56 * 1024 * 1024

EPS = 1e-6
GRID_W = 64
N_RET_HEADS = 4
RET_CHUNK = 128
ROPE_BASE = 10000.0
HY_EMB = 33
HY_BANDS = (HY_EMB - 1) // 2
HY_DECAY_TARGET = 1e-2
HY_FAST_PCT = 0.3
HY_SLOW_PCT = 1.5
N_KEYS = 128
PEER_HEADS = 8
PEER_TOPK = 16
FFT_P = 256
FFT_N = 2 * FFT_P
TOK_VREG = SUBLANES * LANES
TILE_PITCH = 136


def _cparams(sem, vmem=VMEM_LIMIT, flags=None):
    return pltpu.CompilerParams(dimension_semantics=sem, vmem_limit_bytes=vmem, flags=flags)


def _dot(a, b):
    return jnp.dot(a, b, preferred_element_type=F32)


def _split(a):
    hi = a.astype(BF16)
    lo = (a - hi.astype(F32)).astype(BF16)
    return hi, lo


def _dot3(a, b):
    ah, al = _split(a)
    bh, bl = _split(b)
    return _dot(ah, bh) + (_dot(al, bh) + _dot(ah, bl))


def _rms(x, g):
    return x * lax.rsqrt(jnp.mean(x * x, axis=-1, keepdims=True) + EPS) * g


def _ada_kernel(c_ref, w_ref, b_ref, o_ref):
    c = c_ref[...]
    s = c * jax.nn.sigmoid(c)
    o_ref[...] = _dot3(s, w_ref[...]) + b_ref[...]


def _ada(cond8, w_ada, b_ada):
    d, n = w_ada.shape
    tn = 1536
    return pl.pallas_call(
        _ada_kernel,
        out_shape=jax.ShapeDtypeStruct((cond8.shape[0], n), F32),
        grid=(n // tn,),
        in_specs=[pl.BlockSpec((cond8.shape[0], d), lambda j: (0, 0)),
                  pl.BlockSpec((d, tn), lambda j: (0, j)),
                  pl.BlockSpec((1, tn), lambda j: (0, j))],
        out_specs=pl.BlockSpec((cond8.shape[0], tn), lambda j: (0, j)),
        compiler_params=_cparams(("parallel",)),
        name="ada",
    )(cond8, w_ada, b_ada.reshape(1, n))


def _inproj_kernel(x_ref, sh_ref, sc_ref, g_ref, w_ref, hy_ref, q_ref, k_ref, v_ref, gg_ref, *, d_hy3, d_ret):
    h = _rms(x_ref[...], g_ref[...]) * (1.0 + sc_ref[...]) + sh_ref[...]
    hb = h.astype(BF16)
    hy_ref[...] = _dot(hb, w_ref[:, 0:d_hy3])
    for i, o_ref in enumerate((q_ref, k_ref, v_ref, gg_ref)):
        o_ref[...] = _dot(hb, w_ref[:, d_hy3 + i * d_ret:d_hy3 + (i + 1) * d_ret])


def _inproj(x, sh, sc, g, w_bf, grp, tm, d_hy3, d_ret):
    t, d = x.shape
    row = lambda i: (i, 0)
    modspec = pl.BlockSpec((None, 1, d), lambda i: (grp(i), 0, 0))
    outs = [jax.ShapeDtypeStruct((t, d_hy3), F32)] + [jax.ShapeDtypeStruct((t, d_ret), F32)] * 4
    return pl.pallas_call(
        functools.partial(_inproj_kernel, d_hy3=d_hy3, d_ret=d_ret),
        out_shape=outs,
        grid=(t // tm,),
        in_specs=[pl.BlockSpec((tm, d), row), modspec, modspec,
                  pl.BlockSpec((1, d), lambda i: (0, 0)),
                  pl.BlockSpec(w_bf.shape, lambda i: (0, 0))],
        out_specs=[pl.BlockSpec((tm, d_hy3), row)] + [pl.BlockSpec((tm, d_ret), row)] * 4,
        compiler_params=_cparams(("parallel",)),
        name="inproj",
    )(x, sh, sc, g, w_bf)


def _kf_kernel(frow_ref, w1_ref, b1_ref, w2_ref, b2_ref, w3_ref, b3_ref, fr_ref, w4_ref, dl_ref, o_ref, *, seq):
    r = pl.program_id(0)
    rows = o_ref.shape[0]
    i = r * rows + lax.broadcasted_iota(jnp.int32, (rows, LANES), 0)
    pos = jnp.abs(i - seq).astype(F32)
    t = pos / (seq - 1.0)
    ang = frow_ref[...] * ((2.0 * math.pi) * pos / seq)
    lane = lax.broadcasted_iota(jnp.int32, (rows, LANES), 1)
    z = jnp.where(lane == 0, t,
                  jnp.where(lane <= HY_BANDS, jnp.cos(ang),
                            jnp.where(lane <= 2 * HY_BANDS, -jnp.sin(ang), 0.0)))
    fr = fr_ref[...]
    h = jnp.sin(fr * (_dot3(z, w1_ref[...]) + b1_ref[...]))
    h = jnp.sin(fr * (_dot3(h, w2_ref[...]) + b2_ref[...]))
    h = jnp.sin(fr * (_dot3(h, w3_ref[...]) + b3_ref[...]))
    h4 = _dot3(h, w4_ref[...])
    nc = o_ref.shape[1]
    iw = r * rows + lax.broadcasted_iota(jnp.int32, (rows, nc), 0)
    tw = jnp.abs(iw - seq).astype(F32) / (seq - 1.0)
    win = jnp.exp(-tw * dl_ref[...])
    o_ref[...] = jnp.where(iw == 0, 0.0, h4 * win)


def _spec_kernel(k0_ref, k1_ref, a_ref, o_ref):
    a = a_ref[...]
    o_ref[...] = (_dot(a[:, :FFT_P], k0_ref[...].astype(BF16))
                  + _dot(a[:, FFT_P:], k1_ref[...].astype(BF16)))


def _dft_mats():
    f = np.arange(FFT_P)[:, None].astype(np.float64)
    t = np.arange(FFT_P)[None, :].astype(np.float64)
    j = np.arange(FFT_N)[None, :].astype(np.float64)
    w = 2.0 * np.pi / FFT_N
    fwd = np.concatenate([np.cos(w * f * t), -np.sin(w * f * t)], axis=0)
    fwd[FFT_P] = np.cos(np.pi * t[0])
    scale = np.full((1, FFT_P), 2.0 / FFT_N)
    scale[0, 0] = 1.0 / FFT_N
    inv = np.concatenate([np.cos(w * t.T * f.T) * scale, -np.sin(w * t.T * f.T) * scale], axis=1)
    inv[:, FFT_P] = np.cos(np.pi * t[0]) / FFT_N
    sgn = np.where(np.arange(FFT_P) % 2 == 0, 1.0, -1.0)[:, None]
    re = sgn * np.cos(w * f * j)
    im = -sgn * np.sin(w * f * j)
    nyq = np.cos(np.pi * j[0])
    re2 = re.copy()
    re2[0] = nyq
    im[0] = 0.0
    flt = np.concatenate([re, re2, im], axis=0)
    flt[:, 0] = 0.0
    return (jnp.asarray(fwd, F32).astype(BF16), jnp.asarray(inv, F32).astype(BF16),
            jnp.asarray(flt, F32).astype(BF16))


def _hyena_spectra(seq, fw, flt_mat):
    frow, w1, b1, w2, b2, w3, b3, fr, w4, dl = fw
    nblk = 2 * seq // FFT_P
    nc = w4.shape[-1]
    full = lambda a: pl.BlockSpec(a.shape, lambda r: (0,) * a.ndim)
    kf = pl.pallas_call(
        functools.partial(_kf_kernel, seq=seq),
        out_shape=jax.ShapeDtypeStruct((2 * seq, nc), F32),
        grid=(nblk,),
        in_specs=[full(frow), full(w1), full(b1), full(w2), full(b2), full(w3), full(b3), full(fr),
                  pl.BlockSpec((None,) + w4.shape[1:], lambda r: (jnp.where(r >= nblk // 2, 0, 1), 0, 0)),
                  full(dl)],
        out_specs=pl.BlockSpec((FFT_P, nc), lambda r: (r, 0)),
        compiler_params=_cparams(("parallel",)),
        name="hyena_filter",
    )(frow, w1, b1, w2, b2, w3, b3, fr, w4, dl)
    return pl.pallas_call(
        _spec_kernel,
        out_shape=jax.ShapeDtypeStruct((nblk - 1, 3 * FFT_P, nc), F32),
        grid=(nblk - 1,),
        in_specs=[pl.BlockSpec((FFT_P, nc), lambda w: (w, 0)),
                  pl.BlockSpec((FFT_P, nc), lambda w: (w + 1, 0)),
                  pl.BlockSpec(flt_mat.shape, lambda w: (0, 0))],
        out_specs=pl.BlockSpec((None, 3 * FFT_P, nc), lambda w: (w, 0, 0)),
        compiler_params=_cparams(("parallel",)),
        name="hyena_spectra",
    )(kf, kf, flt_mat)


def _hy_kernel(u_ref, x_ref, cwu_ref, cbu_ref, cwx_ref, cbx_ref, bias_ref, g_ref, fwd_ref, inv_ref,
               o_ref, u_scr, gate_scr, uf_scr, y_scr, *, n, conv_u):
    seq = n * FFT_P
    cb = o_ref.shape[1]

    def sconv(v, w_ref, b_ref):
        row = lax.broadcasted_iota(jnp.int32, v.shape, 0)
        prev = jnp.where(row == 0, 0.0, pltpu.roll(v, 1, 0))
        nxt = jnp.where(row == seq - 1, 0.0, pltpu.roll(v, seq - 1, 0))
        return b_ref[...] + prev * w_ref[0:1, :] + v * w_ref[1:2, :] + nxt * w_ref[2:3, :]

    u_scr[...] = sconv(u_ref[...], cwu_ref, cbu_ref) if conv_u else u_ref[...]
    gate_scr[...] = sconv(x_ref[...], cwx_ref, cbx_ref)
    fwd = fwd_ref[...]
    for b in range(n):
        uf_scr[b] = _dot(fwd, u_scr[b * FFT_P:(b + 1) * FFT_P, :].astype(BF16))

    rc = 32

    def a_body(a, carry):
        for c in range(FFT_P // rc):
            r0 = c * rc

            def b_body(b, acc):
                yre, yim = acc
                w = a - b + (n - 1)
                gre = g_ref[w, pl.ds(r0, rc), :]
                gre2 = g_ref[w, pl.ds(FFT_P + r0, rc), :]
                gim = g_ref[w, pl.ds(2 * FFT_P + r0, rc), :]
                ure = uf_scr[b, pl.ds(r0, rc), :]
                uim = uf_scr[b, pl.ds(FFT_P + r0, rc), :]
                return yre + gre * ure - gim * uim, yim + gre2 * uim + gim * ure

            zero = jnp.zeros((rc, cb), F32)
            yre, yim = lax.fori_loop(0, n, b_body, (zero, zero))
            y_scr[pl.ds(r0, rc), :] = yre
            y_scr[pl.ds(FFT_P + r0, rc), :] = yim
        y = _dot(inv_ref[...], y_scr[...].astype(BF16))
        rows = pl.ds(pl.multiple_of(a * FFT_P, FFT_P), FFT_P)
        ua = u_scr[rows, :]
        o_ref[rows, :] = gate_scr[rows, :] * (y + ua * bias_ref[...])
        return carry

    lax.fori_loop(0, n, a_body, 0)


def _hyena_order(u_arr, u_row0, u_col0, x_arr, x_row0, x_col0, conv_w, conv_b, bias_row, spectra, spec_col0,
                 fwd, inv, nbatch, seq, cb, conv_u):
    n = seq // FFT_P
    d_hy = bias_row.shape[1]
    ncb = d_hy // cb
    nwin = spectra.shape[0]
    grid = (ncb, nbatch)
    in_specs = [
        pl.BlockSpec((seq, cb), lambda c, b: (u_row0 + b, u_col0 + c)),
        pl.BlockSpec((seq, cb), lambda c, b: (x_row0 + b, x_col0 + c)),
        pl.BlockSpec((3, cb), lambda c, b: (0, u_col0 + c)),
        pl.BlockSpec((1, cb), lambda c, b: (0, u_col0 + c)),
        pl.BlockSpec((3, cb), lambda c, b: (0, x_col0 + c)),
        pl.BlockSpec((1, cb), lambda c, b: (0, x_col0 + c)),
        pl.BlockSpec((1, cb), lambda c, b: (0, c)),
        pl.BlockSpec((nwin, 3 * FFT_P, cb), lambda c, b: (0, 0, spec_col0 + c)),
        pl.BlockSpec(fwd.shape, lambda c, b: (0, 0)),
        pl.BlockSpec(inv.shape, lambda c, b: (0, 0)),
    ]
    return pl.pallas_call(
        functools.partial(_hy_kernel, n=n, conv_u=conv_u),
        out_shape=jax.ShapeDtypeStruct((nbatch * seq, d_hy), F32),
        grid=grid,
        in_specs=in_specs,
        out_specs=pl.BlockSpec((seq, cb), lambda c, b: (b, c)),
        scratch_shapes=[pltpu.VMEM((seq, cb), F32), pltpu.VMEM((seq, cb), F32),
                        pltpu.VMEM((n, FFT_N, cb), F32), pltpu.VMEM((FFT_N, cb), F32)],
        compiler_params=_cparams(("parallel", "parallel")),
        name="hyena_conv",
    )(u_arr, x_arr, conv_w, conv_b, conv_w, conv_b, bias_row, spectra, fwd, inv)


def _hyena_mix(hy_in, conv_w, conv_b, bias, spectra, fwd, inv, row0, nbatch, seq, cb):
    ncb = bias.shape[1] // cb
    z1 = _hyena_order(hy_in, row0, 0, hy_in, row0, ncb, conv_w, conv_b, bias[0:1], spectra, 0,
                      fwd, inv, nbatch, seq, cb, True)
    return _hyena_order(z1, 0, 0, hy_in, row0, 2 * ncb, conv_w, conv_b, bias[1:2], spectra, ncb,
                        fwd, inv, nbatch, seq, cb, False)


def _ret_kernel(q_ref, k_ref, v_ref, g_ref, cos_ref, sin_ref, df_ref, db_ref, ng_ref, s0f_ref, s0b_ref,
                y_ref, sf_ref, sb_ref, q_scr, k_scr, sb_scr, *, n, rope, has_state):
    c = RET_CHUNK
    dk = q_ref.shape[1]
    lgf = -jnp.exp(df_ref[...])
    lgb = -jnp.exp(db_ref[...])
    ri = lax.broadcasted_iota(jnp.int32, (c, c), 0)
    ci = lax.broadcasted_iota(jnp.int32, (c, c), 1)
    diff = (ri - ci).astype(F32)
    mask = (jnp.where(diff >= 0, jnp.exp(jnp.maximum(diff, 0.0) * lgf), 0.0)
            + jnp.where(diff <= 0, jnp.exp(jnp.maximum(-diff, 0.0) * lgb), 0.0))
    pos = lax.broadcasted_iota(jnp.int32, (c, dk), 0).astype(F32)
    xi_f = jnp.exp((pos + 1.0) * lgf)
    ze_f = jnp.exp((c - 1.0 - pos) * lgf)
    cd_f = jnp.exp(c * lgf)
    xi_b = jnp.exp((c - pos) * lgb)
    ze_b = jnp.exp(pos * lgb)
    cd_b = jnp.exp(c * lgb)

    q = q_ref[...]
    k = k_ref[...]
    if rope:
        cs = cos_ref[...]
        sn = sin_ref[...]
        q = q * cs + pltpu.roll(q, dk // 2, 1) * sn
        k = k * cs + pltpu.roll(k, dk // 2, 1) * sn
    q_scr[...] = q
    k_scr[...] = k * (dk ** -0.5)

    def dot_tn(a, b):
        return lax.dot_general(a, b, (((0,), (0,)), ((), ())), preferred_element_type=F32)

    def chunk(i):
        return pl.ds(pl.multiple_of(i * c, c), c)

    def bwd_body(j, s):
        i = n - 1 - j
        sb_scr[i] = s
        rows = chunk(i)
        kz = (k_scr[rows, :] * ze_b).astype(BF16)
        return s * cd_b + dot_tn(kz, v_ref[rows, :].astype(BF16))

    s0b = s0b_ref[...] if has_state else jnp.zeros((dk, dk), F32)
    sb_ref[...] = lax.fori_loop(0, n, bwd_body, s0b)

    def fwd_body(i, s):
        rows = chunk(i)
        qb = q_scr[rows, :].astype(BF16)
        kc = k_scr[rows, :]
        vb = v_ref[rows, :].astype(BF16)
        sc = lax.dot_general(qb, kc.astype(BF16), (((1,), (1,)), ((), ())), preferred_element_type=F32)
        o = _dot((sc * mask).astype(BF16), vb)
        o = o + _dot(qb, s.astype(BF16)) * xi_f + _dot(qb, sb_scr[i].astype(BF16)) * xi_b
        gt = g_ref[rows, :]
        y_ref[rows, :] = _rms(o, ng_ref[...]) * (gt * jax.nn.sigmoid(gt))
        return s * cd_f + dot_tn((kc * ze_f).astype(BF16), vb)

    s0f = s0f_ref[...] if has_state else jnp.zeros((dk, dk), F32)
    sf_ref[...] = lax.fori_loop(0, n, fwd_body, s0f)


def _retention(q, k, v, g, cos, sin, dec_f, dec_b, norm_g, s0f, s0b, row_blk0, nbatch, seq, rope, has_state):
    nh = N_RET_HEADS
    dk = q.shape[1] // nh
    n = seq // RET_CHUNK
    tok = lambda b, h: (row_blk0 + b, h)
    tspec = pl.BlockSpec((seq, dk), tok)
    rspec = pl.BlockSpec((seq, dk), lambda b, h: (0, 0))
    hspec = pl.BlockSpec((None, 1, dk), lambda b, h: (h, 0, 0))
    sspec = pl.BlockSpec((None, None, None, dk, dk), lambda b, h: (b, 0, h, 0, 0))
    s0spec = sspec if has_state else pl.BlockSpec((None, None, None, dk, dk), lambda b, h: (0, 0, h, 0, 0))
    st_shape = jax.ShapeDtypeStruct((nbatch, 1, nh, dk, dk), F32)
    return pl.pallas_call(
        functools.partial(_ret_kernel, n=n, rope=rope, has_state=has_state),
        out_shape=[jax.ShapeDtypeStruct((nbatch * seq, nh * dk), F32), st_shape, st_shape],
        grid=(nbatch, nh),
        in_specs=[tspec, tspec, tspec, tspec, rspec, rspec, hspec, hspec, hspec, s0spec, s0spec],
        out_specs=[pl.BlockSpec((seq, dk), lambda b, h: (b, h)), sspec, sspec],
        scratch_shapes=[pltpu.VMEM((seq, dk), F32), pltpu.VMEM((seq, dk), F32),
                        pltpu.VMEM((n, dk, dk), F32)],
        compiler_params=_cparams(("parallel", "parallel")),
        name="retention",
    )(q, k, v, g, cos, sin, dec_f, dec_b, norm_g, s0f, s0b)


def _outproj_kernel(yh_ref, yr_ref, x_ref, gt_ref, sh_ref, sc_ref, hg_ref, ng_ref, wo_ref, x1_ref, h2t_ref):
    d_hy = yh_ref.shape[1]
    nh = _rms(yh_ref[...], hg_ref[...]).astype(BF16)
    y = _dot(nh, wo_ref[0:d_hy, :]) + _dot(yr_ref[...].astype(BF16), wo_ref[d_hy:, :])
    x1 = x_ref[...] + gt_ref[...] * y
    x1_ref[...] = x1
    h2 = _rms(x1, ng_ref[...]) * (1.0 + sc_ref[...]) + sh_ref[...]
    h2t_ref[...] = h2.T.astype(BF16)


def _outproj(y_hy, y_ret, x, gt1, sh2, sc2, hy_g, n2_g, wo_bf, grp, tm):
    t, d = x.shape
    row = lambda i: (i, 0)
    modspec = pl.BlockSpec((None, 1, d), lambda i: (grp(i), 0, 0))
    return pl.pallas_call(
        _outproj_kernel,
        out_shape=[jax.ShapeDtypeStruct((t, d), F32), jax.ShapeDtypeStruct((d, t), BF16)],
        grid=(t // tm,),
        in_specs=[pl.BlockSpec((tm, y_hy.shape[1]), row), pl.BlockSpec((tm, y_ret.shape[1]), row),
                  pl.BlockSpec((tm, d), row), modspec, modspec, modspec,
                  pl.BlockSpec((1, y_hy.shape[1]), lambda i: (0, 0)),
                  pl.BlockSpec((1, d), lambda i: (0, 0)),
                  pl.BlockSpec(wo_bf.shape, lambda i: (0, 0))],
        out_specs=[pl.BlockSpec((tm, d), row), pl.BlockSpec((d, tm), lambda i: (0, i))],
        compiler_params=_cparams(("parallel",)),
        name="outproj",
    )(y_hy, y_ret, x, gt1, sh2, sc2, hy_g, n2_g, wo_bf)


def _sort_desc(xs):
    xs = list(xs)
    n = len(xs)
    k = 2
    while k <= n:
        j = k // 2
        while j >= 1:
            for i in range(n):
                l = i ^ j
                if l > i:
                    hi, lo = jnp.maximum(xs[i], xs[l]), jnp.minimum(xs[i], xs[l])
                    xs[i], xs[l] = (hi, lo) if (i & k) == 0 else (lo, hi)
            j //= 2
        k *= 2
    return xs


def _merge_top(a, b):
    n = len(a)
    xs = [jnp.maximum(a[i], b[n - 1 - i]) for i in range(n)]
    j = n // 2
    while j >= 1:
        for i in range(n):
            l = i ^ j
            if l > i:
                xs[i], xs[l] = jnp.maximum(xs[i], xs[l]), jnp.minimum(xs[i], xs[l])
        j //= 2
    return xs


def _top_sorted(load, count, k):
    acc = None
    for g0 in range(0, count, k):
        grp = _sort_desc([load(i) for i in range(g0, g0 + k)])
        acc = grp if acc is None else _merge_top(acc, grp)
    return acc


def _staircase(k):
    return [(a, b) for a in range(k) for b in range(k) if (a + 1) * (b + 1) <= k + 1]


def _twin_bf16(x):
    bits = pltpu.bitcast(x.astype(BF16).astype(F32), jnp.uint32)
    return bits | lax.shift_right_logical(bits, jnp.uint32(16))


def _peer1_kernel(h2t_ref, wq_ref, keys_ref, r2_ref, e2_ref, cnt_ref, e1_ref, q_scr, s1_scr, s2_scr):
    h = pl.program_id(1)
    lt_n = r2_ref.shape[0]
    nk = N_KEYS
    kk = PEER_TOPK

    @pl.when(h == 0)
    def _():
        q_scr[...] = _dot(wq_ref[...], h2t_ref[...]).astype(BF16)

    dq = q_scr.shape[0] // (2 * PEER_HEADS)
    for p, scr in ((0, s1_scr), (1, s2_scr)):
        rows = pl.ds(pl.multiple_of((2 * h + p) * dq, dq), dq)
        s = _dot(keys_ref[2 * h + p], q_scr[rows, :])
        for lt in range(lt_n):
            scr[lt * TILE_PITCH:lt * TILE_PITCH + nk, :] = s[:, lt * LANES:(lt + 1) * LANES]

    ld1 = lambda i: s1_scr[pl.ds(i, lt_n, stride=TILE_PITCH), :]
    ld2 = lambda i: s2_scr[pl.ds(i, lt_n, stride=TILE_PITCH), :]
    v1 = _top_sorted(ld1, nk, kk)
    v2 = _top_sorted(ld2, nk, kk)
    pairs = _staircase(kk)
    cand = [v1[a] + v2[b] for a, b in pairs]
    neg = jnp.full_like(cand[0], -jnp.inf)
    padded = cand + [neg] * (-len(cand) % kk)
    top = _top_sorted(lambda i: padded[i], len(padded), kk)
    tau = top[kk - 1]
    nxt = neg
    for cnd in cand:
        nxt = jnp.maximum(nxt, jnp.where(cnd < tau, cnd, neg))
    thr = 0.5 * (tau + nxt)
    m = v1[0] + v2[0]
    z = jnp.zeros_like(tau)
    for cnd in cand:
        z = z + jnp.where(cnd >= tau, jnp.exp(cnd - m), 0.0)
    inv_z = 1.0 / z

    for i in range(nk):
        x1 = ld1(i)
        gap = thr - x1
        cnt = jnp.zeros_like(x1)
        for b in range(kk):
            cnt = cnt + jnp.where(v2[b] > gap, 1.0, 0.0)
        keep = x1 >= v1[kk - 1]
        cnt_ref[i] = _twin_bf16(jnp.where(keep, cnt, 0.0))
        e1_ref[i] = _twin_bf16(jnp.where(keep, jnp.exp(x1 - v1[0]) * inv_z, 0.0))
    for lt in range(lt_n):
        s2 = s2_scr[lt * TILE_PITCH:lt * TILE_PITCH + nk, :]
        rank = jnp.zeros_like(s2)
        for b in range(kk):
            rank = rank + jnp.where(v2[b][lt:lt + 1, :] > s2, 1.0, 0.0)
        r2_ref[lt] = rank.astype(BF16)
        e2_ref[lt] = jnp.where(s2 >= v2[kk - 1][lt:lt + 1, :], jnp.exp(s2 - v2[0][lt:lt + 1, :]), 0.0).astype(BF16)


def _peer1(h2t, wqt_bf, keys_bf):
    d, t = h2t.shape
    nh = PEER_HEADS
    tb = TOK_VREG
    lt_n = tb // LANES
    tile_shape = jax.ShapeDtypeStruct((nh, t // LANES, N_KEYS, LANES), BF16)
    vreg_shape = jax.ShapeDtypeStruct((nh, t // tb, N_KEYS, lt_n, LANES), jnp.uint32)
    tile_spec = pl.BlockSpec((None, lt_n, N_KEYS, LANES), lambda i, h: (h, i, 0, 0))
    vreg_spec = pl.BlockSpec((None, None, N_KEYS, lt_n, LANES), lambda i, h: (h, i, 0, 0, 0))
    return pl.pallas_call(
        _peer1_kernel,
        out_shape=[tile_shape, tile_shape, vreg_shape, vreg_shape],
        grid=(t // tb, nh),
        in_specs=[pl.BlockSpec((d, tb), lambda i, h: (0, i)),
                  pl.BlockSpec(wqt_bf.shape, lambda i, h: (0, 0)),
                  pl.BlockSpec(keys_bf.shape, lambda i, h: (0, 0, 0))],
        out_specs=[tile_spec, tile_spec, vreg_spec, vreg_spec],
        scratch_shapes=[pltpu.VMEM((wqt_bf.shape[0], tb), BF16),
                        pltpu.VMEM((lt_n * TILE_PITCH, LANES), F32),
                        pltpu.VMEM((lt_n * TILE_PITCH, LANES), F32)],
        compiler_params=_cparams(("parallel", "arbitrary")),
        name="peer_select",
    )(h2t, wqt_bf, keys_bf)


def _gelu_tanh(x):
    c = math.sqrt(2.0 / math.pi)
    return 0.5 * x * (1.0 + jnp.tanh(c * (x + 0.044715 * (x * x * x))))


BF16_ROWS = 2 * SUBLANES


def _peer2_kernel(h2t_in, u_ref, vt_ref, r2_in, e2_in, cnt_ref, e1_ref, ot_ref,
                  at_scr, zt_scr, r2_ref, e2_ref, h2t_ref):
    eb = pl.program_id(1)
    lt_n = r2_in.shape[1]
    ig_n = cnt_ref.shape[1]

    @pl.when(eb == 0)
    def _():
        ot_ref[...] = jnp.zeros_like(ot_ref)
        r2_ref[...] = r2_in[...]
        e2_ref[...] = e2_in[...]
        h2t_ref[...] = h2t_in[...]

    tcw = at_scr.shape[2]
    ntc = lt_n * LANES // tcw
    lt_per = tcw // LANES

    def row_pair(ref, hh, ig, lt):
        word = jnp.broadcast_to(ref[hh, ig, lt:lt + 1, :], (SUBLANES, LANES))
        return pltpu.bitcast(word, BF16)

    def scores(tc):
        at_scr[tc % 2] = _dot(u_ref[...], h2t_ref[:, tc * tcw:(tc + 1) * tcw])

    def gates(tc):
        for ig in range(ig_n):
            for l in range(lt_per):
                lt = tc * lt_per + l
                cnt = [row_pair(cnt_ref, hh, ig, lt) for hh in range(PEER_HEADS)]
                e1 = [row_pair(e1_ref, hh, ig, lt) for hh in range(PEER_HEADS)]
                for c in range(N_KEYS // BF16_ROWS):
                    js = slice(c * BF16_ROWS, (c + 1) * BF16_ROWS)
                    w = jnp.zeros((BF16_ROWS, LANES), BF16)
                    for hh in range(PEER_HEADS):
                        w = w + jnp.where(r2_ref[hh, lt, js, :] < cnt[hh], e2_ref[hh, lt, js, :], 0.0) * e1[hh]
                    rows = slice(ig * N_KEYS + c * BF16_ROWS, ig * N_KEYS + (c + 1) * BF16_ROWS)
                    a = at_scr[tc % 2, rows, l * LANES:(l + 1) * LANES].astype(BF16)
                    zt_scr[rows, lt * LANES:(lt + 1) * LANES] = w * _gelu_tanh(a)

    def combine(tc):
        cols = slice(tc * tcw, (tc + 1) * tcw)
        ot_ref[:, cols] += _dot(vt_ref[...], zt_scr[:, cols])

    scores(0)
    for tc in range(ntc):
        if tc + 1 < ntc:
            scores(tc + 1)
        gates(tc)
        combine(tc)


def _peer2(h2t, u_bf, vt_bf, r2, e2, cnt, e1, eblk):
    d, t = h2t.shape
    ne = u_bf.shape[0]
    nh = PEER_HEADS
    tb = TOK_VREG
    lt_n = tb // LANES
    ig_n = eblk // N_KEYS
    tile_spec = pl.BlockSpec((nh, lt_n, N_KEYS, LANES), lambda i, e: (0, i, 0, 0))
    vreg_spec = pl.BlockSpec((nh, None, ig_n, lt_n, LANES), lambda i, e: (0, i, e, 0, 0))
    return pl.pallas_call(
        _peer2_kernel,
        out_shape=jax.ShapeDtypeStruct((d, t), F32),
        grid=(t // tb, ne // eblk),
        in_specs=[pl.BlockSpec((d, tb), lambda i, e: (0, i)),
                  pl.BlockSpec((eblk, d), lambda i, e: (e, 0)),
                  pl.BlockSpec((d, eblk), lambda i, e: (0, e)),
                  tile_spec, tile_spec, vreg_spec, vreg_spec],
        out_specs=pl.BlockSpec((d, tb), lambda i, e: (0, i)),
        scratch_shapes=[pltpu.VMEM((2, eblk, 2 * LANES), F32), pltpu.VMEM((eblk, tb), BF16),
                        pltpu.VMEM((nh, lt_n, N_KEYS, LANES), BF16), pltpu.VMEM((nh, lt_n, N_KEYS, LANES), BF16),
                        pltpu.VMEM((d, tb), BF16)],
        compiler_params=_cparams(("parallel", "arbitrary")),
        name="peer_experts",
    )(h2t, u_bf, vt_bf, r2, e2, cnt, e1)


def _final_kernel(x1_ref, ot_ref, gt_ref, fg_ref, y_ref):
    y_ref[...] = _rms(x1_ref[...] + gt_ref[...] * ot_ref[...].T, fg_ref[...])


def _final(x1, ot, gt2, final_g, grp, tm):
    t, d = x1.shape
    row = lambda i: (i, 0)
    return pl.pallas_call(
        _final_kernel,
        out_shape=jax.ShapeDtypeStruct((t, d), F32),
        grid=(t // tm,),
        in_specs=[pl.BlockSpec((tm, d), row), pl.BlockSpec((d, tm), lambda i: (0, i)),
                  pl.BlockSpec((None, 1, d), lambda i: (grp(i), 0, 0)),
                  pl.BlockSpec((1, d), lambda i: (0, 0))],
        out_specs=pl.BlockSpec((tm, d), row),
        compiler_params=_cparams(("parallel",)),
        name="final_norm",
    )(x1, ot, gt2, final_g)


def _rope_tables(seq, dk):
    rows = seq // GRID_W
    r, c = jnp.meshgrid(jnp.arange(rows, dtype=F32), jnp.arange(GRID_W, dtype=F32), indexing='ij')
    r = r.reshape(-1)
    c = c.reshape(-1)
    nf = dk // 4
    inv = ROPE_BASE ** (-jnp.arange(nf, dtype=F32) / nf)
    ang = jnp.concatenate([r[:, None] * inv, c[:, None] * inv], axis=-1)
    cos = jnp.concatenate([jnp.cos(ang), jnp.cos(ang)], axis=-1)
    sin = jnp.concatenate([-jnp.sin(ang), jnp.sin(ang)], axis=-1)
    return cos, sin


def _pad2(a, rows, cols):
    return jnp.pad(a, ((0, rows - a.shape[0]), (0, cols - a.shape[1])))


def kernel(x_prompt, x_sample, state_ret_fwd, state_ret_bwd, c, c_ctx, w_ada, b_ada, norm1_g, w_in, hy_conv_w, hy_conv_b, hy_w1, hy_b1, hy_w2, hy_b2, hy_w3, hy_b3, hy_w4, hy_freq, hy_bias, hy_norm_g, ret_decay_fwd, ret_decay_bwd, ret_norm_g, w_out, norm2_g, peer_wq, peer_keys, peer_u, peer_v, final_g):
    b_ctx, l_ctx, d = x_prompt.shape
    b_lat, l_lat, _ = x_sample.shape
    depth = w_ada.shape[0]
    assert depth == 1
    t_ctx, t_lat = b_ctx * l_ctx, b_lat * l_lat
    d_hy = hy_norm_g.shape[1]
    d_ret = ret_norm_g.shape[1]
    dk = d_ret // N_RET_HEADS
    tm = 512

    ngrp = 1 + b_lat
    cond = jnp.concatenate([c_ctx[None, :], c], axis=0)
    cond8 = jnp.pad(cond, ((0, SUBLANES - ngrp), (0, 0)))
    mod = _ada(cond8, w_ada[0], b_ada[0])
    sh1, sc1, gt1, sh2, sc2, gt2 = [m.reshape(SUBLANES, 1, d) for m in jnp.split(mod, 6, axis=-1)]

    fw = hy_w1.shape[-1]
    f = jnp.linspace(1e-4, HY_BANDS - 1, HY_BANDS, dtype=F32)
    frow = jnp.zeros((1, LANES), F32).at[0, 1:1 + HY_BANDS].set(f).at[0, 1 + HY_BANDS:1 + 2 * HY_BANDS].set(f)
    min_decay = math.log(HY_DECAY_TARGET) / HY_SLOW_PCT
    max_decay = math.log(HY_DECAY_TARGET) / HY_FAST_PCT
    deltas = jnp.abs(jnp.linspace(min_decay, max_decay, d_hy, dtype=F32))
    w4 = hy_w4[0].reshape(fw, 2, 2 * d_hy).transpose(1, 0, 2)
    w4 = jnp.pad(w4, ((0, 0), (0, LANES - fw), (0, 0)))
    filt_w = (frow, _pad2(hy_w1[0], LANES, LANES), _pad2(hy_b1, 1, LANES), _pad2(hy_w2[0], LANES, LANES),
              _pad2(hy_b2, 1, LANES), _pad2(hy_w3[0], LANES, LANES), _pad2(hy_b3, 1, LANES),
              _pad2(hy_freq, 1, LANES), w4, jnp.tile(deltas, 2)[None, :])
    fwd_m, inv_m, flt_m = _dft_mats()
    cos_t, sin_t = _rope_tables(l_lat, dk)
    dec_f = jnp.broadcast_to(ret_decay_fwd[0][:, None, None], (N_RET_HEADS, 1, dk))
    dec_b = jnp.broadcast_to(ret_decay_bwd[0][:, None, None], (N_RET_HEADS, 1, dk))
    ng = ret_norm_g[0].reshape(N_RET_HEADS, 1, dk)
    w_in_bf = w_in[0].astype(BF16)
    w_out_bf = w_out[0].astype(BF16)
    wqt = peer_wq[0].T.astype(BF16)
    keys = peer_keys[0].reshape(2 * PEER_HEADS, N_KEYS, -1).astype(BF16)
    u_bf = peer_u[0].astype(BF16)
    vt_bf = peer_v[0].T.astype(BF16)
    fg = final_g.reshape(1, d)

    def trunk(x, nbatch, seq, latent, hy_cb):
        grp = (lambda i: 1 + i // (seq // tm)) if latent else (lambda i: 0)
        hy_in, rq, rk, rv, rg = _inproj(x, sh1, sc1, norm1_g, w_in_bf, grp, tm, 3 * d_hy, d_ret)
        spectra = _hyena_spectra(seq, filt_w, flt_m)
        y_hy = _hyena_mix(hy_in, hy_conv_w[0], hy_conv_b, hy_bias[0], spectra, fwd_m, inv_m, 0, nbatch, seq, hy_cb)
        y_ret, s_f, s_b = _retention(rq, rk, rv, rg, cos_t, sin_t, dec_f, dec_b, ng, state_ret_fwd, state_ret_bwd,
                                     0, nbatch, seq, latent, latent)
        x1, h2t = _outproj(y_hy, y_ret, x, gt1, sh2, sc2, hy_norm_g, norm2_g, w_out_bf, grp, tm)
        r2, e2, cnt, e1 = _peer1(h2t, wqt, keys)
        ot = _peer2(h2t, u_bf, vt_bf, r2, e2, cnt, e1, 1024)
        y = _final(x1, ot, gt2, fg, grp, tm)
        return y.reshape(nbatch, seq, d), s_f, s_b

    y_prompt, sf_new, sb_new = trunk(x_prompt.reshape(t_ctx, d), b_ctx, l_ctx, False, 512)
    y_sample, _, _ = trunk(x_sample.reshape(t_lat, d), b_lat, l_lat, True, 128)
    return (y_prompt, y_sample, sf_new, sb_new)
```

```python
import functools
import math

import numpy as np
import jax
import jax.numpy as jnp
from jax import lax
from jax.experimental import pallas as pl
from jax.experimental.pallas import tpu as pltpu

F32 = jnp.float32
BF16 = jnp.bfloat16

LANES = 128
SUBLANES = 8
VMEM_LIMIT = 56 * 1024 * 1024

EPS = 1e-6
GRID_W = 64
N_RET_HEADS = 4
RET_CHUNK = 128
ROPE_BASE = 10000.0
HY_EMB = 33
HY_BANDS = (HY_EMB - 1) // 2
HY_DECAY_TARGET = 1e-2
HY_FAST_PCT = 0.3
HY_SLOW_PCT = 1.5
N_KEYS = 128
PEER_HEADS = 8
PEER_TOPK = 16
FFT_P = 256
FFT_N = 2 * FFT_P
TOK_VREG = SUBLANES * LANES
TILE_PITCH = 136


def _cparams(sem, vmem=VMEM_LIMIT, flags=None):
    return pltpu.CompilerParams(dimension_semantics=sem, vmem_limit_bytes=vmem, flags=flags)


def _dot(a, b):
    return jnp.dot(a, b, preferred_element_type=F32)


def _split(a):
    hi = a.astype(BF16)
    lo = (a - hi.astype(F32)).astype(BF16)
    return hi, lo


def _dot3(a, b):
    ah, al = _split(a)
    bh, bl = _split(b)
    return _dot(ah, bh) + (_dot(al, bh) + _dot(ah, bl))


def _rms(x, g):
    return x * lax.rsqrt(jnp.mean(x * x, axis=-1, keepdims=True) + EPS) * g


def _ada_kernel(c_ref, w_ref, b_ref, o_ref):
    c = c_ref[...]
    s = c * jax.nn.sigmoid(c)
    o_ref[...] = _dot3(s, w_ref[...]) + b_ref[...]


def _ada(cond8, w_ada, b_ada):
    d, n = w_ada.shape
    tn = 1536
    return pl.pallas_call(
        _ada_kernel,
        out_shape=jax.ShapeDtypeStruct((cond8.shape[0], n), F32),
        grid=(n // tn,),
        in_specs=[pl.BlockSpec((cond8.shape[0], d), lambda j: (0, 0)),
                  pl.BlockSpec((d, tn), lambda j: (0, j)),
                  pl.BlockSpec((1, tn), lambda j: (0, j))],
        out_specs=pl.BlockSpec((cond8.shape[0], tn), lambda j: (0, j)),
        compiler_params=_cparams(("parallel",)),
        name="ada",
    )(cond8, w_ada, b_ada.reshape(1, n))


def _inproj_kernel(x_ref, sh_ref, sc_ref, g_ref, w_ref, hy_ref, q_ref, k_ref, v_ref, gg_ref, *, d_hy3, d_ret):
    h = _rms(x_ref[...], g_ref[...]) * (1.0 + sc_ref[...]) + sh_ref[...]
    hb = h.astype(BF16)
    hy_ref[...] = _dot(hb, w_ref[:, 0:d_hy3])
    for i, o_ref in enumerate((q_ref, k_ref, v_ref, gg_ref)):
        o_ref[...] = _dot(hb, w_ref[:, d_hy3 + i * d_ret:d_hy3 + (i + 1) * d_ret])


def _inproj(x, sh, sc, g, w_bf, grp, tm, d_hy3, d_ret):
    t, d = x.shape
    row = lambda i: (i, 0)
    modspec = pl.BlockSpec((None, 1, d), lambda i: (grp(i), 0, 0))
    outs = [jax.ShapeDtypeStruct((t, d_hy3), F32)] + [jax.ShapeDtypeStruct((t, d_ret), F32)] * 4
    return pl.pallas_call(
        functools.partial(_inproj_kernel, d_hy3=d_hy3, d_ret=d_ret),
        out_shape=outs,
        grid=(t // tm,),
        in_specs=[pl.BlockSpec((tm, d), row), modspec, modspec,
                  pl.BlockSpec((1, d), lambda i: (0, 0)),
                  pl.BlockSpec(w_bf.shape, lambda i: (0, 0))],
        out_specs=[pl.BlockSpec((tm, d_hy3), row)] + [pl.BlockSpec((tm, d_ret), row)] * 4,
        compiler_params=_cparams(("parallel",)),
        name="inproj",
    )(x, sh, sc, g, w_bf)


def _kf_kernel(frow_ref, w1_ref, b1_ref, w2_ref, b2_ref, w3_ref, b3_ref, fr_ref, w4_ref, dl_ref, o_ref, *, seq):
    r = pl.program_id(0)
    rows = o_ref.shape[0]
    i = r * rows + lax.broadcasted_iota(jnp.int32, (rows, LANES), 0)
    pos = jnp.abs(i - seq).astype(F32)
    t = pos / (seq - 1.0)
    ang = frow_ref[...] * ((2.0 * math.pi) * pos / seq)
    lane = lax.broadcasted_iota(jnp.int32, (rows, LANES), 1)
    z = jnp.where(lane == 0, t,
                  jnp.where(lane <= HY_BANDS, jnp.cos(ang),
                            jnp.where(lane <= 2 * HY_BANDS, -jnp.sin(ang), 0.0)))
    fr = fr_ref[...]
    h = jnp.sin(fr * (_dot3(z, w1_ref[...]) + b1_ref[...]))
    h = jnp.sin(fr * (_dot3(h, w2_ref[...]) + b2_ref[...]))
    h = jnp.sin(fr * (_dot3(h, w3_ref[...]) + b3_ref[...]))
    h4 = _dot3(h, w4_ref[...])
    nc = o_ref.shape[1]
    iw = r * rows + lax.broadcasted_iota(jnp.int32, (rows, nc), 0)
    tw = jnp.abs(iw - seq).astype(F32) / (seq - 1.0)
    win = jnp.exp(-tw * dl_ref[...])
    o_ref[...] = jnp.where(iw == 0, 0.0, h4 * win)


def _spec_kernel(k0_ref, k1_ref, a_ref, o_ref):
    a = a_ref[...]
    o_ref[...] = (_dot(a[:, :FFT_P], k0_ref[...].astype(BF16))
                  + _dot(a[:, FFT_P:], k1_ref[...].astype(BF16)))


def _dft_mats():
    f = np.arange(FFT_P)[:, None].astype(np.float64)
    t = np.arange(FFT_P)[None, :].astype(np.float64)
    j = np.arange(FFT_N)[None, :].astype(np.float64)
    w = 2.0 * np.pi / FFT_N
    fwd = np.concatenate([np.cos(w * f * t), -np.sin(w * f * t)], axis=0)
    fwd[FFT_P] = np.cos(np.pi * t[0])
    scale = np.full((1, FFT_P), 2.0 / FFT_N)
    scale[0, 0] = 1.0 / FFT_N
    inv = np.concatenate([np.cos(w * t.T * f.T) * scale, -np.sin(w * t.T * f.T) * scale], axis=1)
    inv[:, FFT_P] = np.cos(np.pi * t[0]) / FFT_N
    sgn = np.where(np.arange(FFT_P) % 2 == 0, 1.0, -1.0)[:, None]
    re = sgn * np.cos(w * f * j)
    im = -sgn * np.sin(w * f * j)
    nyq = np.cos(np.pi * j[0])
    re2 = re.copy()
    re2[0] = nyq
    im[0] = 0.0
    flt = np.concatenate([re, re2, im], axis=0)
    flt[:, 0] = 0.0
    return (jnp.asarray(fwd, F32).astype(BF16), jnp.asarray(inv, F32).astype(BF16),
            jnp.asarray(flt, F32).astype(BF16))


def _hyena_spectra(seq, fw, flt_mat):
    frow, w1, b1, w2, b2, w3, b3, fr, w4, dl = fw
    nblk = 2 * seq // FFT_P
    nc = w4.shape[-1]
    full = lambda a: pl.BlockSpec(a.shape, lambda r: (0,) * a.ndim)
    kf = pl.pallas_call(
        functools.partial(_kf_kernel, seq=seq),
        out_shape=jax.ShapeDtypeStruct((2 * seq, nc), F32),
        grid=(nblk,),
        in_specs=[full(frow), full(w1), full(b1), full(w2), full(b2), full(w3), full(b3), full(fr),
                  pl.BlockSpec((None,) + w4.shape[1:], lambda r: (jnp.where(r >= nblk // 2, 0, 1), 0, 0)),
                  full(dl)],
        out_specs=pl.BlockSpec((FFT_P, nc), lambda r: (r, 0)),
        compiler_params=_cparams(("parallel",)),
        name="hyena_filter",
    )(frow, w1, b1, w2, b2, w3, b3, fr, w4, dl)
    return pl.pallas_call(
        _spec_kernel,
        out_shape=jax.ShapeDtypeStruct((nblk - 1, 3 * FFT_P, nc), F32),
        grid=(nblk - 1,),
        in_specs=[pl.BlockSpec((FFT_P, nc), lambda w: (w, 0)),
                  pl.BlockSpec((FFT_P, nc), lambda w: (w + 1, 0)),
                  pl.BlockSpec(flt_mat.shape, lambda w: (0, 0))],
        out_specs=pl.BlockSpec((None, 3 * FFT_P, nc), lambda w: (w, 0, 0)),
        compiler_params=_cparams(("parallel",)),
        name="hyena_spectra",
    )(kf, kf, flt_mat)


def _hy_kernel(u_ref, x_ref, cwu_ref, cbu_ref, cwx_ref, cbx_ref, bias_ref, g_ref, fwd_ref, inv_ref,
               o_ref, u_scr, gate_scr, uf_scr, y_scr, *, n, conv_u):
    seq = n * FFT_P
    cb = o_ref.shape[1]

    def sconv(v, w_ref, b_ref):
        row = lax.broadcasted_iota(jnp.int32, v.shape, 0)
        prev = jnp.where(row == 0, 0.0, pltpu.roll(v, 1, 0))
        nxt = jnp.where(row == seq - 1, 0.0, pltpu.roll(v, seq - 1, 0))
        return b_ref[...] + prev * w_ref[0:1, :] + v * w_ref[1:2, :] + nxt * w_ref[2:3, :]

    u_scr[...] = sconv(u_ref[...], cwu_ref, cbu_ref) if conv_u else u_ref[...]
    gate_scr[...] = sconv(x_ref[...], cwx_ref, cbx_ref)
    fwd = fwd_ref[...]
    for b in range(n):
        uf_scr[b] = _dot(fwd, u_scr[b * FFT_P:(b + 1) * FFT_P, :].astype(BF16))

    rc = 32

    def a_body(a, carry):
        for c in range(FFT_P // rc):
            r0 = c * rc

            yre = yim = jnp.zeros((rc, cb), F32)
            for b in range(n):
                w = a - b + (n - 1)
                gre = g_ref[w, pl.ds(r0, rc), :]
                gre2 = g_ref[w, pl.ds(FFT_P + r0, rc), :]
                gim = g_ref[w, pl.ds(2 * FFT_P + r0, rc), :]
                ure = uf_scr[b, pl.ds(r0, rc), :]
                uim = uf_scr[b, pl.ds(FFT_P + r0, rc), :]
                yre = yre + gre * ure - gim * uim
                yim = yim + gre2 * uim + gim * ure
            y_scr[pl.ds(r0, rc), :] = yre
            y_scr[pl.ds(FFT_P + r0, rc), :] = yim
        y = _dot(inv_ref[...], y_scr[...].astype(BF16))
        rows = pl.ds(pl.multiple_of(a * FFT_P, FFT_P), FFT_P)
        ua = u_scr[rows, :]
        o_ref[rows, :] = gate_scr[rows, :] * (y + ua * bias_ref[...])
        return carry

    lax.fori_loop(0, n, a_body, 0)


def _hyena_order(u_arr, u_row0, u_col0, x_arr, x_row0, x_col0, conv_w, conv_b, bias_row, spectra, spec_col0,
                 fwd, inv, nbatch, seq, cb, conv_u):
    n = seq // FFT_P
    d_hy = bias_row.shape[1]
    ncb = d_hy // cb
    nwin = spectra.shape[0]
    grid = (ncb, nbatch)
    in_specs = [
        pl.BlockSpec((seq, cb), lambda c, b: (u_row0 + b, u_col0 + c)),
        pl.BlockSpec((seq, cb), lambda c, b: (x_row0 + b, x_col0 + c)),
        pl.BlockSpec((3, cb), lambda c, b: (0, u_col0 + c)),
        pl.BlockSpec((1, cb), lambda c, b: (0, u_col0 + c)),
        pl.BlockSpec((3, cb), lambda c, b: (0, x_col0 + c)),
        pl.BlockSpec((1, cb), lambda c, b: (0, x_col0 + c)),
        pl.BlockSpec((1, cb), lambda c, b: (0, c)),
        pl.BlockSpec((nwin, 3 * FFT_P, cb), lambda c, b: (0, 0, spec_col0 + c)),
        pl.BlockSpec(fwd.shape, lambda c, b: (0, 0)),
        pl.BlockSpec(inv.shape, lambda c, b: (0, 0)),
    ]
    return pl.pallas_call(
        functools.partial(_hy_kernel, n=n, conv_u=conv_u),
        out_shape=jax.ShapeDtypeStruct((nbatch * seq, d_hy), F32),
        grid=grid,
        in_specs=in_specs,
        out_specs=pl.BlockSpec((seq, cb), lambda c, b: (b, c)),
        scratch_shapes=[pltpu.VMEM((seq, cb), F32), pltpu.VMEM((seq, cb), F32),
                        pltpu.VMEM((n, FFT_N, cb), F32), pltpu.VMEM((FFT_N, cb), F32)],
        compiler_params=_cparams(("parallel", "parallel")),
        name="hyena_conv",
    )(u_arr, x_arr, conv_w, conv_b, conv_w, conv_b, bias_row, spectra, fwd, inv)


def _hyena_mix(hy_in, conv_w, conv_b, bias, spectra, fwd, inv, row0, nbatch, seq, cb):
    ncb = bias.shape[1] // cb
    z1 = _hyena_order(hy_in, row0, 0, hy_in, row0, ncb, conv_w, conv_b, bias[0:1], spectra, 0,
                      fwd, inv, nbatch, seq, cb, True)
    return _hyena_order(z1, 0, 0, hy_in, row0, 2 * ncb, conv_w, conv_b, bias[1:2], spectra, ncb,
                        fwd, inv, nbatch, seq, cb, False)


def _ret_kernel(q_ref, k_ref, v_ref, g_ref, cos_ref, sin_ref, df_ref, db_ref, ng_ref, s0f_ref, s0b_ref,
                y_ref, sf_ref, sb_ref, q_scr, k_scr, sb_scr, *, n, rope, has_state):
    c = RET_CHUNK
    dk = q_ref.shape[1]
    lgf = -jnp.exp(df_ref[...])
    lgb = -jnp.exp(db_ref[...])
    ri = lax.broadcasted_iota(jnp.int32, (c, c), 0)
    ci = lax.broadcasted_iota(jnp.int32, (c, c), 1)
    diff = (ri - ci).astype(F32)
    mask = (jnp.where(diff >= 0, jnp.exp(jnp.maximum(diff, 0.0) * lgf), 0.0)
            + jnp.where(diff <= 0, jnp.exp(jnp.maximum(-diff, 0.0) * lgb), 0.0))
    pos = lax.broadcasted_iota(jnp.int32, (c, dk), 0).astype(F32)
    xi_f = jnp.exp((pos + 1.0) * lgf)
    ze_f = jnp.exp((c - 1.0 - pos) * lgf)
    cd_f = jnp.exp(c * lgf)
    xi_b = jnp.exp((c - pos) * lgb)
    ze_b = jnp.exp(pos * lgb)
    cd_b = jnp.exp(c * lgb)

    q = q_ref[...]
    k = k_ref[...]
    if rope:
        cs = cos_ref[...]
        sn = sin_ref[...]
        q = q * cs + pltpu.roll(q, dk // 2, 1) * sn
        k = k * cs + pltpu.roll(k, dk // 2, 1) * sn
    q_scr[...] = q
    k_scr[...] = k * (dk ** -0.5)

    def dot_tn(a, b):
        return lax.dot_general(a, b, (((0,), (0,)), ((), ())), preferred_element_type=F32)

    def chunk(i):
        return pl.ds(pl.multiple_of(i * c, c), c)

    def bwd_body(j, s):
        i = n - 1 - j
        sb_scr[i] = s
        rows = chunk(i)
        kz = (k_scr[rows, :] * ze_b).astype(BF16)
        return s * cd_b + dot_tn(kz, v_ref[rows, :].astype(BF16))

    s0b = s0b_ref[...] if has_state else jnp.zeros((dk, dk), F32)
    unroll = math.gcd(n, 4)
    sb_ref[...] = lax.fori_loop(0, n, bwd_body, s0b, unroll=unroll)

    def fwd_body(i, s):
        rows = chunk(i)
        qb = q_scr[rows, :].astype(BF16)
        kc = k_scr[rows, :]
        vb = v_ref[rows, :].astype(BF16)
        sc = lax.dot_general(qb, kc.astype(BF16), (((1,), (1,)), ((), ())), preferred_element_type=F32)
        o = _dot((sc * mask).astype(BF16), vb)
        o = o + _dot(qb, s.astype(BF16)) * xi_f + _dot(qb, sb_scr[i].astype(BF16)) * xi_b
        gt = g_ref[rows, :]
        y_ref[rows, :] = _rms(o, ng_ref[...]) * (gt * jax.nn.sigmoid(gt))
        return s * cd_f + dot_tn((kc * ze_f).astype(BF16), vb)

    s0f = s0f_ref[...] if has_state else jnp.zeros((dk, dk), F32)
    sf_ref[...] = lax.fori_loop(0, n, fwd_body, s0f, unroll=unroll)


def _retention(q, k, v, g, cos, sin, dec_f, dec_b, norm_g, s0f, s0b, row_blk0, nbatch, seq, rope, has_state):
    nh = N_RET_HEADS
    dk = q.shape[1] // nh
    n = seq // RET_CHUNK
    tok = lambda b, h: (row_blk0 + b, h)
    tspec = pl.BlockSpec((seq, dk), tok)
    rspec = pl.BlockSpec((seq, dk), lambda b, h: (0, 0))
    hspec = pl.BlockSpec((None, 1, dk), lambda b, h: (h, 0, 0))
    sspec = pl.BlockSpec((None, None, None, dk, dk), lambda b, h: (b, 0, h, 0, 0))
    s0spec = sspec if has_state else pl.BlockSpec((None, None, None, dk, dk), lambda b, h: (0, 0, h, 0, 0))
    st_shape = jax.ShapeDtypeStruct((nbatch, 1, nh, dk, dk), F32)
    return pl.pallas_call(
        functools.partial(_ret_kernel, n=n, rope=rope, has_state=has_state),
        out_shape=[jax.ShapeDtypeStruct((nbatch * seq, nh * dk), F32), st_shape, st_shape],
        grid=(nbatch, nh),
        in_specs=[tspec, tspec, tspec, tspec, rspec, rspec, hspec, hspec, hspec, s0spec, s0spec],
        out_specs=[pl.BlockSpec((seq, dk), lambda b, h: (b, h)), sspec, sspec],
        scratch_shapes=[pltpu.VMEM((seq, dk), F32), pltpu.VMEM((seq, dk), F32),
                        pltpu.VMEM((n, dk, dk), F32)],
        compiler_params=_cparams(("parallel", "parallel")),
        name="retention",
    )(q, k, v, g, cos, sin, dec_f, dec_b, norm_g, s0f, s0b)


def _outproj_kernel(yh_ref, yr_ref, x_ref, gt_ref, sh_ref, sc_ref, hg_ref, ng_ref, wo_ref, x1_ref, h2t_ref):
    d_hy = yh_ref.shape[1]
    nh = _rms(yh_ref[...], hg_ref[...]).astype(BF16)
    y = _dot(nh, wo_ref[0:d_hy, :]) + _dot(yr_ref[...].astype(BF16), wo_ref[d_hy:, :])
    x1 = x_ref[...] + gt_ref[...] * y
    x1_ref[...] = x1
    h2 = _rms(x1, ng_ref[...]) * (1.0 + sc_ref[...]) + sh_ref[...]
    h2t_ref[...] = h2.T.astype(BF16)


def _outproj(y_hy, y_ret, x, gt1, sh2, sc2, hy_g, n2_g, wo_bf, grp, tm):
    t, d = x.shape
    row = lambda i: (i, 0)
    modspec = pl.BlockSpec((None, 1, d), lambda i: (grp(i), 0, 0))
    return pl.pallas_call(
        _outproj_kernel,
        out_shape=[jax.ShapeDtypeStruct((t, d), F32), jax.ShapeDtypeStruct((d, t), BF16)],
        grid=(t // tm,),
        in_specs=[pl.BlockSpec((tm, y_hy.shape[1]), row), pl.BlockSpec((tm, y_ret.shape[1]), row),
                  pl.BlockSpec((tm, d), row), modspec, modspec, modspec,
                  pl.BlockSpec((1, y_hy.shape[1]), lambda i: (0, 0)),
                  pl.BlockSpec((1, d), lambda i: (0, 0)),
                  pl.BlockSpec(wo_bf.shape, lambda i: (0, 0))],
        out_specs=[pl.BlockSpec((tm, d), row), pl.BlockSpec((d, tm), lambda i: (0, i))],
        compiler_params=_cparams(("parallel",)),
        name="outproj",
    )(y_hy, y_ret, x, gt1, sh2, sc2, hy_g, n2_g, wo_bf)


def _sort_desc(xs):
    xs = list(xs)
    n = len(xs)
    k = 2
    while k <= n:
        j = k // 2
        while j >= 1:
            for i in range(n):
                l = i ^ j
                if l > i:
                    hi, lo = jnp.maximum(xs[i], xs[l]), jnp.minimum(xs[i], xs[l])
                    xs[i], xs[l] = (hi, lo) if (i & k) == 0 else (lo, hi)
            j //= 2
        k *= 2
    return xs


def _merge_top(a, b):
    n = len(a)
    xs = [jnp.maximum(a[i], b[n - 1 - i]) for i in range(n)]
    j = n // 2
    while j >= 1:
        for i in range(n):
            l = i ^ j
            if l > i:
                xs[i], xs[l] = jnp.maximum(xs[i], xs[l]), jnp.minimum(xs[i], xs[l])
        j //= 2
    return xs


def _top_sorted(load, count, k):
    acc = None
    for g0 in range(0, count, k):
        grp = _sort_desc([load(i) for i in range(g0, g0 + k)])
        acc = grp if acc is None else _merge_top(acc, grp)
    return acc


def _staircase(k):
    return [(a, b) for a in range(k) for b in range(k) if (a + 1) * (b + 1) <= k + 1]


def _peer1_kernel(h2t_ref, wq_ref, keys_ref, r2_ref, e2_ref, cnt_ref, e1_ref, q_scr, s1_scr, s2_scr):
    h = pl.program_id(1)
    lt_n = r2_ref.shape[0]
    nk = N_KEYS
    kk = PEER_TOPK

    @pl.when(h == 0)
    def _():
        q_scr[...] = _dot(wq_ref[...], h2t_ref[...]).astype(BF16)

    dq = q_scr.shape[0] // (2 * PEER_HEADS)
    for p, scr in ((0, s1_scr), (1, s2_scr)):
        rows = pl.ds(pl.multiple_of((2 * h + p) * dq, dq), dq)
        s = _dot(keys_ref[2 * h + p], q_scr[rows, :])
        for lt in range(lt_n):
            scr[lt * TILE_PITCH:lt * TILE_PITCH + nk, :] = s[:, lt * LANES:(lt + 1) * LANES]

    ld1 = lambda i: s1_scr[pl.ds(i, lt_n, stride=TILE_PITCH), :]
    ld2 = lambda i: s2_scr[pl.ds(i, lt_n, stride=TILE_PITCH), :]
    v1 = _top_sorted(ld1, nk, kk)
    v2 = _top_sorted(ld2, nk, kk)
    pairs = _staircase(kk)
    cand = [v1[a] + v2[b] for a, b in pairs]
    neg = jnp.full_like(cand[0], -jnp.inf)
    padded = cand + [neg] * (-len(cand) % kk)
    top = _top_sorted(lambda i: padded[i], len(padded), kk)
    tau = top[kk - 1]
    nxt = neg
    for cnd in cand:
        nxt = jnp.maximum(nxt, jnp.where(cnd < tau, cnd, neg))
    thr = 0.5 * (tau + nxt)
    m = v1[0] + v2[0]
    z = jnp.zeros_like(tau)
    for cnd in cand:
        z = z + jnp.where(cnd >= tau, jnp.exp(cnd - m), 0.0)
    inv_z = 1.0 / z

    for i in range(nk):
        x1 = ld1(i)
        gap = thr - x1
        cnt = jnp.zeros_like(x1)
        for b in range(kk):
            cnt = cnt + jnp.where(v2[b] > gap, 1.0, 0.0)
        keep = x1 >= v1[kk - 1]
        cnt_ref[i] = jnp.where(keep, cnt, 0.0)
        e1_ref[i] = jnp.where(keep, jnp.exp(x1 - v1[0]) * inv_z, 0.0)
    for lt in range(lt_n):
        s2 = s2_scr[lt * TILE_PITCH:lt * TILE_PITCH + nk, :]
        rank = jnp.zeros_like(s2)
        for b in range(kk):
            rank = rank + jnp.where(v2[b][lt:lt + 1, :] > s2, 1.0, 0.0)
        r2_ref[lt] = rank.astype(BF16)
        e2_ref[lt] = jnp.where(s2 >= v2[kk - 1][lt:lt + 1, :], jnp.exp(s2 - v2[0][lt:lt + 1, :]), 0.0).astype(BF16)


def _peer1(h2t, wqt_bf, keys_bf):
    d, t = h2t.shape
    nh = PEER_HEADS
    tb = TOK_VREG
    lt_n = tb // LANES
    tile_shape = jax.ShapeDtypeStruct((nh, t // LANES, N_KEYS, LANES), BF16)
    vreg_shape = jax.ShapeDtypeStruct((nh, t // tb, N_KEYS, lt_n, LANES), F32)
    tile_spec = pl.BlockSpec((None, lt_n, N_KEYS, LANES), lambda i, h: (h, i, 0, 0))
    vreg_spec = pl.BlockSpec((None, None, N_KEYS, lt_n, LANES), lambda i, h: (h, i, 0, 0, 0))
    return pl.pallas_call(
        _peer1_kernel,
        out_shape=[tile_shape, tile_shape, vreg_shape, vreg_shape],
        grid=(t // tb, nh),
        in_specs=[pl.BlockSpec((d, tb), lambda i, h: (0, i)),
                  pl.BlockSpec(wqt_bf.shape, lambda i, h: (0, 0)),
                  pl.BlockSpec(keys_bf.shape, lambda i, h: (0, 0, 0))],
        out_specs=[tile_spec, tile_spec, vreg_spec, vreg_spec],
        scratch_shapes=[pltpu.VMEM((wqt_bf.shape[0], tb), BF16),
                        pltpu.VMEM((lt_n * TILE_PITCH, LANES), F32),
                        pltpu.VMEM((lt_n * TILE_PITCH, LANES), F32)],
        compiler_params=_cparams(("parallel", "arbitrary")),
        name="peer_select",
    )(h2t, wqt_bf, keys_bf)


def _gelu_tanh(x):
    k = -2.0 * math.sqrt(2.0 / math.pi) * math.log2(math.e)
    e = jnp.exp2(x * (x * x * (k * 0.044715) + k))
    return x / (1.0 + e)


GATE_ROWS = 2 * SUBLANES
SEL_PITCH = N_KEYS + SUBLANES


def _peer2_kernel(h2t_in, u_ref, vt_ref, r2_in, e2_in, cnt_ref, e1_ref, ot_ref,
                  at_scr, zt_scr, sel_ref, h2t_ref):
    eb = pl.program_id(1)
    lt_n = r2_in.shape[1]
    ig_n = cnt_ref.shape[1]

    @pl.when(eb == 0)
    def _():
        ot_ref[...] = jnp.zeros_like(ot_ref)
        for hh in range(PEER_HEADS):
            sel_ref[:, 2 * hh, 0:N_KEYS, :] = r2_in[hh].astype(F32)
            sel_ref[:, 2 * hh + 1, 0:N_KEYS, :] = e2_in[hh].astype(F32)
        h2t_ref[...] = h2t_in[...]

    tcw = at_scr.shape[2]
    ntc = lt_n * LANES // tcw
    lt_per = tcw // LANES

    def row(ref, hh, ig, lt):
        return jnp.broadcast_to(ref[hh, ig, lt:lt + 1, :], (GATE_ROWS, LANES))

    def scores(tc):
        at_scr[tc % 2] = _dot(u_ref[...], h2t_ref[:, tc * tcw:(tc + 1) * tcw])

    def gates(tc):
        for ig in range(ig_n):
            for l in range(lt_per):
                lt = tc * lt_per + l
                cnt = [row(cnt_ref, hh, ig, lt) for hh in range(PEER_HEADS)]
                e1 = [row(e1_ref, hh, ig, lt) for hh in range(PEER_HEADS)]
                for c in range(N_KEYS // GATE_ROWS):
                    js = slice(c * GATE_ROWS, (c + 1) * GATE_ROWS)
                    w = jnp.zeros((GATE_ROWS, LANES), F32)
                    for hh in range(PEER_HEADS):
                        r2 = sel_ref[lt, 2 * hh, js, :]
                        e2 = sel_ref[lt, 2 * hh + 1, js, :]
                        w = w + jnp.where(r2 < cnt[hh], e2, 0.0) * e1[hh]
                    rows = slice(ig * N_KEYS + c * GATE_ROWS, ig * N_KEYS + (c + 1) * GATE_ROWS)
                    a = at_scr[tc % 2, rows, l * LANES:(l + 1) * LANES]
                    zt_scr[rows, lt * LANES:(lt + 1) * LANES] = (w * _gelu_tanh(a)).astype(BF16)

    def combine(tc):
        cols = slice(tc * tcw, (tc + 1) * tcw)
        ot_ref[:, cols] += _dot(vt_ref[...], zt_scr[:, cols])

    scores(0)
    for tc in range(ntc):
        if tc + 1 < ntc:
            scores(tc + 1)
        gates(tc)
        combine(tc)


def _peer2(h2t, u_bf, vt_bf, r2, e2, cnt, e1, eblk):
    d, t = h2t.shape
    ne = u_bf.shape[0]
    nh = PEER_HEADS
    tb = TOK_VREG
    lt_n = tb // LANES
    ig_n = eblk // N_KEYS
    tile_spec = pl.BlockSpec((nh, lt_n, N_KEYS, LANES), lambda i, e: (0, i, 0, 0))
    vreg_spec = pl.BlockSpec((nh, None, ig_n, lt_n, LANES), lambda i, e: (0, i, e, 0, 0))
    return pl.pallas_call(
        _peer2_kernel,
        out_shape=jax.ShapeDtypeStruct((d, t), F32),
        grid=(t // tb, ne // eblk),
        in_specs=[pl.BlockSpec((d, tb), lambda i, e: (0, i)),
                  pl.BlockSpec((eblk, d), lambda i, e: (e, 0)),
                  pl.BlockSpec((d, eblk), lambda i, e: (0, e)),
                  tile_spec, tile_spec, vreg_spec, vreg_spec],
        out_specs=pl.BlockSpec((d, tb), lambda i, e: (0, i)),
        scratch_shapes=[pltpu.VMEM((2, eblk, 2 * LANES), F32), pltpu.VMEM((eblk, tb), BF16),
                        pltpu.VMEM((lt_n, 2 * nh, SEL_PITCH, LANES), F32), pltpu.VMEM((d, tb), BF16)],
        compiler_params=_cparams(("parallel", "arbitrary")),
        name="peer_experts",
    )(h2t, u_bf, vt_bf, r2, e2, cnt, e1)


def _final_kernel(x1_ref, ot_ref, gt_ref, fg_ref, y_ref):
    y_ref[...] = _rms(x1_ref[...] + gt_ref[...] * ot_ref[...].T, fg_ref[...])


def _final(x1, ot, gt2, final_g, grp, tm):
    t, d = x1.shape
    row = lambda i: (i, 0)
    return pl.pallas_call(
        _final_kernel,
        out_shape=jax.ShapeDtypeStruct((t, d), F32),
        grid=(t // tm,),
        in_specs=[pl.BlockSpec((tm, d), row), pl.BlockSpec((d, tm), lambda i: (0, i)),
                  pl.BlockSpec((None, 1, d), lambda i: (grp(i), 0, 0)),
                  pl.BlockSpec((1, d), lambda i: (0, 0))],
        out_specs=pl.BlockSpec((tm, d), row),
        compiler_params=_cparams(("parallel",)),
        name="final_norm",
    )(x1, ot, gt2, final_g)


def _rope_tables(seq, dk):
    rows = seq // GRID_W
    r, c = jnp.meshgrid(jnp.arange(rows, dtype=F32), jnp.arange(GRID_W, dtype=F32), indexing='ij')
    r = r.reshape(-1)
    c = c.reshape(-1)
    nf = dk // 4
    inv = ROPE_BASE ** (-jnp.arange(nf, dtype=F32) / nf)
    ang = jnp.concatenate([r[:, None] * inv, c[:, None] * inv], axis=-1)
    cos = jnp.concatenate([jnp.cos(ang), jnp.cos(ang)], axis=-1)
    sin = jnp.concatenate([-jnp.sin(ang), jnp.sin(ang)], axis=-1)
    return cos, sin


def _pad2(a, rows, cols):
    return jnp.pad(a, ((0, rows - a.shape[0]), (0, cols - a.shape[1])))


def kernel(x_prompt, x_sample, state_ret_fwd, state_ret_bwd, c, c_ctx, w_ada, b_ada, norm1_g, w_in, hy_conv_w, hy_conv_b, hy_w1, hy_b1, hy_w2, hy_b2, hy_w3, hy_b3, hy_w4, hy_freq, hy_bias, hy_norm_g, ret_decay_fwd, ret_decay_bwd, ret_norm_g, w_out, norm2_g, peer_wq, peer_keys, peer_u, peer_v, final_g):
    b_ctx, l_ctx, d = x_prompt.shape
    b_lat, l_lat, _ = x_sample.shape
    depth = w_ada.shape[0]
    assert depth == 1
    t_ctx, t_lat = b_ctx * l_ctx, b_lat * l_lat
    d_hy = hy_norm_g.shape[1]
    d_ret = ret_norm_g.shape[1]
    dk = d_ret // N_RET_HEADS
    tm = 512

    ngrp = 1 + b_lat
    cond = jnp.concatenate([c_ctx[None, :], c], axis=0)
    cond8 = jnp.pad(cond, ((0, SUBLANES - ngrp), (0, 0)))
    mod = _ada(cond8, w_ada[0], b_ada[0])
    sh1, sc1, gt1, sh2, sc2, gt2 = [m.reshape(SUBLANES, 1, d) for m in jnp.split(mod, 6, axis=-1)]

    fw = hy_w1.shape[-1]
    f = jnp.linspace(1e-4, HY_BANDS - 1, HY_BANDS, dtype=F32)
    frow = jnp.zeros((1, LANES), F32).at[0, 1:1 + HY_BANDS].set(f).at[0, 1 + HY_BANDS:1 + 2 * HY_BANDS].set(f)
    min_decay = math.log(HY_DECAY_TARGET) / HY_SLOW_PCT
    max_decay = math.log(HY_DECAY_TARGET) / HY_FAST_PCT
    deltas = jnp.abs(jnp.linspace(min_decay, max_decay, d_hy, dtype=F32))
    w4 = hy_w4[0].reshape(fw, 2, 2 * d_hy).transpose(1, 0, 2)
    w4 = jnp.pad(w4, ((0, 0), (0, LANES - fw), (0, 0)))
    filt_w = (frow, _pad2(hy_w1[0], LANES, LANES), _pad2(hy_b1, 1, LANES), _pad2(hy_w2[0], LANES, LANES),
              _pad2(hy_b2, 1, LANES), _pad2(hy_w3[0], LANES, LANES), _pad2(hy_b3, 1, LANES),
              _pad2(hy_freq, 1, LANES), w4, jnp.tile(deltas, 2)[None, :])
    fwd_m, inv_m, flt_m = _dft_mats()
    cos_t, sin_t = _rope_tables(l_lat, dk)
    dec_f = jnp.broadcast_to(ret_decay_fwd[0][:, None, None], (N_RET_HEADS, 1, dk))
    dec_b = jnp.broadcast_to(ret_decay_bwd[0][:, None, None], (N_RET_HEADS, 1, dk))
    ng = ret_norm_g[0].reshape(N_RET_HEADS, 1, dk)
    w_in_bf = w_in[0].astype(BF16)
    w_out_bf = w_out[0].astype(BF16)
    wqt = peer_wq[0].T.astype(BF16)
    keys = peer_keys[0].reshape(2 * PEER_HEADS, N_KEYS, -1).astype(BF16)
    u_bf = peer_u[0].astype(BF16)
    vt_bf = peer_v[0].T.astype(BF16)
    fg = final_g.reshape(1, d)

    def trunk(x, nbatch, seq, latent, hy_cb):
        grp = (lambda i: 1 + i // (seq // tm)) if latent else (lambda i: 0)
        hy_in, rq, rk, rv, rg = _inproj(x, sh1, sc1, norm1_g, w_in_bf, grp, tm, 3 * d_hy, d_ret)
        spectra = _hyena_spectra(seq, filt_w, flt_m)
        y_hy = _hyena_mix(hy_in, hy_conv_w[0], hy_conv_b, hy_bias[0], spectra, fwd_m, inv_m, 0, nbatch, seq, hy_cb)
        y_ret, s_f, s_b = _retention(rq, rk, rv, rg, cos_t, sin_t, dec_f, dec_b, ng, state_ret_fwd, state_ret_bwd,
                                     0, nbatch, seq, latent, latent)
        x1, h2t = _outproj(y_hy, y_ret, x, gt1, sh2, sc2, hy_norm_g, norm2_g, w_out_bf, grp, tm)
        r2, e2, cnt, e1 = _peer1(h2t, wqt, keys)
        ot = _peer2(h2t, u_bf, vt_bf, r2, e2, cnt, e1, 1024)
        y = _final(x1, ot, gt2, fg, grp, tm)
        return y.reshape(nbatch, seq, d), s_f, s_b

    y_prompt, sf_new, sb_new = trunk(x_prompt.reshape(t_ctx, d), b_ctx, l_ctx, False, 512)
    y_sample, _, _ = trunk(x_sample.reshape(t_lat, d), b_lat, l_lat, True, 128)
    return (y_prompt, y_sample, sf_new, sb_new)
```

```python
import functools
import math

import numpy as np
import jax
import jax.numpy as jnp
from jax import lax
from jax.experimental import pallas as pl
from jax.experimental.pallas import tpu as pltpu

F32 = jnp.float32
BF16 = jnp.bfloat16

LANES = 128
SUBLANES = 8
VMEM_LIMIT = 56 * 1024 * 1024

EPS = 1e-6
GRID_W = 64
N_RET_HEADS = 4
RET_CHUNK = 128
ROPE_BASE = 10000.0
HY_EMB = 33
HY_BANDS = (HY_EMB - 1) // 2
HY_DECAY_TARGET = 1e-2
HY_FAST_PCT = 0.3
HY_SLOW_PCT = 1.5
N_KEYS = 128
PEER_HEADS = 8
PEER_TOPK = 16
FFT_P = 256
FFT_N = 2 * FFT_P
TOK_VREG = SUBLANES * LANES
TILE_PITCH = 136


def _cparams(sem, vmem=VMEM_LIMIT, flags=None):
    return pltpu.CompilerParams(dimension_semantics=sem, vmem_limit_bytes=vmem, flags=flags)


def _dot(a, b):
    return jnp.dot(a, b, preferred_element_type=F32)


def _split(a):
    hi = a.astype(BF16)
    lo = (a - hi.astype(F32)).astype(BF16)
    return hi, lo


def _dot3(a, b):
    ah, al = _split(a)
    bh, bl = _split(b)
    return _dot(ah, bh) + (_dot(al, bh) + _dot(ah, bl))


def _rms(x, g):
    return x * lax.rsqrt(jnp.mean(x * x, axis=-1, keepdims=True) + EPS) * g


def _ada_kernel(c_ref, w_ref, b_ref, o_ref):
    c = c_ref[...]
    s = c * jax.nn.sigmoid(c)
    o_ref[...] = _dot3(s, w_ref[...]) + b_ref[...]


def _ada(cond8, w_ada, b_ada):
    d, n = w_ada.shape
    tn = 1536
    return pl.pallas_call(
        _ada_kernel,
        out_shape=jax.ShapeDtypeStruct((cond8.shape[0], n), F32),
        grid=(n // tn,),
        in_specs=[pl.BlockSpec((cond8.shape[0], d), lambda j: (0, 0)),
                  pl.BlockSpec((d, tn), lambda j: (0, j)),
                  pl.BlockSpec((1, tn), lambda j: (0, j))],
        out_specs=pl.BlockSpec((cond8.shape[0], tn), lambda j: (0, j)),
        compiler_params=_cparams(("parallel",)),
        name="ada",
    )(cond8, w_ada, b_ada.reshape(1, n))


def _inproj_kernel(x_ref, sh_ref, sc_ref, g_ref, w_ref, hy_ref, q_ref, k_ref, v_ref, gg_ref, *, d_hy3, d_ret):
    h = _rms(x_ref[...], g_ref[...]) * (1.0 + sc_ref[...]) + sh_ref[...]
    hb = h.astype(BF16)
    hy = _dot(hb, w_ref[:, 0:d_hy3])
    cb = hy_ref.shape[2]
    for c in range(hy_ref.shape[0]):
        hy_ref[c] = hy[:, c * cb:(c + 1) * cb]
    for i, o_ref in enumerate((q_ref, k_ref, v_ref, gg_ref)):
        o_ref[...] = _dot(hb, w_ref[:, d_hy3 + i * d_ret:d_hy3 + (i + 1) * d_ret])


def _inproj(x, sh, sc, g, w_bf, grp, tm, d_hy3, d_ret, hy_cb):
    t, d = x.shape
    row = lambda i: (i, 0)
    modspec = pl.BlockSpec((None, 1, d), lambda i: (grp(i), 0, 0))
    outs = [jax.ShapeDtypeStruct((d_hy3 // hy_cb, t, hy_cb), F32)] + [jax.ShapeDtypeStruct((t, d_ret), F32)] * 4
    return pl.pallas_call(
        functools.partial(_inproj_kernel, d_hy3=d_hy3, d_ret=d_ret),
        out_shape=outs,
        grid=(t // tm,),
        in_specs=[pl.BlockSpec((tm, d), row), modspec, modspec,
                  pl.BlockSpec((1, d), lambda i: (0, 0)),
                  pl.BlockSpec(w_bf.shape, lambda i: (0, 0))],
        out_specs=[pl.BlockSpec((d_hy3 // hy_cb, tm, hy_cb), lambda i: (0, i, 0))]
        + [pl.BlockSpec((tm, d_ret), row)] * 4,
        compiler_params=_cparams(("parallel",)),
        name="inproj",
    )(x, sh, sc, g, w_bf)


def _kf_kernel(frow_ref, w1_ref, b1_ref, w2_ref, b2_ref, w3_ref, b3_ref, fr_ref, w4_ref, dl_ref, o_ref, *, seq):
    r = pl.program_id(0)
    rows = o_ref.shape[0]
    i = r * rows + lax.broadcasted_iota(jnp.int32, (rows, LANES), 0)
    pos = jnp.abs(i - seq).astype(F32)
    t = pos / (seq - 1.0)
    ang = frow_ref[...] * ((2.0 * math.pi) * pos / seq)
    lane = lax.broadcasted_iota(jnp.int32, (rows, LANES), 1)
    z = jnp.where(lane == 0, t,
                  jnp.where(lane <= HY_BANDS, jnp.cos(ang),
                            jnp.where(lane <= 2 * HY_BANDS, -jnp.sin(ang), 0.0)))
    fr = fr_ref[...]
    h = jnp.sin(fr * (_dot3(z, w1_ref[...]) + b1_ref[...]))
    h = jnp.sin(fr * (_dot3(h, w2_ref[...]) + b2_ref[...]))
    h = jnp.sin(fr * (_dot3(h, w3_ref[...]) + b3_ref[...]))
    h4 = _dot3(h, w4_ref[...])
    nc = o_ref.shape[1]
    iw = r * rows + lax.broadcasted_iota(jnp.int32, (rows, nc), 0)
    tw = jnp.abs(iw - seq).astype(F32) / (seq - 1.0)
    win = jnp.exp(-tw * dl_ref[...])
    o_ref[...] = jnp.where(iw == 0, 0.0, h4 * win)


def _spec_kernel(k0_ref, k1_ref, a_ref, o_ref):
    a = a_ref[...]
    s = (_dot(a[:, :FFT_P], k0_ref[...].astype(BF16))
         + _dot(a[:, FFT_P:], k1_ref[...].astype(BF16)))
    cb = o_ref.shape[2]
    for c in range(o_ref.shape[0]):
        o_ref[c] = s[:, c * cb:(c + 1) * cb]


def _dft_mats():
    f = np.arange(FFT_P)[:, None].astype(np.float64)
    t = np.arange(FFT_P)[None, :].astype(np.float64)
    j = np.arange(FFT_N)[None, :].astype(np.float64)
    w = 2.0 * np.pi / FFT_N
    fwd = np.concatenate([np.cos(w * f * t), -np.sin(w * f * t)], axis=0)
    fwd[FFT_P] = np.cos(np.pi * t[0])
    scale = np.full((1, FFT_P), 2.0 / FFT_N)
    scale[0, 0] = 1.0 / FFT_N
    inv = np.concatenate([np.cos(w * t.T * f.T) * scale, -np.sin(w * t.T * f.T) * scale], axis=1)
    inv[:, FFT_P] = np.cos(np.pi * t[0]) / FFT_N
    sgn = np.where(np.arange(FFT_P) % 2 == 0, 1.0, -1.0)[:, None]
    re = sgn * np.cos(w * f * j)
    im = -sgn * np.sin(w * f * j)
    nyq = np.cos(np.pi * j[0])
    re2 = re.copy()
    re2[0] = nyq
    im[0] = 0.0
    flt = np.concatenate([re, re2, im], axis=0)
    flt[:, 0] = 0.0
    return (jnp.asarray(fwd, F32).astype(BF16), jnp.asarray(inv, F32).astype(BF16),
            jnp.asarray(flt, F32).astype(BF16))


def _hyena_spectra(seq, fw, flt_mat, cb):
    frow, w1, b1, w2, b2, w3, b3, fr, w4, dl = fw
    nblk = 2 * seq // FFT_P
    nc = w4.shape[-1]
    full = lambda a: pl.BlockSpec(a.shape, lambda r: (0,) * a.ndim)
    kf = pl.pallas_call(
        functools.partial(_kf_kernel, seq=seq),
        out_shape=jax.ShapeDtypeStruct((2 * seq, nc), F32),
        grid=(nblk,),
        in_specs=[full(frow), full(w1), full(b1), full(w2), full(b2), full(w3), full(b3), full(fr),
                  pl.BlockSpec((None,) + w4.shape[1:], lambda r: (jnp.where(r >= nblk // 2, 0, 1), 0, 0)),
                  full(dl)],
        out_specs=pl.BlockSpec((FFT_P, nc), lambda r: (r, 0)),
        compiler_params=_cparams(("parallel",)),
        name="hyena_filter",
    )(frow, w1, b1, w2, b2, w3, b3, fr, w4, dl)
    return pl.pallas_call(
        _spec_kernel,
        out_shape=jax.ShapeDtypeStruct((nc // cb, nblk - 1, 3 * FFT_P, cb), F32),
        grid=(nblk - 1,),
        in_specs=[pl.BlockSpec((FFT_P, nc), lambda w: (w, 0)),
                  pl.BlockSpec((FFT_P, nc), lambda w: (w + 1, 0)),
                  pl.BlockSpec(flt_mat.shape, lambda w: (0, 0))],
        out_specs=pl.BlockSpec((nc // cb, None, 3 * FFT_P, cb), lambda w: (0, w, 0, 0)),
        compiler_params=_cparams(("parallel",)),
        name="hyena_spectra",
    )(kf, kf, flt_mat)


def _hy_kernel(u_ref, x_ref, cwu_ref, cbu_ref, cwx_ref, cbx_ref, bias_ref, g_ref, fwd_ref, inv_ref,
               o_ref, u_scr, gate_scr, uf_scr, y_scr, *, n, conv_u):
    seq = n * FFT_P
    cb = o_ref.shape[1]

    def sconv(v, w_ref, b_ref):
        row = lax.broadcasted_iota(jnp.int32, v.shape, 0)
        prev = jnp.where(row == 0, 0.0, pltpu.roll(v, 1, 0))
        nxt = jnp.where(row == seq - 1, 0.0, pltpu.roll(v, seq - 1, 0))
        return b_ref[...] + prev * w_ref[0:1, :] + v * w_ref[1:2, :] + nxt * w_ref[2:3, :]

    u_scr[...] = sconv(u_ref[...], cwu_ref, cbu_ref) if conv_u else u_ref[...]
    gate_scr[...] = sconv(x_ref[...], cwx_ref, cbx_ref)
    fwd = fwd_ref[...]
    for b in range(n):
        uf_scr[b] = _dot(fwd, u_scr[b * FFT_P:(b + 1) * FFT_P, :].astype(BF16))

    rc = 32

    def a_body(a, carry):
        for c in range(FFT_P // rc):
            r0 = c * rc

            yre = yim = jnp.zeros((rc, cb), F32)
            for b in range(n):
                w = a - b + (n - 1)
                gre = g_ref[w, pl.ds(r0, rc), :]
                gre2 = g_ref[w, pl.ds(FFT_P + r0, rc), :]
                gim = g_ref[w, pl.ds(2 * FFT_P + r0, rc), :]
                ure = uf_scr[b, pl.ds(r0, rc), :]
                uim = uf_scr[b, pl.ds(FFT_P + r0, rc), :]
                yre = yre + gre * ure - gim * uim
                yim = yim + gre2 * uim + gim * ure
            y_scr[pl.ds(r0, rc), :] = yre
            y_scr[pl.ds(FFT_P + r0, rc), :] = yim
        y = _dot(inv_ref[...], y_scr[...].astype(BF16))
        rows = pl.ds(pl.multiple_of(a * FFT_P, FFT_P), FFT_P)
        ua = u_scr[rows, :]
        o_ref[rows, :] = gate_scr[rows, :] * (y + ua * bias_ref[...])
        return carry

    lax.fori_loop(0, n, a_body, 0)


def _hyena_order(u_arr, u_row0, u_col0, x_arr, x_row0, x_col0, conv_w, conv_b, bias_row, spectra, spec_col0,
                 fwd, inv, nbatch, seq, cb, conv_u):
    n = seq // FFT_P
    d_hy = bias_row.shape[1]
    ncb = d_hy // cb
    nwin = spectra.shape[1]
    grid = (ncb, nbatch)
    in_specs = [
        pl.BlockSpec((None, seq, cb), lambda c, b: (u_col0 + c, u_row0 + b, 0)),
        pl.BlockSpec((None, seq, cb), lambda c, b: (x_col0 + c, x_row0 + b, 0)),
        pl.BlockSpec((3, cb), lambda c, b: (0, u_col0 + c)),
        pl.BlockSpec((1, cb), lambda c, b: (0, u_col0 + c)),
        pl.BlockSpec((3, cb), lambda c, b: (0, x_col0 + c)),
        pl.BlockSpec((1, cb), lambda c, b: (0, x_col0 + c)),
        pl.BlockSpec((1, cb), lambda c, b: (0, c)),
        pl.BlockSpec((None, nwin, 3 * FFT_P, cb), lambda c, b: (spec_col0 + c, 0, 0, 0)),
        pl.BlockSpec(fwd.shape, lambda c, b: (0, 0)),
        pl.BlockSpec(inv.shape, lambda c, b: (0, 0)),
    ]
    return pl.pallas_call(
        functools.partial(_hy_kernel, n=n, conv_u=conv_u),
        out_shape=jax.ShapeDtypeStruct((ncb, nbatch * seq, cb), F32),
        grid=grid,
        in_specs=in_specs,
        out_specs=pl.BlockSpec((None, seq, cb), lambda c, b: (c, b, 0)),
        scratch_shapes=[pltpu.VMEM((seq, cb), F32), pltpu.VMEM((seq, cb), F32),
                        pltpu.VMEM((n, FFT_N, cb), F32), pltpu.VMEM((FFT_N, cb), F32)],
        compiler_params=_cparams(("parallel", "parallel")),
        name="hyena_conv",
    )(u_arr, x_arr, conv_w, conv_b, conv_w, conv_b, bias_row, spectra, fwd, inv)


def _hyena_mix(hy_in, conv_w, conv_b, bias, spectra, fwd, inv, row0, nbatch, seq, cb):
    ncb = bias.shape[1] // cb
    z1 = _hyena_order(hy_in, row0, 0, hy_in, row0, ncb, conv_w, conv_b, bias[0:1], spectra, 0,
                      fwd, inv, nbatch, seq, cb, True)
    return _hyena_order(z1, 0, 0, hy_in, row0, 2 * ncb, conv_w, conv_b, bias[1:2], spectra, ncb,
                        fwd, inv, nbatch, seq, cb, False)


def _ret_kernel(q_ref, k_ref, v_ref, g_ref, cos_ref, sin_ref, df_ref, db_ref, ng_ref, s0f_ref, s0b_ref,
                y_ref, sf_ref, sb_ref, q_scr, k_scr, sb_scr, *, n, rope, has_state):
    c = RET_CHUNK
    dk = q_ref.shape[1]
    lgf = -jnp.exp(df_ref[...])
    lgb = -jnp.exp(db_ref[...])
    ri = lax.broadcasted_iota(jnp.int32, (c, c), 0)
    ci = lax.broadcasted_iota(jnp.int32, (c, c), 1)
    diff = (ri - ci).astype(F32)
    mask = (jnp.where(diff >= 0, jnp.exp(jnp.maximum(diff, 0.0) * lgf), 0.0)
            + jnp.where(diff <= 0, jnp.exp(jnp.maximum(-diff, 0.0) * lgb), 0.0))
    pos = lax.broadcasted_iota(jnp.int32, (c, dk), 0).astype(F32)
    xi_f = jnp.exp((pos + 1.0) * lgf)
    ze_f = jnp.exp((c - 1.0 - pos) * lgf)
    cd_f = jnp.exp(c * lgf)
    xi_b = jnp.exp((c - pos) * lgb)
    ze_b = jnp.exp(pos * lgb)
    cd_b = jnp.exp(c * lgb)

    q = q_ref[...]
    k = k_ref[...]
    if rope:
        cs = cos_ref[...]
        sn = sin_ref[...]
        q = q * cs + pltpu.roll(q, dk // 2, 1) * sn
        k = k * cs + pltpu.roll(k, dk // 2, 1) * sn
    q_scr[...] = q
    k_scr[...] = k * (dk ** -0.5)

    def dot_tn(a, b):
        return lax.dot_general(a, b, (((0,), (0,)), ((), ())), preferred_element_type=F32)

    def chunk(i):
        return pl.ds(pl.multiple_of(i * c, c), c)

    def bwd_body(j, s):
        i = n - 1 - j
        sb_scr[i] = s
        rows = chunk(i)
        kz = (k_scr[rows, :] * ze_b).astype(BF16)
        return s * cd_b + dot_tn(kz, v_ref[rows, :].astype(BF16))

    s0b = s0b_ref[...] if has_state else jnp.zeros((dk, dk), F32)
    unroll = math.gcd(n, 8)
    sb_ref[...] = lax.fori_loop(0, n, bwd_body, s0b, unroll=unroll)

    def fwd_body(i, s):
        rows = chunk(i)
        qb = q_scr[rows, :].astype(BF16)
        kc = k_scr[rows, :]
        vb = v_ref[rows, :].astype(BF16)
        sc = lax.dot_general(qb, kc.astype(BF16), (((1,), (1,)), ((), ())), preferred_element_type=F32)
        o = _dot((sc * mask).astype(BF16), vb)
        o = o + _dot(qb, s.astype(BF16)) * xi_f + _dot(qb, sb_scr[i].astype(BF16)) * xi_b
        gt = g_ref[rows, :]
        y_ref[rows, :] = _rms(o, ng_ref[...]) * (gt * jax.nn.sigmoid(gt))
        return s * cd_f + dot_tn((kc * ze_f).astype(BF16), vb)

    s0f = s0f_ref[...] if has_state else jnp.zeros((dk, dk), F32)
    sf_ref[...] = lax.fori_loop(0, n, fwd_body, s0f, unroll=unroll)


def _retention(q, k, v, g, cos, sin, dec_f, dec_b, norm_g, s0f, s0b, row_blk0, nbatch, seq, rope, has_state):
    nh = N_RET_HEADS
    dk = q.shape[1] // nh
    n = seq // RET_CHUNK
    tok = lambda b, h: (row_blk0 + b, h)
    tspec = pl.BlockSpec((seq, dk), tok)
    rspec = pl.BlockSpec((seq, dk), lambda b, h: (0, 0))
    hspec = pl.BlockSpec((None, 1, dk), lambda b, h: (h, 0, 0))
    sspec = pl.BlockSpec((None, None, None, dk, dk), lambda b, h: (b, 0, h, 0, 0))
    s0spec = sspec if has_state else pl.BlockSpec((None, None, None, dk, dk), lambda b, h: (0, 0, h, 0, 0))
    st_shape = jax.ShapeDtypeStruct((nbatch, 1, nh, dk, dk), F32)
    return pl.pallas_call(
        functools.partial(_ret_kernel, n=n, rope=rope, has_state=has_state),
        out_shape=[jax.ShapeDtypeStruct((nbatch * seq, nh * dk), F32), st_shape, st_shape],
        grid=(nbatch, nh),
        in_specs=[tspec, tspec, tspec, tspec, rspec, rspec, hspec, hspec, hspec, s0spec, s0spec],
        out_specs=[pl.BlockSpec((seq, dk), lambda b, h: (b, h)), sspec, sspec],
        scratch_shapes=[pltpu.VMEM((seq, dk), F32), pltpu.VMEM((seq, dk), F32),
                        pltpu.VMEM((n, dk, dk), F32)],
        compiler_params=_cparams(("parallel", "parallel")),
        name="retention",
    )(q, k, v, g, cos, sin, dec_f, dec_b, norm_g, s0f, s0b)


def _outproj_kernel(yh_ref, yr_ref, x_ref, gt_ref, sh_ref, sc_ref, hg_ref, ng_ref, wo_ref, x1_ref, h2t_ref):
    ncb = yh_ref.shape[0]
    yh = yh_ref[0] if ncb == 1 else jnp.concatenate([yh_ref[c] for c in range(ncb)], axis=1)
    d_hy = yh.shape[1]
    nh = _rms(yh, hg_ref[...]).astype(BF16)
    y = _dot(nh, wo_ref[0:d_hy, :]) + _dot(yr_ref[...].astype(BF16), wo_ref[d_hy:, :])
    x1 = x_ref[...] + gt_ref[...] * y
    x1_ref[...] = x1
    h2 = _rms(x1, ng_ref[...]) * (1.0 + sc_ref[...]) + sh_ref[...]
    h2t_ref[...] = h2.T.astype(BF16)


def _outproj(y_hy, y_ret, x, gt1, sh2, sc2, hy_g, n2_g, wo_bf, grp, tm):
    t, d = x.shape
    row = lambda i: (i, 0)
    modspec = pl.BlockSpec((None, 1, d), lambda i: (grp(i), 0, 0))
    return pl.pallas_call(
        _outproj_kernel,
        out_shape=[jax.ShapeDtypeStruct((t, d), F32), jax.ShapeDtypeStruct((d, t), BF16)],
        grid=(t // tm,),
        in_specs=[pl.BlockSpec((y_hy.shape[0], tm, y_hy.shape[2]), lambda i: (0, i, 0)),
                  pl.BlockSpec((tm, y_ret.shape[1]), row),
                  pl.BlockSpec((tm, d), row), modspec, modspec, modspec,
                  pl.BlockSpec(hy_g.shape, lambda i: (0, 0)),
                  pl.BlockSpec((1, d), lambda i: (0, 0)),
                  pl.BlockSpec(wo_bf.shape, lambda i: (0, 0))],
        out_specs=[pl.BlockSpec((tm, d), row), pl.BlockSpec((d, tm), lambda i: (0, i))],
        compiler_params=_cparams(("parallel",)),
        name="outproj",
    )(y_hy, y_ret, x, gt1, sh2, sc2, hy_g, n2_g, wo_bf)


def _sort_desc(xs):
    xs = list(xs)
    n = len(xs)
    k = 2
    while k <= n:
        j = k // 2
        while j >= 1:
            for i in range(n):
                l = i ^ j
                if l > i:
                    hi, lo = jnp.maximum(xs[i], xs[l]), jnp.minimum(xs[i], xs[l])
                    xs[i], xs[l] = (hi, lo) if (i & k) == 0 else (lo, hi)
            j //= 2
        k *= 2
    return xs


def _merge_top(a, b):
    n = len(a)
    xs = [jnp.maximum(a[i], b[n - 1 - i]) for i in range(n)]
    j = n // 2
    while j >= 1:
        for i in range(n):
            l = i ^ j
            if l > i:
                xs[i], xs[l] = jnp.maximum(xs[i], xs[l]), jnp.minimum(xs[i], xs[l])
        j //= 2
    return xs


def _top_sorted(load, count, k):
    acc = None
    for g0 in range(0, count, k):
        grp = _sort_desc([load(i) for i in range(g0, g0 + k)])
        acc = grp if acc is None else _merge_top(acc, grp)
    return acc


def _staircase(k):
    return [(a, b) for a in range(k) for b in range(k) if (a + 1) * (b + 1) <= k + 1]


def _peer1_kernel(h2t_ref, wq_ref, keys_ref, r2_ref, e2_ref, cnt_ref, e1_ref, q_scr, s1_scr, s2_scr):
    h = pl.program_id(1)
    lt_n = r2_ref.shape[0]
    nk = N_KEYS
    kk = PEER_TOPK

    @pl.when(h == 0)
    def _():
        q_scr[...] = _dot(wq_ref[...], h2t_ref[...]).astype(BF16)

    dq = q_scr.shape[0] // (2 * PEER_HEADS)
    for p, scr in ((0, s1_scr), (1, s2_scr)):
        rows = pl.ds(pl.multiple_of((2 * h + p) * dq, dq), dq)
        s = _dot(keys_ref[2 * h + p], q_scr[rows, :])
        for lt in range(lt_n):
            scr[lt * TILE_PITCH:lt * TILE_PITCH + nk, :] = s[:, lt * LANES:(lt + 1) * LANES]

    ld1 = lambda i: s1_scr[pl.ds(i, lt_n, stride=TILE_PITCH), :]
    ld2 = lambda i: s2_scr[pl.ds(i, lt_n, stride=TILE_PITCH), :]
    v1 = _top_sorted(ld1, nk, kk)
    v2 = _top_sorted(ld2, nk, kk)
    pairs = _staircase(kk)
    cand = [v1[a] + v2[b] for a, b in pairs]
    neg = jnp.full_like(cand[0], -jnp.inf)
    padded = cand + [neg] * (-len(cand) % kk)
    top = _top_sorted(lambda i: padded[i], len(padded), kk)
    tau = top[kk - 1]
    nxt = neg
    for cnd in cand:
        nxt = jnp.maximum(nxt, jnp.where(cnd < tau, cnd, neg))
    thr = 0.5 * (tau + nxt)
    m = v1[0] + v2[0]
    z = jnp.zeros_like(tau)
    for cnd in cand:
        z = z + jnp.where(cnd >= tau, jnp.exp(cnd - m), 0.0)
    inv_z = 1.0 / z

    for i in range(nk):
        x1 = ld1(i)
        gap = thr - x1
        cnt = jnp.zeros_like(x1)
        for b in range(kk):
            cnt = cnt + jnp.where(v2[b] > gap, 1.0, 0.0)
        keep = x1 >= v1[kk - 1]
        cnt_ref[i] = jnp.where(keep, cnt, 0.0)
        e1_ref[i] = jnp.where(keep, jnp.exp(x1 - v1[0]) * inv_z, 0.0)
    for lt in range(lt_n):
        s2 = s2_scr[lt * TILE_PITCH:lt * TILE_PITCH + nk, :]
        rank = jnp.zeros_like(s2)
        for b in range(kk):
            rank = rank + jnp.where(v2[b][lt:lt + 1, :] > s2, 1.0, 0.0)
        r2_ref[lt] = rank.astype(BF16)
        e2_ref[lt] = jnp.where(s2 >= v2[kk - 1][lt:lt + 1, :], jnp.exp(s2 - v2[0][lt:lt + 1, :]), 0.0).astype(BF16)


def _peer1(h2t, wqt_bf, keys_bf):
    d, t = h2t.shape
    nh = PEER_HEADS
    tb = TOK_VREG
    lt_n = tb // LANES
    tile_shape = jax.ShapeDtypeStruct((nh, t // LANES, N_KEYS, LANES), BF16)
    vreg_shape = jax.ShapeDtypeStruct((nh, t // tb, N_KEYS, lt_n, LANES), F32)
    tile_spec = pl.BlockSpec((None, lt_n, N_KEYS, LANES), lambda i, h: (h, i, 0, 0))
    vreg_spec = pl.BlockSpec((None, None, N_KEYS, lt_n, LANES), lambda i, h: (h, i, 0, 0, 0))
    return pl.pallas_call(
        _peer1_kernel,
        out_shape=[tile_shape, tile_shape, vreg_shape, vreg_shape],
        grid=(t // tb, nh),
        in_specs=[pl.BlockSpec((d, tb), lambda i, h: (0, i)),
                  pl.BlockSpec(wqt_bf.shape, lambda i, h: (0, 0)),
                  pl.BlockSpec(keys_bf.shape, lambda i, h: (0, 0, 0))],
        out_specs=[tile_spec, tile_spec, vreg_spec, vreg_spec],
        scratch_shapes=[pltpu.VMEM((wqt_bf.shape[0], tb), BF16),
                        pltpu.VMEM((lt_n * TILE_PITCH, LANES), F32),
                        pltpu.VMEM((lt_n * TILE_PITCH, LANES), F32)],
        compiler_params=_cparams(("parallel", "arbitrary")),
        name="peer_select",
    )(h2t, wqt_bf, keys_bf)


def _gelu_tanh(x):
    k = -2.0 * math.sqrt(2.0 / math.pi) * math.log2(math.e)
    e = jnp.exp2(x * (x * x * (k * 0.044715) + k))
    return x / (1.0 + e)


GATE_ROWS = 2 * SUBLANES
SEL_PITCH = N_KEYS + SUBLANES


def _peer2_kernel(h2t_in, u_ref, vt_ref, r2_in, e2_in, cnt_ref, e1_ref, ot_ref,
                  at_scr, zt_scr, sel_ref, h2t_ref):
    eb = pl.program_id(1)
    lt_n = r2_in.shape[1]
    ig_n = cnt_ref.shape[1]

    @pl.when(eb == 0)
    def _():
        ot_ref[...] = jnp.zeros_like(ot_ref)
        for hh in range(PEER_HEADS):
            sel_ref[:, 2 * hh, 0:N_KEYS, :] = r2_in[hh].astype(F32)
            sel_ref[:, 2 * hh + 1, 0:N_KEYS, :] = e2_in[hh].astype(F32)
        h2t_ref[...] = h2t_in[...]

    tcw = at_scr.shape[2]
    ntc = lt_n * LANES // tcw
    lt_per = tcw // LANES

    def row(ref, hh, ig, lt):
        return jnp.broadcast_to(ref[hh, ig, lt:lt + 1, :], (GATE_ROWS, LANES))

    def scores(tc):
        at_scr[tc % 2] = _dot(u_ref[...], h2t_ref[:, tc * tcw:(tc + 1) * tcw])

    def gates(tc):
        for ig in range(ig_n):
            for l in range(lt_per):
                lt = tc * lt_per + l
                cnt = [row(cnt_ref, hh, ig, lt) for hh in range(PEER_HEADS)]
                e1 = [row(e1_ref, hh, ig, lt) for hh in range(PEER_HEADS)]
                for c in range(N_KEYS // GATE_ROWS):
                    js = slice(c * GATE_ROWS, (c + 1) * GATE_ROWS)
                    w = jnp.zeros((GATE_ROWS, LANES), F32)
                    for hh in range(PEER_HEADS):
                        r2 = sel_ref[lt, 2 * hh, js, :]
                        e2 = sel_ref[lt, 2 * hh + 1, js, :]
                        w = w + jnp.where(r2 < cnt[hh], e2, 0.0) * e1[hh]
                    rows = slice(ig * N_KEYS + c * GATE_ROWS, ig * N_KEYS + (c + 1) * GATE_ROWS)
                    a = at_scr[tc % 2, rows, l * LANES:(l + 1) * LANES]
                    zt_scr[rows, lt * LANES:(lt + 1) * LANES] = (w * _gelu_tanh(a)).astype(BF16)

    def combine(tc):
        cols = slice(tc * tcw, (tc + 1) * tcw)
        ot_ref[:, cols] += _dot(vt_ref[...], zt_scr[:, cols])

    scores(0)
    for tc in range(ntc):
        if tc + 1 < ntc:
            scores(tc + 1)
        gates(tc)
        combine(tc)


def _peer2(h2t, u_bf, vt_bf, r2, e2, cnt, e1, eblk):
    d, t = h2t.shape
    ne = u_bf.shape[0]
    nh = PEER_HEADS
    tb = TOK_VREG
    lt_n = tb // LANES
    ig_n = eblk // N_KEYS
    tile_spec = pl.BlockSpec((nh, lt_n, N_KEYS, LANES), lambda i, e: (0, i, 0, 0))
    vreg_spec = pl.BlockSpec((nh, None, ig_n, lt_n, LANES), lambda i, e: (0, i, e, 0, 0))
    return pl.pallas_call(
        _peer2_kernel,
        out_shape=jax.ShapeDtypeStruct((d, t), F32),
        grid=(t // tb, ne // eblk),
        in_specs=[pl.BlockSpec((d, tb), lambda i, e: (0, i)),
                  pl.BlockSpec((eblk, d), lambda i, e: (e, 0)),
                  pl.BlockSpec((d, eblk), lambda i, e: (0, e)),
                  tile_spec, tile_spec, vreg_spec, vreg_spec],
        out_specs=pl.BlockSpec((d, tb), lambda i, e: (0, i)),
        scratch_shapes=[pltpu.VMEM((2, eblk, 2 * LANES), F32), pltpu.VMEM((eblk, tb), BF16),
                        pltpu.VMEM((lt_n, 2 * nh, SEL_PITCH, LANES), F32), pltpu.VMEM((d, tb), BF16)],
        compiler_params=_cparams(("parallel", "arbitrary")),
        name="peer_experts",
    )(h2t, u_bf, vt_bf, r2, e2, cnt, e1)


def _final_kernel(x1_ref, ot_ref, gt_ref, fg_ref, y_ref):
    y_ref[...] = _rms(x1_ref[...] + gt_ref[...] * ot_ref[...].T, fg_ref[...])


def _final(x1, ot, gt2, final_g, grp, tm):
    t, d = x1.shape
    row = lambda i: (i, 0)
    return pl.pallas_call(
        _final_kernel,
        out_shape=jax.ShapeDtypeStruct((t, d), F32),
        grid=(t // tm,),
        in_specs=[pl.BlockSpec((tm, d), row), pl.BlockSpec((d, tm), lambda i: (0, i)),
                  pl.BlockSpec((None, 1, d), lambda i: (grp(i), 0, 0)),
                  pl.BlockSpec((1, d), lambda i: (0, 0))],
        out_specs=pl.BlockSpec((tm, d), row),
        compiler_params=_cparams(("parallel",)),
        name="final_norm",
    )(x1, ot, gt2, final_g)


def _rope_tables(seq, dk):
    rows = seq // GRID_W
    r, c = jnp.meshgrid(jnp.arange(rows, dtype=F32), jnp.arange(GRID_W, dtype=F32), indexing='ij')
    r = r.reshape(-1)
    c = c.reshape(-1)
    nf = dk // 4
    inv = ROPE_BASE ** (-jnp.arange(nf, dtype=F32) / nf)
    ang = jnp.concatenate([r[:, None] * inv, c[:, None] * inv], axis=-1)
    cos = jnp.concatenate([jnp.cos(ang), jnp.cos(ang)], axis=-1)
    sin = jnp.concatenate([-jnp.sin(ang), jnp.sin(ang)], axis=-1)
    return cos, sin


def _pad2(a, rows, cols):
    return jnp.pad(a, ((0, rows - a.shape[0]), (0, cols - a.shape[1])))


def kernel(x_prompt, x_sample, state_ret_fwd, state_ret_bwd, c, c_ctx, w_ada, b_ada, norm1_g, w_in, hy_conv_w, hy_conv_b, hy_w1, hy_b1, hy_w2, hy_b2, hy_w3, hy_b3, hy_w4, hy_freq, hy_bias, hy_norm_g, ret_decay_fwd, ret_decay_bwd, ret_norm_g, w_out, norm2_g, peer_wq, peer_keys, peer_u, peer_v, final_g):
    b_ctx, l_ctx, d = x_prompt.shape
    b_lat, l_lat, _ = x_sample.shape
    depth = w_ada.shape[0]
    assert depth == 1
    t_ctx, t_lat = b_ctx * l_ctx, b_lat * l_lat
    d_hy = hy_norm_g.shape[1]
    d_ret = ret_norm_g.shape[1]
    dk = d_ret // N_RET_HEADS
    tm = 512

    ngrp = 1 + b_lat
    cond = jnp.concatenate([c_ctx[None, :], c], axis=0)
    cond8 = jnp.pad(cond, ((0, SUBLANES - ngrp), (0, 0)))
    mod = _ada(cond8, w_ada[0], b_ada[0])
    sh1, sc1, gt1, sh2, sc2, gt2 = [m.reshape(SUBLANES, 1, d) for m in jnp.split(mod, 6, axis=-1)]

    fw = hy_w1.shape[-1]
    f = jnp.linspace(1e-4, HY_BANDS - 1, HY_BANDS, dtype=F32)
    frow = jnp.zeros((1, LANES), F32).at[0, 1:1 + HY_BANDS].set(f).at[0, 1 + HY_BANDS:1 + 2 * HY_BANDS].set(f)
    min_decay = math.log(HY_DECAY_TARGET) / HY_SLOW_PCT
    max_decay = math.log(HY_DECAY_TARGET) / HY_FAST_PCT
    deltas = jnp.abs(jnp.linspace(min_decay, max_decay, d_hy, dtype=F32))
    w4 = hy_w4[0].reshape(fw, 2, 2 * d_hy).transpose(1, 0, 2)
    w4 = jnp.pad(w4, ((0, 0), (0, LANES - fw), (0, 0)))
    filt_w = (frow, _pad2(hy_w1[0], LANES, LANES), _pad2(hy_b1, 1, LANES), _pad2(hy_w2[0], LANES, LANES),
              _pad2(hy_b2, 1, LANES), _pad2(hy_w3[0], LANES, LANES), _pad2(hy_b3, 1, LANES),
              _pad2(hy_freq, 1, LANES), w4, jnp.tile(deltas, 2)[None, :])
    fwd_m, inv_m, flt_m = _dft_mats()
    cos_t, sin_t = _rope_tables(l_lat, dk)
    dec_f = jnp.broadcast_to(ret_decay_fwd[0][:, None, None], (N_RET_HEADS, 1, dk))
    dec_b = jnp.broadcast_to(ret_decay_bwd[0][:, None, None], (N_RET_HEADS, 1, dk))
    ng = ret_norm_g[0].reshape(N_RET_HEADS, 1, dk)
    w_in_bf = w_in[0].astype(BF16)
    w_out_bf = w_out[0].astype(BF16)
    wqt = peer_wq[0].T.astype(BF16)
    keys = peer_keys[0].reshape(2 * PEER_HEADS, N_KEYS, -1).astype(BF16)
    u_bf = peer_u[0].astype(BF16)
    vt_bf = peer_v[0].T.astype(BF16)
    fg = final_g.reshape(1, d)

    def trunk(x, nbatch, seq, latent, hy_cb):
        grp = (lambda i: 1 + i // (seq // tm)) if latent else (lambda i: 0)
        hy_in, rq, rk, rv, rg = _inproj(x, sh1, sc1, norm1_g, w_in_bf, grp, tm, 3 * d_hy, d_ret, hy_cb)
        spectra = _hyena_spectra(seq, filt_w, flt_m, hy_cb)
        y_hy = _hyena_mix(hy_in, hy_conv_w[0], hy_conv_b, hy_bias[0], spectra, fwd_m, inv_m, 0, nbatch, seq, hy_cb)
        y_ret, s_f, s_b = _retention(rq, rk, rv, rg, cos_t, sin_t, dec_f, dec_b, ng, state_ret_fwd, state_ret_bwd,
                                     0, nbatch, seq, latent, latent)
        x1, h2t = _outproj(y_hy, y_ret, x, gt1, sh2, sc2, hy_norm_g, norm2_g, w_out_bf, grp, tm)
        r2, e2, cnt, e1 = _peer1(h2t, wqt, keys)
        ot = _peer2(h2t, u_bf, vt_bf, r2, e2, cnt, e1, 1024)
        y = _final(x1, ot, gt2, fg, grp, tm)
        return y.reshape(nbatch, seq, d), s_f, s_b

    y_prompt, sf_new, sb_new = trunk(x_prompt.reshape(t_ctx, d), b_ctx, l_ctx, False, 512)
    y_sample, _, _ = trunk(x_sample.reshape(t_lat, d), b_lat, l_lat, True, 128)
    return (y_prompt, y_sample, sf_new, sb_new)
```

```python
import functools
import math

import numpy as np
import jax
import jax.numpy as jnp
from jax import lax
from jax.experimental import pallas as pl
from jax.experimental.pallas import tpu as pltpu

F32 = jnp.float32
BF16 = jnp.bfloat16

LANES = 128
SUBLANES = 8
VMEM_LIMIT = 56 * 1024 * 1024

EPS = 1e-6
GRID_W = 64
N_RET_HEADS = 4
RET_CHUNK = 128
ROPE_BASE = 10000.0
HY_EMB = 33
HY_BANDS = (HY_EMB - 1) // 2
HY_DECAY_TARGET = 1e-2
HY_FAST_PCT = 0.3
HY_SLOW_PCT = 1.5
N_KEYS = 128
PEER_HEADS = 8
PEER_TOPK = 16
FFT_P = 256
FFT_N = 2 * FFT_P
SPEC_PITCH = FFT_P + SUBLANES
SPEC_HEAD = 2 * SUBLANES
SPEC_ROWS = 2 * SPEC_PITCH + SPEC_HEAD
TOK_VREG = SUBLANES * LANES
TILE_PITCH = 136


def _cparams(sem, vmem=VMEM_LIMIT, flags=None):
    return pltpu.CompilerParams(dimension_semantics=sem, vmem_limit_bytes=vmem, flags=flags)


def _dot(a, b):
    return jnp.dot(a, b, preferred_element_type=F32)


def _split(a):
    hi = a.astype(BF16)
    lo = (a - hi.astype(F32)).astype(BF16)
    return hi, lo


def _dot3(a, b):
    ah, al = _split(a)
    bh, bl = _split(b)
    return _dot(ah, bh) + (_dot(al, bh) + _dot(ah, bl))


def _rms(x, g):
    return x * lax.rsqrt(jnp.mean(x * x, axis=-1, keepdims=True) + EPS) * g


def _ada_kernel(c_ref, w_ref, b_ref, o_ref):
    c = c_ref[...]
    s = c * jax.nn.sigmoid(c)
    o_ref[...] = _dot3(s, w_ref[...]) + b_ref[...]


def _ada(cond8, w_ada, b_ada):
    d, n = w_ada.shape
    tn = 1536
    return pl.pallas_call(
        _ada_kernel,
        out_shape=jax.ShapeDtypeStruct((cond8.shape[0], n), F32),
        grid=(n // tn,),
        in_specs=[pl.BlockSpec((cond8.shape[0], d), lambda j: (0, 0)),
                  pl.BlockSpec((d, tn), lambda j: (0, j)),
                  pl.BlockSpec((1, tn), lambda j: (0, j))],
        out_specs=pl.BlockSpec((cond8.shape[0], tn), lambda j: (0, j)),
        compiler_params=_cparams(("parallel",)),
        name="ada",
    )(cond8, w_ada, b_ada.reshape(1, n))


def _inproj_kernel(x_ref, sh_ref, sc_ref, g_ref, w_ref, hy_ref, q_ref, k_ref, v_ref, gg_ref, *, d_hy3, d_ret):
    h = _rms(x_ref[...], g_ref[...]) * (1.0 + sc_ref[...]) + sh_ref[...]
    hb = h.astype(BF16)
    hy = _dot(hb, w_ref[:, 0:d_hy3])
    cb = hy_ref.shape[2]
    for c in range(hy_ref.shape[0]):
        hy_ref[c] = hy[:, c * cb:(c + 1) * cb]
    for i, o_ref in enumerate((q_ref, k_ref, v_ref, gg_ref)):
        o_ref[...] = _dot(hb, w_ref[:, d_hy3 + i * d_ret:d_hy3 + (i + 1) * d_ret])


def _inproj(x, sh, sc, g, w_bf, grp, tm, d_hy3, d_ret, hy_cb):
    t, d = x.shape
    row = lambda i: (i, 0)
    modspec = pl.BlockSpec((None, 1, d), lambda i: (grp(i), 0, 0))
    outs = [jax.ShapeDtypeStruct((d_hy3 // hy_cb, t, hy_cb), F32)] + [jax.ShapeDtypeStruct((t, d_ret), F32)] * 4
    return pl.pallas_call(
        functools.partial(_inproj_kernel, d_hy3=d_hy3, d_ret=d_ret),
        out_shape=outs,
        grid=(t // tm,),
        in_specs=[pl.BlockSpec((tm, d), row), modspec, modspec,
                  pl.BlockSpec((1, d), lambda i: (0, 0)),
                  pl.BlockSpec(w_bf.shape, lambda i: (0, 0))],
        out_specs=[pl.BlockSpec((d_hy3 // hy_cb, tm, hy_cb), lambda i: (0, i, 0))]
        + [pl.BlockSpec((tm, d_ret), row)] * 4,
        compiler_params=_cparams(("parallel",)),
        name="inproj",
    )(x, sh, sc, g, w_bf)


def _kf_kernel(frow_ref, w1_ref, b1_ref, w2_ref, b2_ref, w3_ref, b3_ref, fr_ref, w4_ref, dl_ref, o_ref, *, seq):
    r = pl.program_id(0)
    rows = o_ref.shape[0]
    i = r * rows + lax.broadcasted_iota(jnp.int32, (rows, LANES), 0)
    pos = jnp.abs(i - seq).astype(F32)
    t = pos / (seq - 1.0)
    ang = frow_ref[...] * ((2.0 * math.pi) * pos / seq)
    lane = lax.broadcasted_iota(jnp.int32, (rows, LANES), 1)
    z = jnp.where(lane == 0, t,
                  jnp.where(lane <= HY_BANDS, jnp.cos(ang),
                            jnp.where(lane <= 2 * HY_BANDS, -jnp.sin(ang), 0.0)))
    fr = fr_ref[...]
    h = jnp.sin(fr * (_dot3(z, w1_ref[...]) + b1_ref[...]))
    h = jnp.sin(fr * (_dot3(h, w2_ref[...]) + b2_ref[...]))
    h = jnp.sin(fr * (_dot3(h, w3_ref[...]) + b3_ref[...]))
    h4 = _dot3(h, w4_ref[...])
    nc = o_ref.shape[1]
    iw = r * rows + lax.broadcasted_iota(jnp.int32, (rows, nc), 0)
    tw = jnp.abs(iw - seq).astype(F32) / (seq - 1.0)
    win = jnp.exp(-tw * dl_ref[...])
    o_ref[...] = jnp.where(iw == 0, 0.0, h4 * win)


def _spec_kernel(k0_ref, k1_ref, a_ref, o_ref):
    a = a_ref[...]
    s = (_dot(a[:, :FFT_P], k0_ref[...].astype(BF16))
         + _dot(a[:, FFT_P:], k1_ref[...].astype(BF16)))
    cb = o_ref.shape[2]
    for c in range(o_ref.shape[0]):
        cols = slice(c * cb, (c + 1) * cb)
        for p in range(2):
            r0 = p * SPEC_PITCH
            o_ref[c, r0:r0 + FFT_P, :] = s[p * FFT_P:(p + 1) * FFT_P, cols]
            o_ref[c, r0 + FFT_P:r0 + SPEC_PITCH, :] = jnp.zeros((SPEC_PITCH - FFT_P, cb), F32)
        o_ref[c, 2 * SPEC_PITCH:SPEC_ROWS, :] = s[2 * FFT_P:2 * FFT_P + SPEC_HEAD, cols]


def _dft_mats():
    f = np.arange(FFT_P)[:, None].astype(np.float64)
    t = np.arange(FFT_P)[None, :].astype(np.float64)
    j = np.arange(FFT_N)[None, :].astype(np.float64)
    w = 2.0 * np.pi / FFT_N
    fwd = np.concatenate([np.cos(w * f * t), -np.sin(w * f * t)], axis=0)
    fwd[FFT_P] = np.cos(np.pi * t[0])
    scale = np.full((1, FFT_P), 2.0 / FFT_N)
    scale[0, 0] = 1.0 / FFT_N
    inv = np.concatenate([np.cos(w * t.T * f.T) * scale, -np.sin(w * t.T * f.T) * scale], axis=1)
    inv[:, FFT_P] = np.cos(np.pi * t[0]) / FFT_N
    sgn = np.where(np.arange(FFT_P) % 2 == 0, 1.0, -1.0)[:, None]
    re = sgn * np.cos(w * f * j)
    im = -sgn * np.sin(w * f * j)
    nyq = np.cos(np.pi * j[0])
    re2 = re.copy()
    re2[0] = nyq
    im[0] = 0.0
    flt = np.concatenate([re, im, re2[:SPEC_HEAD]], axis=0)
    flt[:, 0] = 0.0
    return (jnp.asarray(fwd, F32).astype(BF16), jnp.asarray(inv, F32).astype(BF16),
            jnp.asarray(flt, F32).astype(BF16))


def _hyena_spectra(seq, fw, flt_mat, cb):
    frow, w1, b1, w2, b2, w3, b3, fr, w4, dl = fw
    nblk = 2 * seq // FFT_P
    nc = w4.shape[-1]
    full = lambda a: pl.BlockSpec(a.shape, lambda r: (0,) * a.ndim)
    kf = pl.pallas_call(
        functools.partial(_kf_kernel, seq=seq),
        out_shape=jax.ShapeDtypeStruct((2 * seq, nc), F32),
        grid=(nblk,),
        in_specs=[full(frow), full(w1), full(b1), full(w2), full(b2), full(w3), full(b3), full(fr),
                  pl.BlockSpec((None,) + w4.shape[1:], lambda r: (jnp.where(r >= nblk // 2, 0, 1), 0, 0)),
                  full(dl)],
        out_specs=pl.BlockSpec((FFT_P, nc), lambda r: (r, 0)),
        compiler_params=_cparams(("parallel",)),
        name="hyena_filter",
    )(frow, w1, b1, w2, b2, w3, b3, fr, w4, dl)
    return pl.pallas_call(
        _spec_kernel,
        out_shape=jax.ShapeDtypeStruct((nc // cb, nblk - 1, SPEC_ROWS, cb), F32),
        grid=(nblk - 1,),
        in_specs=[pl.BlockSpec((FFT_P, nc), lambda w: (w, 0)),
                  pl.BlockSpec((FFT_P, nc), lambda w: (w + 1, 0)),
                  pl.BlockSpec(flt_mat.shape, lambda w: (0, 0))],
        out_specs=pl.BlockSpec((nc // cb, None, SPEC_ROWS, cb), lambda w: (0, w, 0, 0)),
        compiler_params=_cparams(("parallel",)),
        name="hyena_spectra",
    )(kf, kf, flt_mat)


def _hy_kernel(u_ref, x_ref, cwu_ref, cbu_ref, cwx_ref, cbx_ref, bias_ref, g_ref, fwd_ref, inv_ref,
               o_ref, u_scr, gate_scr, uf_scr, y_scr, *, n, conv_u):
    seq = n * FFT_P
    cb = o_ref.shape[1]

    def sconv(v, w_ref, b_ref):
        row = lax.broadcasted_iota(jnp.int32, v.shape, 0)
        prev = jnp.where(row == 0, 0.0, pltpu.roll(v, 1, 0))
        nxt = jnp.where(row == seq - 1, 0.0, pltpu.roll(v, seq - 1, 0))
        return b_ref[...] + prev * w_ref[0:1, :] + v * w_ref[1:2, :] + nxt * w_ref[2:3, :]

    u_scr[...] = sconv(u_ref[...], cwu_ref, cbu_ref) if conv_u else u_ref[...]
    gate_scr[...] = sconv(x_ref[...], cwx_ref, cbx_ref)
    fwd = fwd_ref[...]
    for b in range(n):
        uf = _dot(fwd, u_scr[b * FFT_P:(b + 1) * FFT_P, :].astype(BF16))
        uf_scr[b, 0:FFT_P, :] = uf[:FFT_P]
        uf_scr[b, SPEC_PITCH:SPEC_PITCH + FFT_P, :] = uf[FFT_P:]

    rc = SPEC_HEAD if n == 1 else SUBLANES

    def freq_rows(r0, first):
        ure = [uf_scr[b, pl.ds(r0, rc), :] for b in range(n)]
        uim = [uf_scr[b, pl.ds(SPEC_PITCH + r0, rc), :] for b in range(n)]
        for a in range(n):
            yre = yim = None
            for b in range(n):
                w = a - b + (n - 1)
                gre = g_ref[w, pl.ds(r0, rc), :]
                gim = g_ref[w, pl.ds(SPEC_PITCH + r0, rc), :]
                gre2 = g_ref[w, 2 * SPEC_PITCH:2 * SPEC_PITCH + rc, :] if first else gre
                tre = gre * ure[b] - gim * uim[b]
                tim = gre2 * uim[b] + gim * ure[b]
                yre = tre if yre is None else yre + tre
                yim = tim if yim is None else yim + tim
            y_scr[a, pl.ds(r0, rc), :] = yre
            y_scr[a, pl.ds(FFT_P + r0, rc), :] = yim

    freq_rows(0, True)

    def rows_body(i, carry):
        freq_rows(pl.multiple_of(i * rc, rc), False)
        return carry

    lax.fori_loop(1, FFT_P // rc, rows_body, 0)

    def out_body(a, carry):
        y = _dot(inv_ref[...], y_scr[a].astype(BF16))
        rows = pl.ds(pl.multiple_of(a * FFT_P, FFT_P), FFT_P)
        ua = u_scr[rows, :]
        o_ref[rows, :] = gate_scr[rows, :] * (y + ua * bias_ref[...])
        return carry

    lax.fori_loop(0, n, out_body, 0)


def _hyena_order(u_arr, u_row0, u_col0, x_arr, x_row0, x_col0, conv_w, conv_b, bias_row, spectra, spec_col0,
                 fwd, inv, nbatch, seq, cb, conv_u):
    n = seq // FFT_P
    d_hy = bias_row.shape[1]
    ncb = d_hy // cb
    nwin = spectra.shape[1]
    grid = (ncb, nbatch)
    in_specs = [
        pl.BlockSpec((None, seq, cb), lambda c, b: (u_col0 + c, u_row0 + b, 0)),
        pl.BlockSpec((None, seq, cb), lambda c, b: (x_col0 + c, x_row0 + b, 0)),
        pl.BlockSpec((3, cb), lambda c, b: (0, u_col0 + c)),
        pl.BlockSpec((1, cb), lambda c, b: (0, u_col0 + c)),
        pl.BlockSpec((3, cb), lambda c, b: (0, x_col0 + c)),
        pl.BlockSpec((1, cb), lambda c, b: (0, x_col0 + c)),
        pl.BlockSpec((1, cb), lambda c, b: (0, c)),
        pl.BlockSpec((None, nwin, SPEC_ROWS, cb), lambda c, b: (spec_col0 + c, 0, 0, 0)),
        pl.BlockSpec(fwd.shape, lambda c, b: (0, 0)),
        pl.BlockSpec(inv.shape, lambda c, b: (0, 0)),
    ]
    return pl.pallas_call(
        functools.partial(_hy_kernel, n=n, conv_u=conv_u),
        out_shape=jax.ShapeDtypeStruct((ncb, nbatch * seq, cb), F32),
        grid=grid,
        in_specs=in_specs,
        out_specs=pl.BlockSpec((None, seq, cb), lambda c, b: (c, b, 0)),
        scratch_shapes=[pltpu.VMEM((seq, cb), F32), pltpu.VMEM((seq, cb), F32),
                        pltpu.VMEM((n, 2 * SPEC_PITCH, cb), F32), pltpu.VMEM((n, FFT_N, cb), F32)],
        compiler_params=_cparams(("parallel", "parallel")),
        name="hyena_conv",
    )(u_arr, x_arr, conv_w, conv_b, conv_w, conv_b, bias_row, spectra, fwd, inv)


def _hyena_mix(hy_in, conv_w, conv_b, bias, spectra, fwd, inv, row0, nbatch, seq, cb):
    ncb = bias.shape[1] // cb
    z1 = _hyena_order(hy_in, row0, 0, hy_in, row0, ncb, conv_w, conv_b, bias[0:1], spectra, 0,
                      fwd, inv, nbatch, seq, cb, True)
    return _hyena_order(z1, 0, 0, hy_in, row0, 2 * ncb, conv_w, conv_b, bias[1:2], spectra, ncb,
                        fwd, inv, nbatch, seq, cb, False)


def _ret_kernel(q_ref, k_ref, v_ref, g_ref, cos_ref, sin_ref, df_ref, db_ref, ng_ref, s0f_ref, s0b_ref,
                y_ref, sf_ref, sb_ref, q_scr, k_scr, sb_scr, *, n, rope, has_state):
    c = RET_CHUNK
    dk = q_ref.shape[1]
    lgf = -jnp.exp(df_ref[...])
    lgb = -jnp.exp(db_ref[...])
    ri = lax.broadcasted_iota(jnp.int32, (c, c), 0)
    ci = lax.broadcasted_iota(jnp.int32, (c, c), 1)
    diff = (ri - ci).astype(F32)
    mask = (jnp.where(diff >= 0, jnp.exp(jnp.maximum(diff, 0.0) * lgf), 0.0)
            + jnp.where(diff <= 0, jnp.exp(jnp.maximum(-diff, 0.0) * lgb), 0.0))
    pos = lax.broadcasted_iota(jnp.int32, (c, dk), 0).astype(F32)
    xi_f = jnp.exp((pos + 1.0) * lgf)
    ze_f = jnp.exp((c - 1.0 - pos) * lgf)
    cd_f = jnp.exp(c * lgf)
    xi_b = jnp.exp((c - pos) * lgb)
    ze_b = jnp.exp(pos * lgb)
    cd_b = jnp.exp(c * lgb)

    q = q_ref[...]
    k = k_ref[...]
    if rope:
        cs = cos_ref[...]
        sn = sin_ref[...]
        q = q * cs + pltpu.roll(q, dk // 2, 1) * sn
        k = k * cs + pltpu.roll(k, dk // 2, 1) * sn
    q_scr[...] = q
    k_scr[...] = k * (dk ** -0.5)

    def dot_tn(a, b):
        return lax.dot_general(a, b, (((0,), (0,)), ((), ())), preferred_element_type=F32)

    def chunk(i):
        return pl.ds(pl.multiple_of(i * c, c), c)

    def bwd_body(j, s):
        i = n - 1 - j
        sb_scr[i] = s
        rows = chunk(i)
        kz = (k_scr[rows, :] * ze_b).astype(BF16)
        return s * cd_b + dot_tn(kz, v_ref[rows, :].astype(BF16))

    s0b = s0b_ref[...] if has_state else jnp.zeros((dk, dk), F32)
    unroll = math.gcd(n, 8)
    sb_ref[...] = lax.fori_loop(0, n, bwd_body, s0b, unroll=unroll)

    def fwd_body(i, s):
        rows = chunk(i)
        qb = q_scr[rows, :].astype(BF16)
        kc = k_scr[rows, :]
        vb = v_ref[rows, :].astype(BF16)
        sc = lax.dot_general(qb, kc.astype(BF16), (((1,), (1,)), ((), ())), preferred_element_type=F32)
        o = _dot((sc * mask).astype(BF16), vb)
        o = o + _dot(qb, s.astype(BF16)) * xi_f + _dot(qb, sb_scr[i].astype(BF16)) * xi_b
        gt = g_ref[rows, :]
        y_ref[rows, :] = _rms(o, ng_ref[...]) * (gt * jax.nn.sigmoid(gt))
        return s * cd_f + dot_tn((kc * ze_f).astype(BF16), vb)

    s0f = s0f_ref[...] if has_state else jnp.zeros((dk, dk), F32)
    sf_ref[...] = lax.fori_loop(0, n, fwd_body, s0f, unroll=unroll)


def _retention(q, k, v, g, cos, sin, dec_f, dec_b, norm_g, s0f, s0b, row_blk0, nbatch, seq, rope, has_state):
    nh = N_RET_HEADS
    dk = q.shape[1] // nh
    n = seq // RET_CHUNK
    tok = lambda b, h: (row_blk0 + b, h)
    tspec = pl.BlockSpec((seq, dk), tok)
    rspec = pl.BlockSpec((seq, dk), lambda b, h: (0, 0))
    hspec = pl.BlockSpec((None, 1, dk), lambda b, h: (h, 0, 0))
    sspec = pl.BlockSpec((None, None, None, dk, dk), lambda b, h: (b, 0, h, 0, 0))
    s0spec = sspec if has_state else pl.BlockSpec((None, None, None, dk, dk), lambda b, h: (0, 0, h, 0, 0))
    st_shape = jax.ShapeDtypeStruct((nbatch, 1, nh, dk, dk), F32)
    return pl.pallas_call(
        functools.partial(_ret_kernel, n=n, rope=rope, has_state=has_state),
        out_shape=[jax.ShapeDtypeStruct((nbatch * seq, nh * dk), F32), st_shape, st_shape],
        grid=(nbatch, nh),
        in_specs=[tspec, tspec, tspec, tspec, rspec, rspec, hspec, hspec, hspec, s0spec, s0spec],
        out_specs=[pl.BlockSpec((seq, dk), lambda b, h: (b, h)), sspec, sspec],
        scratch_shapes=[pltpu.VMEM((seq, dk), F32), pltpu.VMEM((seq, dk), F32),
                        pltpu.VMEM((n, dk, dk), F32)],
        compiler_params=_cparams(("parallel", "parallel")),
        name="retention",
    )(q, k, v, g, cos, sin, dec_f, dec_b, norm_g, s0f, s0b)


def _outproj_kernel(yh_ref, yr_ref, x_ref, gt_ref, sh_ref, sc_ref, hg_ref, ng_ref, wo_ref, x1_ref, h2t_ref):
    ncb = yh_ref.shape[0]
    yh = yh_ref[0] if ncb == 1 else jnp.concatenate([yh_ref[c] for c in range(ncb)], axis=1)
    d_hy = yh.shape[1]
    nh = _rms(yh, hg_ref[...]).astype(BF16)
    y = _dot(nh, wo_ref[0:d_hy, :]) + _dot(yr_ref[...].astype(BF16), wo_ref[d_hy:, :])
    x1 = x_ref[...] + gt_ref[...] * y
    x1_ref[...] = x1
    h2 = _rms(x1, ng_ref[...]) * (1.0 + sc_ref[...]) + sh_ref[...]
    h2t_ref[...] = h2.T.astype(BF16)


def _outproj(y_hy, y_ret, x, gt1, sh2, sc2, hy_g, n2_g, wo_bf, grp, tm):
    t, d = x.shape
    row = lambda i: (i, 0)
    modspec = pl.BlockSpec((None, 1, d), lambda i: (grp(i), 0, 0))
    return pl.pallas_call(
        _outproj_kernel,
        out_shape=[jax.ShapeDtypeStruct((t, d), F32), jax.ShapeDtypeStruct((d, t), BF16)],
        grid=(t // tm,),
        in_specs=[pl.BlockSpec((y_hy.shape[0], tm, y_hy.shape[2]), lambda i: (0, i, 0)),
                  pl.BlockSpec((tm, y_ret.shape[1]), row),
                  pl.BlockSpec((tm, d), row), modspec, modspec, modspec,
                  pl.BlockSpec(hy_g.shape, lambda i: (0, 0)),
                  pl.BlockSpec((1, d), lambda i: (0, 0)),
                  pl.BlockSpec(wo_bf.shape, lambda i: (0, 0))],
        out_specs=[pl.BlockSpec((tm, d), row), pl.BlockSpec((d, tm), lambda i: (0, i))],
        compiler_params=_cparams(("parallel",)),
        name="outproj",
    )(y_hy, y_ret, x, gt1, sh2, sc2, hy_g, n2_g, wo_bf)


def _sort_desc(xs):
    xs = list(xs)
    n = len(xs)
    k = 2
    while k <= n:
        j = k // 2
        while j >= 1:
            for i in range(n):
                l = i ^ j
                if l > i:
                    hi, lo = jnp.maximum(xs[i], xs[l]), jnp.minimum(xs[i], xs[l])
                    xs[i], xs[l] = (hi, lo) if (i & k) == 0 else (lo, hi)
            j //= 2
        k *= 2
    return xs


def _merge_top(a, b):
    n = len(a)
    xs = [jnp.maximum(a[i], b[n - 1 - i]) for i in range(n)]
    j = n // 2
    while j >= 1:
        for i in range(n):
            l = i ^ j
            if l > i:
                xs[i], xs[l] = jnp.maximum(xs[i], xs[l]), jnp.minimum(xs[i], xs[l])
        j //= 2
    return xs


def _top_sorted(load, count, k):
    acc = None
    for g0 in range(0, count, k):
        grp = _sort_desc([load(i) for i in range(g0, g0 + k)])
        acc = grp if acc is None else _merge_top(acc, grp)
    return acc


def _staircase(k):
    return [(a, b) for a in range(k) for b in range(k) if (a + 1) * (b + 1) <= k + 1]


def _peer1_kernel(h2t_ref, wq_ref, keys_ref, r2_ref, e2_ref, cnt_ref, e1_ref, q_scr, s1_scr, s2_scr):
    h = pl.program_id(1)
    lt_n = r2_ref.shape[0]
    nk = N_KEYS
    kk = PEER_TOPK

    @pl.when(h == 0)
    def _():
        q_scr[...] = _dot(wq_ref[...], h2t_ref[...]).astype(BF16)

    dq = q_scr.shape[0] // (2 * PEER_HEADS)
    for p, scr in ((0, s1_scr), (1, s2_scr)):
        rows = pl.ds(pl.multiple_of((2 * h + p) * dq, dq), dq)
        s = _dot(keys_ref[2 * h + p], q_scr[rows, :])
        for lt in range(lt_n):
            scr[lt * TILE_PITCH:lt * TILE_PITCH + nk, :] = s[:, lt * LANES:(lt + 1) * LANES]

    ld1 = lambda i: s1_scr[pl.ds(i, lt_n, stride=TILE_PITCH), :]
    ld2 = lambda i: s2_scr[pl.ds(i, lt_n, stride=TILE_PITCH), :]
    v1 = _top_sorted(ld1, nk, kk)
    v2 = _top_sorted(ld2, nk, kk)
    pairs = _staircase(kk)
    cand = [v1[a] + v2[b] for a, b in pairs]
    neg = jnp.full_like(cand[0], -jnp.inf)
    padded = cand + [neg] * (-len(cand) % kk)
    top = _top_sorted(lambda i: padded[i], len(padded), kk)
    tau = top[kk - 1]
    nxt = neg
    for cnd in cand:
        nxt = jnp.maximum(nxt, jnp.where(cnd < tau, cnd, neg))
    thr = 0.5 * (tau + nxt)
    m = v1[0] + v2[0]
    z = jnp.zeros_like(tau)
    for cnd in cand:
        z = z + jnp.where(cnd >= tau, jnp.exp(cnd - m), 0.0)
    inv_z = 1.0 / z

    for i in range(nk):
        x1 = ld1(i)
        gap = thr - x1
        cnt = jnp.zeros_like(x1)
        for b in range(kk):
            cnt = cnt + jnp.where(v2[b] > gap, 1.0, 0.0)
        keep = x1 >= v1[kk - 1]
        cnt_ref[i] = jnp.where(keep, cnt, 0.0)
        e1_ref[i] = jnp.where(keep, jnp.exp(x1 - v1[0]) * inv_z, 0.0)
    for lt in range(lt_n):
        s2 = s2_scr[lt * TILE_PITCH:lt * TILE_PITCH + nk, :]
        rank = jnp.zeros_like(s2)
        for b in range(kk):
            rank = rank + jnp.where(v2[b][lt:lt + 1, :] > s2, 1.0, 0.0)
        r2_ref[lt] = rank.astype(BF16)
        e2_ref[lt] = jnp.where(s2 >= v2[kk - 1][lt:lt + 1, :], jnp.exp(s2 - v2[0][lt:lt + 1, :]), 0.0).astype(BF16)


def _peer1(h2t, wqt_bf, keys_bf):
    d, t = h2t.shape
    nh = PEER_HEADS
    tb = TOK_VREG
    lt_n = tb // LANES
    tile_shape = jax.ShapeDtypeStruct((nh, t // LANES, N_KEYS, LANES), BF16)
    vreg_shape = jax.ShapeDtypeStruct((nh, t // tb, N_KEYS, lt_n, LANES), F32)
    tile_spec = pl.BlockSpec((None, lt_n, N_KEYS, LANES), lambda i, h: (h, i, 0, 0))
    vreg_spec = pl.BlockSpec((None, None, N_KEYS, lt_n, LANES), lambda i, h: (h, i, 0, 0, 0))
    return pl.pallas_call(
        _peer1_kernel,
        out_shape=[tile_shape, tile_shape, vreg_shape, vreg_shape],
        grid=(t // tb, nh),
        in_specs=[pl.BlockSpec((d, tb), lambda i, h: (0, i)),
                  pl.BlockSpec(wqt_bf.shape, lambda i, h: (0, 0)),
                  pl.BlockSpec(keys_bf.shape, lambda i, h: (0, 0, 0))],
        out_specs=[tile_spec, tile_spec, vreg_spec, vreg_spec],
        scratch_shapes=[pltpu.VMEM((wqt_bf.shape[0], tb), BF16),
                        pltpu.VMEM((lt_n * TILE_PITCH, LANES), F32),
                        pltpu.VMEM((lt_n * TILE_PITCH, LANES), F32)],
        compiler_params=_cparams(("parallel", "arbitrary")),
        name="peer_select",
    )(h2t, wqt_bf, keys_bf)


def _gelu_tanh(x):
    k = -2.0 * math.sqrt(2.0 / math.pi) * math.log2(math.e)
    e = jnp.exp2(x * (x * x * (k * 0.044715) + k))
    return x / (1.0 + e)


GATE_ROWS = 2 * SUBLANES
SEL_PITCH = N_KEYS + SUBLANES


def _peer2_kernel(h2t_in, u_ref, vt_ref, r2_in, e2_in, cnt_ref, e1_ref, x1_ref, gt_ref, fg_ref, y_ref,
                  at_scr, zt_scr, sel_ref, h2t_ref, ot_ref):
    eb = pl.program_id(1)
    lt_n = r2_in.shape[1]
    ig_n = cnt_ref.shape[1]

    @pl.when(eb == 0)
    def _():
        ot_ref[...] = jnp.zeros_like(ot_ref)
        for hh in range(PEER_HEADS):
            sel_ref[:, 2 * hh, 0:N_KEYS, :] = r2_in[hh].astype(F32)
            sel_ref[:, 2 * hh + 1, 0:N_KEYS, :] = e2_in[hh].astype(F32)
        h2t_ref[...] = h2t_in[...]

    tcw = at_scr.shape[2]
    ntc = lt_n * LANES // tcw
    lt_per = tcw // LANES

    def row(ref, hh, ig, lt):
        return jnp.broadcast_to(ref[hh, ig, lt:lt + 1, :], (GATE_ROWS, LANES))

    def scores(tc):
        at_scr[tc % 2] = _dot(u_ref[...], h2t_ref[:, tc * tcw:(tc + 1) * tcw])

    def gates(tc):
        for ig in range(ig_n):
            for l in range(lt_per):
                lt = tc * lt_per + l
                cnt = [row(cnt_ref, hh, ig, lt) for hh in range(PEER_HEADS)]
                e1 = [row(e1_ref, hh, ig, lt) for hh in range(PEER_HEADS)]
                for c in range(N_KEYS // GATE_ROWS):
                    js = slice(c * GATE_ROWS, (c + 1) * GATE_ROWS)
                    w = jnp.zeros((GATE_ROWS, LANES), F32)
                    for hh in range(PEER_HEADS):
                        r2 = sel_ref[lt, 2 * hh, js, :]
                        e2 = sel_ref[lt, 2 * hh + 1, js, :]
                        w = w + jnp.where(r2 < cnt[hh], e2, 0.0) * e1[hh]
                    rows = slice(ig * N_KEYS + c * GATE_ROWS, ig * N_KEYS + (c + 1) * GATE_ROWS)
                    a = at_scr[tc % 2, rows, l * LANES:(l + 1) * LANES]
                    zt_scr[rows, lt * LANES:(lt + 1) * LANES] = (w * _gelu_tanh(a)).astype(BF16)

    def combine(tc):
        cols = slice(tc * tcw, (tc + 1) * tcw)
        ot_ref[:, cols] += _dot(vt_ref[...], zt_scr[:, cols])

    scores(0)
    for tc in range(ntc):
        if tc + 1 < ntc:
            scores(tc + 1)
        gates(tc)
        combine(tc)

    @pl.when(eb == pl.num_programs(1) - 1)
    def _():
        y_ref[...] = _rms(x1_ref[...] + gt_ref[...] * ot_ref[...].T, fg_ref[...])


def _peer2(h2t, u_bf, vt_bf, r2, e2, cnt, e1, x1, gt2, final_g, grp, eblk):
    d, t = h2t.shape
    ne = u_bf.shape[0]
    nh = PEER_HEADS
    tb = TOK_VREG
    lt_n = tb // LANES
    ig_n = eblk // N_KEYS
    tile_spec = pl.BlockSpec((nh, lt_n, N_KEYS, LANES), lambda i, e: (0, i, 0, 0))
    vreg_spec = pl.BlockSpec((nh, None, ig_n, lt_n, LANES), lambda i, e: (0, i, e, 0, 0))
    return pl.pallas_call(
        _peer2_kernel,
        out_shape=jax.ShapeDtypeStruct((t, d), F32),
        grid=(t // tb, ne // eblk),
        in_specs=[pl.BlockSpec((d, tb), lambda i, e: (0, i)),
                  pl.BlockSpec((eblk, d), lambda i, e: (e, 0)),
                  pl.BlockSpec((d, eblk), lambda i, e: (0, e)),
                  tile_spec, tile_spec, vreg_spec, vreg_spec,
                  pl.BlockSpec((tb, d), lambda i, e: (i, 0), pipeline_mode=pl.Buffered(1)),
                  pl.BlockSpec((None, 1, d), lambda i, e: (grp(i), 0, 0)),
                  pl.BlockSpec((1, d), lambda i, e: (0, 0))],
        out_specs=pl.BlockSpec((tb, d), lambda i, e: (i, 0)),
        scratch_shapes=[pltpu.VMEM((2, eblk, 2 * LANES), F32), pltpu.VMEM((eblk, tb), BF16),
                        pltpu.VMEM((lt_n, 2 * nh, SEL_PITCH, LANES), F32), pltpu.VMEM((d, tb), BF16),
                        pltpu.VMEM((d, tb), F32)],
        compiler_params=_cparams(("parallel", "arbitrary")),
        name="peer_experts",
    )(h2t, u_bf, vt_bf, r2, e2, cnt, e1, x1, gt2, final_g)


def _rope_tables(seq, dk):
    rows = seq // GRID_W
    r, c = jnp.meshgrid(jnp.arange(rows, dtype=F32), jnp.arange(GRID_W, dtype=F32), indexing='ij')
    r = r.reshape(-1)
    c = c.reshape(-1)
    nf = dk // 4
    inv = ROPE_BASE ** (-jnp.arange(nf, dtype=F32) / nf)
    ang = jnp.concatenate([r[:, None] * inv, c[:, None] * inv], axis=-1)
    cos = jnp.concatenate([jnp.cos(ang), jnp.cos(ang)], axis=-1)
    sin = jnp.concatenate([-jnp.sin(ang), jnp.sin(ang)], axis=-1)
    return cos, sin


def _pad2(a, rows, cols):
    return jnp.pad(a, ((0, rows - a.shape[0]), (0, cols - a.shape[1])))


def kernel(x_prompt, x_sample, state_ret_fwd, state_ret_bwd, c, c_ctx, w_ada, b_ada, norm1_g, w_in, hy_conv_w, hy_conv_b, hy_w1, hy_b1, hy_w2, hy_b2, hy_w3, hy_b3, hy_w4, hy_freq, hy_bias, hy_norm_g, ret_decay_fwd, ret_decay_bwd, ret_norm_g, w_out, norm2_g, peer_wq, peer_keys, peer_u, peer_v, final_g):
    b_ctx, l_ctx, d = x_prompt.shape
    b_lat, l_lat, _ = x_sample.shape
    depth = w_ada.shape[0]
    assert depth == 1
    t_ctx, t_lat = b_ctx * l_ctx, b_lat * l_lat
    d_hy = hy_norm_g.shape[1]
    d_ret = ret_norm_g.shape[1]
    dk = d_ret // N_RET_HEADS
    tm = 512

    ngrp = 1 + b_lat
    cond = jnp.concatenate([c_ctx[None, :], c], axis=0)
    cond8 = jnp.pad(cond, ((0, SUBLANES - ngrp), (0, 0)))
    mod = _ada(cond8, w_ada[0], b_ada[0])
    sh1, sc1, gt1, sh2, sc2, gt2 = [m.reshape(SUBLANES, 1, d) for m in jnp.split(mod, 6, axis=-1)]

    fw = hy_w1.shape[-1]
    f = jnp.linspace(1e-4, HY_BANDS - 1, HY_BANDS, dtype=F32)
    frow = jnp.zeros((1, LANES), F32).at[0, 1:1 + HY_BANDS].set(f).at[0, 1 + HY_BANDS:1 + 2 * HY_BANDS].set(f)
    min_decay = math.log(HY_DECAY_TARGET) / HY_SLOW_PCT
    max_decay = math.log(HY_DECAY_TARGET) / HY_FAST_PCT
    deltas = jnp.abs(jnp.linspace(min_decay, max_decay, d_hy, dtype=F32))
    w4 = hy_w4[0].reshape(fw, 2, 2 * d_hy).transpose(1, 0, 2)
    w4 = jnp.pad(w4, ((0, 0), (0, LANES - fw), (0, 0)))
    filt_w = (frow, _pad2(hy_w1[0], LANES, LANES), _pad2(hy_b1, 1, LANES), _pad2(hy_w2[0], LANES, LANES),
              _pad2(hy_b2, 1, LANES), _pad2(hy_w3[0], LANES, LANES), _pad2(hy_b3, 1, LANES),
              _pad2(hy_freq, 1, LANES), w4, jnp.tile(deltas, 2)[None, :])
    fwd_m, inv_m, flt_m = _dft_mats()
    cos_t, sin_t = _rope_tables(l_lat, dk)
    dec_f = jnp.broadcast_to(ret_decay_fwd[0][:, None, None], (N_RET_HEADS, 1, dk))
    dec_b = jnp.broadcast_to(ret_decay_bwd[0][:, None, None], (N_RET_HEADS, 1, dk))
    ng = ret_norm_g[0].reshape(N_RET_HEADS, 1, dk)
    w_in_bf = w_in[0].astype(BF16)
    w_out_bf = w_out[0].astype(BF16)
    wqt = peer_wq[0].T.astype(BF16)
    keys = peer_keys[0].reshape(2 * PEER_HEADS, N_KEYS, -1).astype(BF16)
    u_bf = peer_u[0].astype(BF16)
    vt_bf = peer_v[0].T.astype(BF16)
    fg = final_g.reshape(1, d)

    def trunk(x, nbatch, seq, latent, hy_cb):
        grp = (lambda i: 1 + i // (seq // tm)) if latent else (lambda i: 0)
        hy_in, rq, rk, rv, rg = _inproj(x, sh1, sc1, norm1_g, w_in_bf, grp, tm, 3 * d_hy, d_ret, hy_cb)
        spectra = _hyena_spectra(seq, filt_w, flt_m, hy_cb)
        y_hy = _hyena_mix(hy_in, hy_conv_w[0], hy_conv_b, hy_bias[0], spectra, fwd_m, inv_m, 0, nbatch, seq, hy_cb)
        y_ret, s_f, s_b = _retention(rq, rk, rv, rg, cos_t, sin_t, dec_f, dec_b, ng, state_ret_fwd, state_ret_bwd,
                                     0, nbatch, seq, latent, latent)
        x1, h2t = _outproj(y_hy, y_ret, x, gt1, sh2, sc2, hy_norm_g, norm2_g, w_out_bf, grp, tm)
        r2, e2, cnt, e1 = _peer1(h2t, wqt, keys)
        grp_tok = (lambda i: 1 + i // (seq // TOK_VREG)) if latent else (lambda i: 0)
        y = _peer2(h2t, u_bf, vt_bf, r2, e2, cnt, e1, x1, gt2, fg, grp_tok, 1024)
        return y.reshape(nbatch, seq, d), s_f, s_b

    y_prompt, sf_new, sb_new = trunk(x_prompt.reshape(t_ctx, d), b_ctx, l_ctx, False, 512)
    y_sample, _, _ = trunk(x_sample.reshape(t_lat, d), b_lat, l_lat, True, 128)
    return (y_prompt, y_sample, sf_new, sb_new)
```

```python
import functools
import math

import numpy as np
import jax
import jax.numpy as jnp
from jax import lax
from jax.experimental import pallas as pl
from jax.experimental.pallas import tpu as pltpu

F32 = jnp.float32
BF16 = jnp.bfloat16

LANES = 128
SUBLANES = 8
VMEM_LIMIT = 56 * 1024 * 1024

EPS = 1e-6
GRID_W = 64
N_RET_HEADS = 4
RET_CHUNK = 128
ROPE_BASE = 10000.0
HY_EMB = 33
HY_BANDS = (HY_EMB - 1) // 2
HY_DECAY_TARGET = 1e-2
HY_FAST_PCT = 0.3
HY_SLOW_PCT = 1.5
N_KEYS = 128
PEER_HEADS = 8
PEER_TOPK = 16
FFT_P = 256
FFT_N = 2 * FFT_P
SPEC_PITCH = FFT_P + SUBLANES
SPEC_HEAD = 2 * SUBLANES
SPEC_ROWS = 2 * SPEC_PITCH + SPEC_HEAD
TOK_VREG = SUBLANES * LANES
TILE_PITCH = 136


def _cparams(sem, vmem=VMEM_LIMIT, flags=None):
    return pltpu.CompilerParams(dimension_semantics=sem, vmem_limit_bytes=vmem, flags=flags)


def _dot(a, b):
    return jnp.dot(a, b, preferred_element_type=F32)


def _split(a):
    hi = a.astype(BF16)
    lo = (a - hi.astype(F32)).astype(BF16)
    return hi, lo


def _dot3(a, b):
    ah, al = _split(a)
    bh, bl = _split(b)
    return _dot(ah, bh) + (_dot(al, bh) + _dot(ah, bl))


def _rms(x, g):
    return x * lax.rsqrt(jnp.mean(x * x, axis=-1, keepdims=True) + EPS) * g


def _ada_kernel(c_ref, w_ref, b_ref, o_ref):
    c = c_ref[...]
    s = c * jax.nn.sigmoid(c)
    o_ref[...] = _dot3(s, w_ref[...]) + b_ref[...]


def _ada(cond8, w_ada, b_ada):
    d, n = w_ada.shape
    tn = 1536
    return pl.pallas_call(
        _ada_kernel,
        out_shape=jax.ShapeDtypeStruct((cond8.shape[0], n), F32),
        grid=(n // tn,),
        in_specs=[pl.BlockSpec((cond8.shape[0], d), lambda j: (0, 0)),
                  pl.BlockSpec((d, tn), lambda j: (0, j)),
                  pl.BlockSpec((1, tn), lambda j: (0, j))],
        out_specs=pl.BlockSpec((cond8.shape[0], tn), lambda j: (0, j)),
        compiler_params=_cparams(("parallel",)),
        name="ada",
    )(cond8, w_ada, b_ada.reshape(1, n))


def _inproj_kernel(x_ref, sh_ref, sc_ref, g_ref, w_ref, hy_ref, q_ref, k_ref, v_ref, gg_ref, *, d_hy3, d_ret):
    h = _rms(x_ref[...], g_ref[...]) * (1.0 + sc_ref[...]) + sh_ref[...]
    hb = h.astype(BF16)
    hy = _dot(hb, w_ref[:, 0:d_hy3])
    cb = hy_ref.shape[2]
    for c in range(hy_ref.shape[0]):
        hy_ref[c] = hy[:, c * cb:(c + 1) * cb]
    for i, o_ref in enumerate((q_ref, k_ref, v_ref, gg_ref)):
        o_ref[...] = _dot(hb, w_ref[:, d_hy3 + i * d_ret:d_hy3 + (i + 1) * d_ret])


def _inproj(x, sh, sc, g, w_bf, grp, tm, d_hy3, d_ret, hy_cb):
    t, d = x.shape
    row = lambda i: (i, 0)
    modspec = pl.BlockSpec((None, 1, d), lambda i: (grp(i), 0, 0))
    outs = [jax.ShapeDtypeStruct((d_hy3 // hy_cb, t, hy_cb), F32)] + [jax.ShapeDtypeStruct((t, d_ret), F32)] * 4
    return pl.pallas_call(
        functools.partial(_inproj_kernel, d_hy3=d_hy3, d_ret=d_ret),
        out_shape=outs,
        grid=(t // tm,),
        in_specs=[pl.BlockSpec((tm, d), row), modspec, modspec,
                  pl.BlockSpec((1, d), lambda i: (0, 0)),
                  pl.BlockSpec(w_bf.shape, lambda i: (0, 0))],
        out_specs=[pl.BlockSpec((d_hy3 // hy_cb, tm, hy_cb), lambda i: (0, i, 0))]
        + [pl.BlockSpec((tm, d_ret), row)] * 4,
        compiler_params=_cparams(("parallel",)),
        name="inproj",
    )(x, sh, sc, g, w_bf)


def _kf_kernel(frow_ref, w1_ref, b1_ref, w2_ref, b2_ref, w3_ref, b3_ref, fr_ref, w4_ref, dl_ref, o_ref, *, seq):
    r = pl.program_id(0)
    rows = o_ref.shape[0]
    i = r * rows + lax.broadcasted_iota(jnp.int32, (rows, LANES), 0)
    pos = jnp.abs(i - seq).astype(F32)
    t = pos / (seq - 1.0)
    ang = frow_ref[...] * ((2.0 * math.pi) * pos / seq)
    lane = lax.broadcasted_iota(jnp.int32, (rows, LANES), 1)
    z = jnp.where(lane == 0, t,
                  jnp.where(lane <= HY_BANDS, jnp.cos(ang),
                            jnp.where(lane <= 2 * HY_BANDS, -jnp.sin(ang), 0.0)))
    fr = fr_ref[...]
    h = jnp.sin(fr * (_dot3(z, w1_ref[...]) + b1_ref[...]))
    h = jnp.sin(fr * (_dot3(h, w2_ref[...]) + b2_ref[...]))
    h = jnp.sin(fr * (_dot3(h, w3_ref[...]) + b3_ref[...]))
    h4 = _dot3(h, w4_ref[...])
    nc = o_ref.shape[1]
    iw = r * rows + lax.broadcasted_iota(jnp.int32, (rows, nc), 0)
    tw = jnp.abs(iw - seq).astype(F32) / (seq - 1.0)
    win = jnp.exp(-tw * dl_ref[...])
    o_ref[...] = jnp.where(iw == 0, 0.0, h4 * win)


def _spec_kernel(k0_ref, k1_ref, a_ref, o_ref):
    a = a_ref[...]
    s = (_dot(a[:, :FFT_P], k0_ref[...].astype(BF16))
         + _dot(a[:, FFT_P:], k1_ref[...].astype(BF16)))
    cb = o_ref.shape[2]
    for c in range(o_ref.shape[0]):
        cols = slice(c * cb, (c + 1) * cb)
        for p in range(2):
            r0 = p * SPEC_PITCH
            o_ref[c, r0:r0 + FFT_P, :] = s[p * FFT_P:(p + 1) * FFT_P, cols]
            o_ref[c, r0 + FFT_P:r0 + SPEC_PITCH, :] = jnp.zeros((SPEC_PITCH - FFT_P, cb), F32)
        o_ref[c, 2 * SPEC_PITCH:SPEC_ROWS, :] = s[2 * FFT_P:2 * FFT_P + SPEC_HEAD, cols]


def _dft_mats():
    f = np.arange(FFT_P)[:, None].astype(np.float64)
    t = np.arange(FFT_P)[None, :].astype(np.float64)
    j = np.arange(FFT_N)[None, :].astype(np.float64)
    w = 2.0 * np.pi / FFT_N
    fwd = np.concatenate([np.cos(w * f * t), -np.sin(w * f * t)], axis=0)
    fwd[FFT_P] = np.cos(np.pi * t[0])
    scale = np.full((1, FFT_P), 2.0 / FFT_N)
    scale[0, 0] = 1.0 / FFT_N
    inv = np.concatenate([np.cos(w * t.T * f.T) * scale, -np.sin(w * t.T * f.T) * scale], axis=1)
    inv[:, FFT_P] = np.cos(np.pi * t[0]) / FFT_N
    sgn = np.where(np.arange(FFT_P) % 2 == 0, 1.0, -1.0)[:, None]
    re = sgn * np.cos(w * f * j)
    im = -sgn * np.sin(w * f * j)
    nyq = np.cos(np.pi * j[0])
    re2 = re.copy()
    re2[0] = nyq
    im[0] = 0.0
    flt = np.concatenate([re, im, re2[:SPEC_HEAD]], axis=0)
    flt[:, 0] = 0.0
    return (jnp.asarray(fwd, F32).astype(BF16), jnp.asarray(inv, F32).astype(BF16),
            jnp.asarray(flt, F32).astype(BF16))


def _hyena_spectra(seq, fw, flt_mat, cb):
    frow, w1, b1, w2, b2, w3, b3, fr, w4, dl = fw
    nblk = 2 * seq // FFT_P
    nc = w4.shape[-1]
    full = lambda a: pl.BlockSpec(a.shape, lambda r: (0,) * a.ndim)
    kf = pl.pallas_call(
        functools.partial(_kf_kernel, seq=seq),
        out_shape=jax.ShapeDtypeStruct((2 * seq, nc), F32),
        grid=(nblk,),
        in_specs=[full(frow), full(w1), full(b1), full(w2), full(b2), full(w3), full(b3), full(fr),
                  pl.BlockSpec((None,) + w4.shape[1:], lambda r: (jnp.where(r >= nblk // 2, 0, 1), 0, 0)),
                  full(dl)],
        out_specs=pl.BlockSpec((FFT_P, nc), lambda r: (r, 0)),
        compiler_params=_cparams(("parallel",)),
        name="hyena_filter",
    )(frow, w1, b1, w2, b2, w3, b3, fr, w4, dl)
    return pl.pallas_call(
        _spec_kernel,
        out_shape=jax.ShapeDtypeStruct((nc // cb, nblk - 1, SPEC_ROWS, cb), F32),
        grid=(nblk - 1,),
        in_specs=[pl.BlockSpec((FFT_P, nc), lambda w: (w, 0)),
                  pl.BlockSpec((FFT_P, nc), lambda w: (w + 1, 0)),
                  pl.BlockSpec(flt_mat.shape, lambda w: (0, 0))],
        out_specs=pl.BlockSpec((nc // cb, None, SPEC_ROWS, cb), lambda w: (0, w, 0, 0)),
        compiler_params=_cparams(("parallel",)),
        name="hyena_spectra",
    )(kf, kf, flt_mat)


def _hy_kernel(u_ref, x_ref, cwu_ref, cbu_ref, cwx_ref, cbx_ref, bias_ref, g_ref, fwd_ref, inv_ref,
               o_ref, u_scr, gate_scr, uf_scr, y_scr, *, n, conv_u):
    seq = n * FFT_P
    cb = o_ref.shape[1]

    def sconv(v, w_ref, b_ref):
        row = lax.broadcasted_iota(jnp.int32, v.shape, 0)
        prev = jnp.where(row == 0, 0.0, pltpu.roll(v, 1, 0))
        nxt = jnp.where(row == seq - 1, 0.0, pltpu.roll(v, seq - 1, 0))
        return b_ref[...] + prev * w_ref[0:1, :] + v * w_ref[1:2, :] + nxt * w_ref[2:3, :]

    u_scr[...] = sconv(u_ref[...], cwu_ref, cbu_ref) if conv_u else u_ref[...]
    gate_scr[...] = sconv(x_ref[...], cwx_ref, cbx_ref)
    fwd = fwd_ref[...]
    for b in range(n):
        uf = _dot(fwd, u_scr[b * FFT_P:(b + 1) * FFT_P, :].astype(BF16))
        uf_scr[b, 0:FFT_P, :] = uf[:FFT_P]
        uf_scr[b, SPEC_PITCH:SPEC_PITCH + FFT_P, :] = uf[FFT_P:]

    rc = SPEC_HEAD if n == 1 else SUBLANES

    def freq_rows(r0, first):
        ure = [uf_scr[b, pl.ds(r0, rc), :] for b in range(n)]
        uim = [uf_scr[b, pl.ds(SPEC_PITCH + r0, rc), :] for b in range(n)]
        for a in range(n):
            yre = yim = None
            for b in range(n):
                w = a - b + (n - 1)
                gre = g_ref[w, pl.ds(r0, rc), :]
                gim = g_ref[w, pl.ds(SPEC_PITCH + r0, rc), :]
                gre2 = g_ref[w, 2 * SPEC_PITCH:2 * SPEC_PITCH + rc, :] if first else gre
                tre = gre * ure[b] - gim * uim[b]
                tim = gre2 * uim[b] + gim * ure[b]
                yre = tre if yre is None else yre + tre
                yim = tim if yim is None else yim + tim
            y_scr[a, pl.ds(r0, rc), :] = yre
            y_scr[a, pl.ds(FFT_P + r0, rc), :] = yim

    freq_rows(0, True)

    def rows_body(i, carry):
        freq_rows(pl.multiple_of(i * rc, rc), False)
        return carry

    lax.fori_loop(1, FFT_P // rc, rows_body, 0)

    def out_body(a, carry):
        y = _dot(inv_ref[...], y_scr[a].astype(BF16))
        rows = pl.ds(pl.multiple_of(a * FFT_P, FFT_P), FFT_P)
        ua = u_scr[rows, :]
        o_ref[rows, :] = gate_scr[rows, :] * (y + ua * bias_ref[...])
        return carry

    lax.fori_loop(0, n, out_body, 0)


def _hyena_order(u_arr, u_row0, u_col0, x_arr, x_row0, x_col0, conv_w, conv_b, bias_row, spectra, spec_col0,
                 fwd, inv, nbatch, seq, cb, conv_u):
    n = seq // FFT_P
    d_hy = bias_row.shape[1]
    ncb = d_hy // cb
    nwin = spectra.shape[1]
    grid = (ncb, nbatch)
    in_specs = [
        pl.BlockSpec((None, seq, cb), lambda c, b: (u_col0 + c, u_row0 + b, 0)),
        pl.BlockSpec((None, seq, cb), lambda c, b: (x_col0 + c, x_row0 + b, 0)),
        pl.BlockSpec((3, cb), lambda c, b: (0, u_col0 + c)),
        pl.BlockSpec((1, cb), lambda c, b: (0, u_col0 + c)),
        pl.BlockSpec((3, cb), lambda c, b: (0, x_col0 + c)),
        pl.BlockSpec((1, cb), lambda c, b: (0, x_col0 + c)),
        pl.BlockSpec((1, cb), lambda c, b: (0, c)),
        pl.BlockSpec((None, nwin, SPEC_ROWS, cb), lambda c, b: (spec_col0 + c, 0, 0, 0)),
        pl.BlockSpec(fwd.shape, lambda c, b: (0, 0)),
        pl.BlockSpec(inv.shape, lambda c, b: (0, 0)),
    ]
    return pl.pallas_call(
        functools.partial(_hy_kernel, n=n, conv_u=conv_u),
        out_shape=jax.ShapeDtypeStruct((ncb, nbatch * seq, cb), F32),
        grid=grid,
        in_specs=in_specs,
        out_specs=pl.BlockSpec((None, seq, cb), lambda c, b: (c, b, 0)),
        scratch_shapes=[pltpu.VMEM((seq, cb), F32), pltpu.VMEM((seq, cb), F32),
                        pltpu.VMEM((n, 2 * SPEC_PITCH, cb), F32), pltpu.VMEM((n, FFT_N, cb), F32)],
        compiler_params=_cparams(("parallel", "parallel")),
        name="hyena_conv",
    )(u_arr, x_arr, conv_w, conv_b, conv_w, conv_b, bias_row, spectra, fwd, inv)


def _hyena_mix(hy_in, conv_w, conv_b, bias, spectra, fwd, inv, row0, nbatch, seq, cb):
    ncb = bias.shape[1] // cb
    z1 = _hyena_order(hy_in, row0, 0, hy_in, row0, ncb, conv_w, conv_b, bias[0:1], spectra, 0,
                      fwd, inv, nbatch, seq, cb, True)
    return _hyena_order(z1, 0, 0, hy_in, row0, 2 * ncb, conv_w, conv_b, bias[1:2], spectra, ncb,
                        fwd, inv, nbatch, seq, cb, False)


def _ret_kernel(q_ref, k_ref, v_ref, g_ref, cos_ref, sin_ref, df_ref, db_ref, ng_ref, s0f_ref, s0b_ref,
                y_ref, sf_ref, sb_ref, q_scr, k_scr, sb_scr, *, n, rope, has_state):
    c = RET_CHUNK
    dk = cos_ref.shape[1]
    hps = df_ref.shape[0]
    ri = lax.broadcasted_iota(jnp.int32, (c, c), 0)
    ci = lax.broadcasted_iota(jnp.int32, (c, c), 1)
    diff = (ri - ci).astype(F32)
    pos = lax.broadcasted_iota(jnp.int32, (c, dk), 0).astype(F32)

    def decay_terms(hh):
        lgf = -jnp.exp(df_ref[hh])
        lgb = -jnp.exp(db_ref[hh])
        mask = (jnp.where(diff >= 0, jnp.exp(jnp.maximum(diff, 0.0) * lgf), 0.0)
                + jnp.where(diff <= 0, jnp.exp(jnp.maximum(-diff, 0.0) * lgb), 0.0))
        return dict(mask=mask,
                    xi_f=jnp.exp((pos + 1.0) * lgf), ze_f=jnp.exp((c - 1.0 - pos) * lgf), cd_f=jnp.exp(c * lgf),
                    xi_b=jnp.exp((c - pos) * lgb), ze_b=jnp.exp(pos * lgb), cd_b=jnp.exp(c * lgb))

    dec = [decay_terms(hh) for hh in range(hps)]
    cols = lambda hh: slice(hh * dk, (hh + 1) * dk)

    for hh in range(hps):
        q = q_ref[:, cols(hh)]
        k = k_ref[:, cols(hh)]
        if rope:
            cs = cos_ref[...]
            sn = sin_ref[...]
            q = q * cs + pltpu.roll(q, dk // 2, 1) * sn
            k = k * cs + pltpu.roll(k, dk // 2, 1) * sn
        q_scr[:, cols(hh)] = q
        k_scr[:, cols(hh)] = k * (dk ** -0.5)
        sf_ref[hh] = s0f_ref[hh] if has_state else jnp.zeros((dk, dk), F32)
        sb_ref[hh] = s0b_ref[hh] if has_state else jnp.zeros((dk, dk), F32)

    def dot_tn(a, b):
        return lax.dot_general(a, b, (((0,), (0,)), ((), ())), preferred_element_type=F32)

    def chunk(i):
        return pl.ds(pl.multiple_of(i * c, c), c)

    def bwd_body(j, carry):
        i = n - 1 - j
        rows = chunk(i)
        for hh in range(hps):
            s = sb_ref[hh]
            sb_scr[hh, i] = s
            kz = (k_scr[rows, cols(hh)] * dec[hh]["ze_b"]).astype(BF16)
            sb_ref[hh] = s * dec[hh]["cd_b"] + dot_tn(kz, v_ref[rows, cols(hh)].astype(BF16))
        return carry

    unroll = math.gcd(n, 8)
    lax.fori_loop(0, n, bwd_body, 0, unroll=unroll)

    def fwd_body(i, carry):
        rows = chunk(i)
        for hh in range(hps):
            d = dec[hh]
            s = sf_ref[hh]
            qb = q_scr[rows, cols(hh)].astype(BF16)
            kc = k_scr[rows, cols(hh)]
            vb = v_ref[rows, cols(hh)].astype(BF16)
            sc = lax.dot_general(qb, kc.astype(BF16), (((1,), (1,)), ((), ())), preferred_element_type=F32)
            o = _dot((sc * d["mask"]).astype(BF16), vb)
            o = o + _dot(qb, s.astype(BF16)) * d["xi_f"] + _dot(qb, sb_scr[hh, i].astype(BF16)) * d["xi_b"]
            gt = g_ref[rows, cols(hh)]
            y_ref[rows, cols(hh)] = _rms(o, ng_ref[hh]) * (gt * jax.nn.sigmoid(gt))
            sf_ref[hh] = s * d["cd_f"] + dot_tn((kc * d["ze_f"]).astype(BF16), vb)
        return carry

    lax.fori_loop(0, n, fwd_body, 0, unroll=unroll)


def _retention(q, k, v, g, cos, sin, dec_f, dec_b, norm_g, s0f, s0b, nbatch, seq, rope, has_state, hps):
    nh = N_RET_HEADS
    dk = q.shape[1] // nh
    n = seq // RET_CHUNK
    tspec = pl.BlockSpec((seq, hps * dk), lambda b, h: (b, h))
    rspec = pl.BlockSpec((seq, dk), lambda b, h: (0, 0))
    hspec = pl.BlockSpec((hps, 1, dk), lambda b, h: (h, 0, 0))
    sspec = pl.BlockSpec((None, None, hps, dk, dk), lambda b, h: (b, 0, h, 0, 0))
    s0spec = sspec if has_state else pl.BlockSpec((None, None, hps, dk, dk), lambda b, h: (0, 0, h, 0, 0))
    st_shape = jax.ShapeDtypeStruct((nbatch, 1, nh, dk, dk), F32)
    return pl.pallas_call(
        functools.partial(_ret_kernel, n=n, rope=rope, has_state=has_state),
        out_shape=[jax.ShapeDtypeStruct((nbatch * seq, nh * dk), F32), st_shape, st_shape],
        grid=(nbatch, nh // hps),
        in_specs=[tspec, tspec, tspec, tspec, rspec, rspec, hspec, hspec, hspec, s0spec, s0spec],
        out_specs=[tspec, sspec, sspec],
        scratch_shapes=[pltpu.VMEM((seq, hps * dk), F32), pltpu.VMEM((seq, hps * dk), F32),
                        pltpu.VMEM((hps, n, dk, dk), F32)],
        compiler_params=_cparams(("parallel", "parallel")),
        name="retention",
    )(q, k, v, g, cos, sin, dec_f, dec_b, norm_g, s0f, s0b)


def _outproj_kernel(yh_ref, yr_ref, x_ref, gt_ref, sh_ref, sc_ref, hg_ref, ng_ref, wo_ref, x1_ref, h2t_ref):
    ncb = yh_ref.shape[0]
    yh = yh_ref[0] if ncb == 1 else jnp.concatenate([yh_ref[c] for c in range(ncb)], axis=1)
    d_hy = yh.shape[1]
    nh = _rms(yh, hg_ref[...]).astype(BF16)
    y = _dot(nh, wo_ref[0:d_hy, :]) + _dot(yr_ref[...].astype(BF16), wo_ref[d_hy:, :])
    x1 = x_ref[...] + gt_ref[...] * y
    x1_ref[...] = x1
    h2 = _rms(x1, ng_ref[...]) * (1.0 + sc_ref[...]) + sh_ref[...]
    h2t_ref[...] = h2.T.astype(BF16)


def _outproj(y_hy, y_ret, x, gt1, sh2, sc2, hy_g, n2_g, wo_bf, grp, tm):
    t, d = x.shape
    row = lambda i: (i, 0)
    modspec = pl.BlockSpec((None, 1, d), lambda i: (grp(i), 0, 0))
    return pl.pallas_call(
        _outproj_kernel,
        out_shape=[jax.ShapeDtypeStruct((t, d), F32), jax.ShapeDtypeStruct((d, t), BF16)],
        grid=(t // tm,),
        in_specs=[pl.BlockSpec((y_hy.shape[0], tm, y_hy.shape[2]), lambda i: (0, i, 0)),
                  pl.BlockSpec((tm, y_ret.shape[1]), row),
                  pl.BlockSpec((tm, d), row), modspec, modspec, modspec,
                  pl.BlockSpec(hy_g.shape, lambda i: (0, 0)),
                  pl.BlockSpec((1, d), lambda i: (0, 0)),
                  pl.BlockSpec(wo_bf.shape, lambda i: (0, 0))],
        out_specs=[pl.BlockSpec((tm, d), row), pl.BlockSpec((d, tm), lambda i: (0, i))],
        compiler_params=_cparams(("parallel",)),
        name="outproj",
    )(y_hy, y_ret, x, gt1, sh2, sc2, hy_g, n2_g, wo_bf)


def _sort_desc(xs):
    xs = list(xs)
    n = len(xs)
    k = 2
    while k <= n:
        j = k // 2
        while j >= 1:
            for i in range(n):
                l = i ^ j
                if l > i:
                    hi, lo = jnp.maximum(xs[i], xs[l]), jnp.minimum(xs[i], xs[l])
                    xs[i], xs[l] = (hi, lo) if (i & k) == 0 else (lo, hi)
            j //= 2
        k *= 2
    return xs


def _merge_top(a, b):
    n = len(a)
    xs = [jnp.maximum(a[i], b[n - 1 - i]) for i in range(n)]
    j = n // 2
    while j >= 1:
        for i in range(n):
            l = i ^ j
            if l > i:
                xs[i], xs[l] = jnp.maximum(xs[i], xs[l]), jnp.minimum(xs[i], xs[l])
        j //= 2
    return xs


def _top_sorted(load, count, k):
    acc = None
    for g0 in range(0, count, k):
        grp = _sort_desc([load(i) for i in range(g0, g0 + k)])
        acc = grp if acc is None else _merge_top(acc, grp)
    return acc


def _staircase(k):
    return [(a, b) for a in range(k) for b in range(k) if (a + 1) * (b + 1) <= k + 1]


def _peer1_kernel(h2t_ref, wq_ref, keys_ref, r2_ref, e2_ref, cnt_ref, e1_ref, q_scr, s1_scr, s2_scr):
    h = pl.program_id(1)
    lt_n = r2_ref.shape[0]
    nk = N_KEYS
    kk = PEER_TOPK

    @pl.when(h == 0)
    def _():
        q_scr[...] = _dot(wq_ref[...], h2t_ref[...]).astype(BF16)

    dq = q_scr.shape[0] // (2 * PEER_HEADS)
    for p, scr in ((0, s1_scr), (1, s2_scr)):
        rows = pl.ds(pl.multiple_of((2 * h + p) * dq, dq), dq)
        s = _dot(keys_ref[2 * h + p], q_scr[rows, :])
        for lt in range(lt_n):
            scr[lt * TILE_PITCH:lt * TILE_PITCH + nk, :] = s[:, lt * LANES:(lt + 1) * LANES]

    ld1 = lambda i: s1_scr[pl.ds(i, lt_n, stride=TILE_PITCH), :]
    ld2 = lambda i: s2_scr[pl.ds(i, lt_n, stride=TILE_PITCH), :]
    v1 = _top_sorted(ld1, nk, kk)
    v2 = _top_sorted(ld2, nk, kk)
    pairs = _staircase(kk)
    cand = [v1[a] + v2[b] for a, b in pairs]
    neg = jnp.full_like(cand[0], -jnp.inf)
    padded = cand + [neg] * (-len(cand) % kk)
    top = _top_sorted(lambda i: padded[i], len(padded), kk)
    tau = top[kk - 1]
    nxt = neg
    for cnd in cand:
        nxt = jnp.maximum(nxt, jnp.where(cnd < tau, cnd, neg))
    thr = 0.5 * (tau + nxt)
    m = v1[0] + v2[0]
    z = jnp.zeros_like(tau)
    for cnd in cand:
        z = z + jnp.where(cnd >= tau, jnp.exp(cnd - m), 0.0)
    inv_z = 1.0 / z

    for i in range(nk):
        x1 = ld1(i)
        gap = thr - x1
        cnt = jnp.zeros_like(x1)
        for b in range(kk):
            cnt = cnt + jnp.where(v2[b] > gap, 1.0, 0.0)
        keep = x1 >= v1[kk - 1]
        cnt_ref[i] = jnp.where(keep, cnt, 0.0)
        e1_ref[i] = jnp.where(keep, jnp.exp(x1 - v1[0]) * inv_z, 0.0)
    for lt in range(lt_n):
        s2 = s2_scr[lt * TILE_PITCH:lt * TILE_PITCH + nk, :]
        rank = jnp.zeros_like(s2)
        for b in range(kk):
            rank = rank + jnp.where(v2[b][lt:lt + 1, :] > s2, 1.0, 0.0)
        r2_ref[lt] = rank.astype(BF16)
        e2_ref[lt] = jnp.where(s2 >= v2[kk - 1][lt:lt + 1, :], jnp.exp(s2 - v2[0][lt:lt + 1, :]), 0.0).astype(BF16)


def _peer1(h2t, wqt_bf, keys_bf):
    d, t = h2t.shape
    nh = PEER_HEADS
    tb = TOK_VREG
    lt_n = tb // LANES
    tile_shape = jax.ShapeDtypeStruct((nh, t // LANES, N_KEYS, LANES), BF16)
    vreg_shape = jax.ShapeDtypeStruct((nh, t // tb, N_KEYS, lt_n, LANES), F32)
    tile_spec = pl.BlockSpec((None, lt_n, N_KEYS, LANES), lambda i, h: (h, i, 0, 0))
    vreg_spec = pl.BlockSpec((None, None, N_KEYS, lt_n, LANES), lambda i, h: (h, i, 0, 0, 0))
    return pl.pallas_call(
        _peer1_kernel,
        out_shape=[tile_shape, tile_shape, vreg_shape, vreg_shape],
        grid=(t // tb, nh),
        in_specs=[pl.BlockSpec((d, tb), lambda i, h: (0, i)),
                  pl.BlockSpec(wqt_bf.shape, lambda i, h: (0, 0)),
                  pl.BlockSpec(keys_bf.shape, lambda i, h: (0, 0, 0))],
        out_specs=[tile_spec, tile_spec, vreg_spec, vreg_spec],
        scratch_shapes=[pltpu.VMEM((wqt_bf.shape[0], tb), BF16),
                        pltpu.VMEM((lt_n * TILE_PITCH, LANES), F32),
                        pltpu.VMEM((lt_n * TILE_PITCH, LANES), F32)],
        compiler_params=_cparams(("parallel", "arbitrary")),
        name="peer_select",
    )(h2t, wqt_bf, keys_bf)


def _gelu_tanh(x):
    k = -2.0 * math.sqrt(2.0 / math.pi) * math.log2(math.e)
    e = jnp.exp2(x * (x * x * (k * 0.044715) + k))
    return x / (1.0 + e)


GATE_ROWS = 2 * SUBLANES
SEL_PITCH = N_KEYS + SUBLANES


def _peer2_kernel(h2t_in, u_ref, vt_ref, r2_in, e2_in, cnt_ref, e1_ref, x1_ref, gt_ref, fg_ref, y_ref,
                  at_scr, zt_scr, sel_ref, h2t_ref, ot_ref):
    eb = pl.program_id(1)
    lt_n = r2_in.shape[1]
    ig_n = cnt_ref.shape[1]

    @pl.when(eb == 0)
    def _():
        ot_ref[...] = jnp.zeros_like(ot_ref)
        for hh in range(PEER_HEADS):
            sel_ref[:, 2 * hh, 0:N_KEYS, :] = r2_in[hh].astype(F32)
            sel_ref[:, 2 * hh + 1, 0:N_KEYS, :] = e2_in[hh].astype(F32)
        h2t_ref[...] = h2t_in[...]

    tcw = at_scr.shape[2]
    ntc = lt_n * LANES // tcw
    lt_per = tcw // LANES

    def row(ref, hh, ig, lt):
        return jnp.broadcast_to(ref[hh, ig, lt:lt + 1, :], (GATE_ROWS, LANES))

    def scores(tc):
        at_scr[tc % 2] = _dot(u_ref[...], h2t_ref[:, tc * tcw:(tc + 1) * tcw])

    def gates(tc):
        for ig in range(ig_n):
            for l in range(lt_per):
                lt = tc * lt_per + l
                cnt = [row(cnt_ref, hh, ig, lt) for hh in range(PEER_HEADS)]
                e1 = [row(e1_ref, hh, ig, lt) for hh in range(PEER_HEADS)]
                for c in range(N_KEYS // GATE_ROWS):
                    js = slice(c * GATE_ROWS, (c + 1) * GATE_ROWS)
                    w = jnp.zeros((GATE_ROWS, LANES), F32)
                    for hh in range(PEER_HEADS):
                        r2 = sel_ref[lt, 2 * hh, js, :]
                        e2 = sel_ref[lt, 2 * hh + 1, js, :]
                        w = w + jnp.where(r2 < cnt[hh], e2, 0.0) * e1[hh]
                    rows = slice(ig * N_KEYS + c * GATE_ROWS, ig * N_KEYS + (c + 1) * GATE_ROWS)
                    a = at_scr[tc % 2, rows, l * LANES:(l + 1) * LANES]
                    zt_scr[rows, lt * LANES:(lt + 1) * LANES] = (w * _gelu_tanh(a)).astype(BF16)

    def combine(tc):
        cols = slice(tc * tcw, (tc + 1) * tcw)
        ot_ref[:, cols] += _dot(vt_ref[...], zt_scr[:, cols])

    scores(0)
    for tc in range(ntc):
        if tc + 1 < ntc:
            scores(tc + 1)
        gates(tc)
        combine(tc)

    @pl.when(eb == pl.num_programs(1) - 1)
    def _():
        y_ref[...] = _rms(x1_ref[...] + gt_ref[...] * ot_ref[...].T, fg_ref[...])


def _peer2(h2t, u_bf, vt_bf, r2, e2, cnt, e1, x1, gt2, final_g, grp, eblk):
    d, t = h2t.shape
    ne = u_bf.shape[0]
    nh = PEER_HEADS
    tb = TOK_VREG
    lt_n = tb // LANES
    ig_n = eblk // N_KEYS
    tile_spec = pl.BlockSpec((nh, lt_n, N_KEYS, LANES), lambda i, e: (0, i, 0, 0))
    vreg_spec = pl.BlockSpec((nh, None, ig_n, lt_n, LANES), lambda i, e: (0, i, e, 0, 0))
    return pl.pallas_call(
        _peer2_kernel,
        out_shape=jax.ShapeDtypeStruct((t, d), F32),
        grid=(t // tb, ne // eblk),
        in_specs=[pl.BlockSpec((d, tb), lambda i, e: (0, i)),
                  pl.BlockSpec((eblk, d), lambda i, e: (e, 0)),
                  pl.BlockSpec((d, eblk), lambda i, e: (0, e)),
                  tile_spec, tile_spec, vreg_spec, vreg_spec,
                  pl.BlockSpec((tb, d), lambda i, e: (i, 0), pipeline_mode=pl.Buffered(1)),
                  pl.BlockSpec((None, 1, d), lambda i, e: (grp(i), 0, 0)),
                  pl.BlockSpec((1, d), lambda i, e: (0, 0))],
        out_specs=pl.BlockSpec((tb, d), lambda i, e: (i, 0)),
        scratch_shapes=[pltpu.VMEM((2, eblk, 2 * LANES), F32), pltpu.VMEM((eblk, tb), BF16),
                        pltpu.VMEM((lt_n, 2 * nh, SEL_PITCH, LANES), F32), pltpu.VMEM((d, tb), BF16),
                        pltpu.VMEM((d, tb), F32)],
        compiler_params=_cparams(("parallel", "arbitrary")),
        name="peer_experts",
    )(h2t, u_bf, vt_bf, r2, e2, cnt, e1, x1, gt2, final_g)


def _rope_tables(seq, dk):
    rows = seq // GRID_W
    r, c = jnp.meshgrid(jnp.arange(rows, dtype=F32), jnp.arange(GRID_W, dtype=F32), indexing='ij')
    r = r.reshape(-1)
    c = c.reshape(-1)
    nf = dk // 4
    inv = ROPE_BASE ** (-jnp.arange(nf, dtype=F32) / nf)
    ang = jnp.concatenate([r[:, None] * inv, c[:, None] * inv], axis=-1)
    cos = jnp.concatenate([jnp.cos(ang), jnp.cos(ang)], axis=-1)
    sin = jnp.concatenate([-jnp.sin(ang), jnp.sin(ang)], axis=-1)
    return cos, sin


def _pad2(a, rows, cols):
    return jnp.pad(a, ((0, rows - a.shape[0]), (0, cols - a.shape[1])))


def kernel(x_prompt, x_sample, state_ret_fwd, state_ret_bwd, c, c_ctx, w_ada, b_ada, norm1_g, w_in, hy_conv_w, hy_conv_b, hy_w1, hy_b1, hy_w2, hy_b2, hy_w3, hy_b3, hy_w4, hy_freq, hy_bias, hy_norm_g, ret_decay_fwd, ret_decay_bwd, ret_norm_g, w_out, norm2_g, peer_wq, peer_keys, peer_u, peer_v, final_g):
    b_ctx, l_ctx, d = x_prompt.shape
    b_lat, l_lat, _ = x_sample.shape
    depth = w_ada.shape[0]
    assert depth == 1
    t_ctx, t_lat = b_ctx * l_ctx, b_lat * l_lat
    d_hy = hy_norm_g.shape[1]
    d_ret = ret_norm_g.shape[1]
    dk = d_ret // N_RET_HEADS
    tm = 512

    ngrp = 1 + b_lat
    cond = jnp.concatenate([c_ctx[None, :], c], axis=0)
    cond8 = jnp.pad(cond, ((0, SUBLANES - ngrp), (0, 0)))
    mod = _ada(cond8, w_ada[0], b_ada[0])
    sh1, sc1, gt1, sh2, sc2, gt2 = [m.reshape(SUBLANES, 1, d) for m in jnp.split(mod, 6, axis=-1)]

    fw = hy_w1.shape[-1]
    f = jnp.linspace(1e-4, HY_BANDS - 1, HY_BANDS, dtype=F32)
    frow = jnp.zeros((1, LANES), F32).at[0, 1:1 + HY_BANDS].set(f).at[0, 1 + HY_BANDS:1 + 2 * HY_BANDS].set(f)
    min_decay = math.log(HY_DECAY_TARGET) / HY_SLOW_PCT
    max_decay = math.log(HY_DECAY_TARGET) / HY_FAST_PCT
    deltas = jnp.abs(jnp.linspace(min_decay, max_decay, d_hy, dtype=F32))
    w4 = hy_w4[0].reshape(fw, 2, 2 * d_hy).transpose(1, 0, 2)
    w4 = jnp.pad(w4, ((0, 0), (0, LANES - fw), (0, 0)))
    filt_w = (frow, _pad2(hy_w1[0], LANES, LANES), _pad2(hy_b1, 1, LANES), _pad2(hy_w2[0], LANES, LANES),
              _pad2(hy_b2, 1, LANES), _pad2(hy_w3[0], LANES, LANES), _pad2(hy_b3, 1, LANES),
              _pad2(hy_freq, 1, LANES), w4, jnp.tile(deltas, 2)[None, :])
    fwd_m, inv_m, flt_m = _dft_mats()
    cos_t, sin_t = _rope_tables(l_lat, dk)
    dec_f = jnp.broadcast_to(ret_decay_fwd[0][:, None, None], (N_RET_HEADS, 1, dk))
    dec_b = jnp.broadcast_to(ret_decay_bwd[0][:, None, None], (N_RET_HEADS, 1, dk))
    ng = ret_norm_g[0].reshape(N_RET_HEADS, 1, dk)
    w_in_bf = w_in[0].astype(BF16)
    w_out_bf = w_out[0].astype(BF16)
    wqt = peer_wq[0].T.astype(BF16)
    keys = peer_keys[0].reshape(2 * PEER_HEADS, N_KEYS, -1).astype(BF16)
    u_bf = peer_u[0].astype(BF16)
    vt_bf = peer_v[0].T.astype(BF16)
    fg = final_g.reshape(1, d)

    def trunk(x, nbatch, seq, latent, hy_cb, ret_hps):
        grp = (lambda i: 1 + i // (seq // tm)) if latent else (lambda i: 0)
        hy_in, rq, rk, rv, rg = _inproj(x, sh1, sc1, norm1_g, w_in_bf, grp, tm, 3 * d_hy, d_ret, hy_cb)
        spectra = _hyena_spectra(seq, filt_w, flt_m, hy_cb)
        y_hy = _hyena_mix(hy_in, hy_conv_w[0], hy_conv_b, hy_bias[0], spectra, fwd_m, inv_m, 0, nbatch, seq, hy_cb)
        y_ret, s_f, s_b = _retention(rq, rk, rv, rg, cos_t, sin_t, dec_f, dec_b, ng, state_ret_fwd, state_ret_bwd,
                                     nbatch, seq, latent, latent, ret_hps)
        x1, h2t = _outproj(y_hy, y_ret, x, gt1, sh2, sc2, hy_norm_g, norm2_g, w_out_bf, grp, tm)
        r2, e2, cnt, e1 = _peer1(h2t, wqt, keys)
        grp_tok = (lambda i: 1 + i // (seq // TOK_VREG)) if latent else (lambda i: 0)
        y = _peer2(h2t, u_bf, vt_bf, r2, e2, cnt, e1, x1, gt2, fg, grp_tok, 1024)
        return y.reshape(nbatch, seq, d), s_f, s_b

    y_prompt, sf_new, sb_new = trunk(x_prompt.reshape(t_ctx, d), b_ctx, l_ctx, False, 512, N_RET_HEADS)
    y_sample, _, _ = trunk(x_sample.reshape(t_lat, d), b_lat, l_lat, True, 128, 1)
    return (y_prompt, y_sample, sf_new, sb_new)
```

```python
import functools
import math

import numpy as np
import jax
import jax.numpy as jnp
from jax import lax
from jax.experimental import pallas as pl
from jax.experimental.pallas import tpu as pltpu

F32 = jnp.float32
BF16 = jnp.bfloat16

LANES = 128
SUBLANES = 8
VMEM_LIMIT = 56 * 1024 * 1024

EPS = 1e-6
GRID_W = 64
N_RET_HEADS = 4
RET_CHUNK = 128
ROPE_BASE = 10000.0
HY_EMB = 33
HY_BANDS = (HY_EMB - 1) // 2
HY_DECAY_TARGET = 1e-2
HY_FAST_PCT = 0.3
HY_SLOW_PCT = 1.5
N_KEYS = 128
PEER_HEADS = 8
PEER_TOPK = 16
FFT_P = 256
FFT_N = 2 * FFT_P
SPEC_PITCH = FFT_P + SUBLANES
SPEC_HEAD = 2 * SUBLANES
SPEC_ROWS = 2 * SPEC_PITCH + SPEC_HEAD
TOK_VREG = SUBLANES * LANES
TILE_PITCH = N_KEYS + SUBLANES

TOKEN_BLOCK = 512
ADA_COL_BLOCK = 1536
PEER_EXPERT_BLOCK = 1024
PEER_TOKEN_CHUNK = 2 * LANES


def _path_blocks(seq, d_hy):
    return (d_hy, N_RET_HEADS) if seq == FFT_P else (LANES, 1)


def _cparams(sem):
    return pltpu.CompilerParams(dimension_semantics=sem, vmem_limit_bytes=VMEM_LIMIT)


def _dot(a, b):
    return jnp.dot(a, b, preferred_element_type=F32)


def _split(a):
    hi = a.astype(BF16)
    lo = (a - hi.astype(F32)).astype(BF16)
    return hi, lo


def _dot3(a, b):
    ah, al = _split(a)
    bh, bl = _split(b)
    return _dot(ah, bh) + (_dot(al, bh) + _dot(ah, bl))


def _rms(x, g):
    return x * lax.rsqrt(jnp.mean(x * x, axis=-1, keepdims=True) + EPS) * g


def _ada_kernel(c_ref, w_ref, b_ref, o_ref):
    c = c_ref[...]
    s = c * jax.nn.sigmoid(c)
    o_ref[...] = _dot3(s, w_ref[...]) + b_ref[...]


def _ada(cond8, w_ada, b_ada):
    d, n = w_ada.shape
    tn = ADA_COL_BLOCK
    return pl.pallas_call(
        _ada_kernel,
        out_shape=jax.ShapeDtypeStruct((cond8.shape[0], n), F32),
        grid=(n // tn,),
        in_specs=[pl.BlockSpec((cond8.shape[0], d), lambda j: (0, 0)),
                  pl.BlockSpec((d, tn), lambda j: (0, j)),
                  pl.BlockSpec((1, tn), lambda j: (0, j))],
        out_specs=pl.BlockSpec((cond8.shape[0], tn), lambda j: (0, j)),
        compiler_params=_cparams(("parallel",)),
        name="ada",
    )(cond8, w_ada, b_ada.reshape(1, n))


def _inproj_kernel(x_ref, sh_ref, sc_ref, g_ref, w_ref, hy_ref, q_ref, k_ref, v_ref, gg_ref, *, d_hy3, d_ret):
    h = _rms(x_ref[...], g_ref[...]) * (1.0 + sc_ref[...]) + sh_ref[...]
    hb = h.astype(BF16)
    hy = _dot(hb, w_ref[:, 0:d_hy3])
    cb = hy_ref.shape[2]
    for c in range(hy_ref.shape[0]):
        hy_ref[c] = hy[:, c * cb:(c + 1) * cb]
    for i, o_ref in enumerate((q_ref, k_ref, v_ref, gg_ref)):
        o_ref[...] = _dot(hb, w_ref[:, d_hy3 + i * d_ret:d_hy3 + (i + 1) * d_ret])


def _inproj(x, sh, sc, g, w_bf, grp, tm, d_hy3, d_ret, hy_cb):
    t, d = x.shape
    row = lambda i: (i, 0)
    modspec = pl.BlockSpec((None, 1, d), lambda i: (grp(i), 0, 0))
    outs = [jax.ShapeDtypeStruct((d_hy3 // hy_cb, t, hy_cb), F32)] + [jax.ShapeDtypeStruct((t, d_ret), F32)] * 4
    return pl.pallas_call(
        functools.partial(_inproj_kernel, d_hy3=d_hy3, d_ret=d_ret),
        out_shape=outs,
        grid=(t // tm,),
        in_specs=[pl.BlockSpec((tm, d), row), modspec, modspec,
                  pl.BlockSpec((1, d), lambda i: (0, 0)),
                  pl.BlockSpec(w_bf.shape, lambda i: (0, 0))],
        out_specs=[pl.BlockSpec((d_hy3 // hy_cb, tm, hy_cb), lambda i: (0, i, 0))]
        + [pl.BlockSpec((tm, d_ret), row)] * 4,
        compiler_params=_cparams(("parallel",)),
        name="inproj",
    )(x, sh, sc, g, w_bf)


def _kf_kernel(frow_ref, w1_ref, b1_ref, w2_ref, b2_ref, w3_ref, b3_ref, fr_ref, w4_ref, dl_ref, o_ref, *, seq):
    r = pl.program_id(0)
    rows = o_ref.shape[0]
    i = r * rows + lax.broadcasted_iota(jnp.int32, (rows, LANES), 0)
    pos = jnp.abs(i - seq).astype(F32)
    t = pos / (seq - 1.0)
    ang = frow_ref[...] * ((2.0 * math.pi) * pos / seq)
    lane = lax.broadcasted_iota(jnp.int32, (rows, LANES), 1)
    z = jnp.where(lane == 0, t,
                  jnp.where(lane <= HY_BANDS, jnp.cos(ang),
                            jnp.where(lane <= 2 * HY_BANDS, -jnp.sin(ang), 0.0)))
    fr = fr_ref[...]
    h = jnp.sin(fr * (_dot3(z, w1_ref[...]) + b1_ref[...]))
    h = jnp.sin(fr * (_dot3(h, w2_ref[...]) + b2_ref[...]))
    h = jnp.sin(fr * (_dot3(h, w3_ref[...]) + b3_ref[...]))
    h4 = _dot3(h, w4_ref[...])
    nc = o_ref.shape[1]
    iw = r * rows + lax.broadcasted_iota(jnp.int32, (rows, nc), 0)
    tw = jnp.abs(iw - seq).astype(F32) / (seq - 1.0)
    win = jnp.exp(-tw * dl_ref[...])
    o_ref[...] = jnp.where(iw == 0, 0.0, h4 * win)


def _spec_kernel(k0_ref, k1_ref, a_ref, o_ref):
    a = a_ref[...]
    s = (_dot(a[:, :FFT_P], k0_ref[...].astype(BF16))
         + _dot(a[:, FFT_P:], k1_ref[...].astype(BF16)))
    cb = o_ref.shape[2]
    for c in range(o_ref.shape[0]):
        cols = slice(c * cb, (c + 1) * cb)
        for p in range(2):
            r0 = p * SPEC_PITCH
            o_ref[c, r0:r0 + FFT_P, :] = s[p * FFT_P:(p + 1) * FFT_P, cols]
            o_ref[c, r0 + FFT_P:r0 + SPEC_PITCH, :] = jnp.zeros((SPEC_PITCH - FFT_P, cb), F32)
        o_ref[c, 2 * SPEC_PITCH:SPEC_ROWS, :] = s[2 * FFT_P:2 * FFT_P + SPEC_HEAD, cols]


def _dft_mats():
    f = np.arange(FFT_P)[:, None].astype(np.float64)
    t = np.arange(FFT_P)[None, :].astype(np.float64)
    j = np.arange(FFT_N)[None, :].astype(np.float64)
    w = 2.0 * np.pi / FFT_N
    fwd = np.concatenate([np.cos(w * f * t), -np.sin(w * f * t)], axis=0)
    fwd[FFT_P] = np.cos(np.pi * t[0])
    scale = np.full((1, FFT_P), 2.0 / FFT_N)
    scale[0, 0] = 1.0 / FFT_N
    inv = np.concatenate([np.cos(w * t.T * f.T) * scale, -np.sin(w * t.T * f.T) * scale], axis=1)
    inv[:, FFT_P] = np.cos(np.pi * t[0]) / FFT_N
    sgn = np.where(np.arange(FFT_P) % 2 == 0, 1.0, -1.0)[:, None]
    re = sgn * np.cos(w * f * j)
    im = -sgn * np.sin(w * f * j)
    nyq = np.cos(np.pi * j[0])
    re2 = re.copy()
    re2[0] = nyq
    im[0] = 0.0
    flt = np.concatenate([re, im, re2[:SPEC_HEAD]], axis=0)
    flt[:, 0] = 0.0
    return (jnp.asarray(fwd, F32).astype(BF16), jnp.asarray(inv, F32).astype(BF16),
            jnp.asarray(flt, F32).astype(BF16))


def _hyena_spectra(seq, fw, flt_mat, cb):
    frow, w1, b1, w2, b2, w3, b3, fr, w4, dl = fw
    nblk = 2 * seq // FFT_P
    nc = w4.shape[-1]
    full = lambda a: pl.BlockSpec(a.shape, lambda r: (0,) * a.ndim)
    kf = pl.pallas_call(
        functools.partial(_kf_kernel, seq=seq),
        out_shape=jax.ShapeDtypeStruct((2 * seq, nc), F32),
        grid=(nblk,),
        in_specs=[full(frow), full(w1), full(b1), full(w2), full(b2), full(w3), full(b3), full(fr),
                  pl.BlockSpec((None,) + w4.shape[1:], lambda r: (jnp.where(r >= nblk // 2, 0, 1), 0, 0)),
                  full(dl)],
        out_specs=pl.BlockSpec((FFT_P, nc), lambda r: (r, 0)),
        compiler_params=_cparams(("parallel",)),
        name="hyena_filter",
    )(frow, w1, b1, w2, b2, w3, b3, fr, w4, dl)
    return pl.pallas_call(
        _spec_kernel,
        out_shape=jax.ShapeDtypeStruct((nc // cb, nblk - 1, SPEC_ROWS, cb), F32),
        grid=(nblk - 1,),
        in_specs=[pl.BlockSpec((FFT_P, nc), lambda w: (w, 0)),
                  pl.BlockSpec((FFT_P, nc), lambda w: (w + 1, 0)),
                  pl.BlockSpec(flt_mat.shape, lambda w: (0, 0))],
        out_specs=pl.BlockSpec((nc // cb, None, SPEC_ROWS, cb), lambda w: (0, w, 0, 0)),
        compiler_params=_cparams(("parallel",)),
        name="hyena_spectra",
    )(kf, kf, flt_mat)


def _hy_kernel(u_ref, x_ref, cwu_ref, cbu_ref, cwx_ref, cbx_ref, bias_ref, g_ref, fwd_ref, inv_ref,
               o_ref, u_scr, gate_scr, uf_scr, y_scr, *, n, conv_u):
    seq = n * FFT_P
    cb = o_ref.shape[1]

    def sconv(v, w_ref, b_ref):
        row = lax.broadcasted_iota(jnp.int32, v.shape, 0)
        prev = jnp.where(row == 0, 0.0, pltpu.roll(v, 1, 0))
        nxt = jnp.where(row == seq - 1, 0.0, pltpu.roll(v, seq - 1, 0))
        return b_ref[...] + prev * w_ref[0:1, :] + v * w_ref[1:2, :] + nxt * w_ref[2:3, :]

    u_scr[...] = sconv(u_ref[...], cwu_ref, cbu_ref) if conv_u else u_ref[...]
    gate_scr[...] = sconv(x_ref[...], cwx_ref, cbx_ref)
    fwd = fwd_ref[...]
    for b in range(n):
        uf = _dot(fwd, u_scr[b * FFT_P:(b + 1) * FFT_P, :].astype(BF16))
        uf_scr[b, 0:FFT_P, :] = uf[:FFT_P]
        uf_scr[b, SPEC_PITCH:SPEC_PITCH + FFT_P, :] = uf[FFT_P:]

    rc = SPEC_HEAD if n == 1 else SUBLANES

    def freq_rows(r0, first):
        ure = [uf_scr[b, pl.ds(r0, rc), :] for b in range(n)]
        uim = [uf_scr[b, pl.ds(SPEC_PITCH + r0, rc), :] for b in range(n)]
        for a in range(n):
            yre = yim = None
            for b in range(n):
                w = a - b + (n - 1)
                gre = g_ref[w, pl.ds(r0, rc), :]
                gim = g_ref[w, pl.ds(SPEC_PITCH + r0, rc), :]
                gre2 = g_ref[w, 2 * SPEC_PITCH:2 * SPEC_PITCH + rc, :] if first else gre
                tre = gre * ure[b] - gim * uim[b]
                tim = gre2 * uim[b] + gim * ure[b]
                yre = tre if yre is None else yre + tre
                yim = tim if yim is None else yim + tim
            y_scr[a, pl.ds(r0, rc), :] = yre
            y_scr[a, pl.ds(FFT_P + r0, rc), :] = yim

    freq_rows(0, True)

    def rows_body(i, carry):
        freq_rows(pl.multiple_of(i * rc, rc), False)
        return carry

    lax.fori_loop(1, FFT_P // rc, rows_body, 0)

    def out_body(a, carry):
        y = _dot(inv_ref[...], y_scr[a].astype(BF16))
        rows = pl.ds(pl.multiple_of(a * FFT_P, FFT_P), FFT_P)
        ua = u_scr[rows, :]
        o_ref[rows, :] = gate_scr[rows, :] * (y + ua * bias_ref[...])
        return carry

    lax.fori_loop(0, n, out_body, 0)


def _hyena_order(u_arr, u_row0, u_col0, x_arr, x_row0, x_col0, conv_w, conv_b, bias_row, spectra, spec_col0,
                 fwd, inv, nbatch, seq, cb, conv_u):
    n = seq // FFT_P
    d_hy = bias_row.shape[1]
    ncb = d_hy // cb
    nwin = spectra.shape[1]
    grid = (ncb, nbatch)
    in_specs = [
        pl.BlockSpec((None, seq, cb), lambda c, b: (u_col0 + c, u_row0 + b, 0)),
        pl.BlockSpec((None, seq, cb), lambda c, b: (x_col0 + c, x_row0 + b, 0)),
        pl.BlockSpec((3, cb), lambda c, b: (0, u_col0 + c)),
        pl.BlockSpec((1, cb), lambda c, b: (0, u_col0 + c)),
        pl.BlockSpec((3, cb), lambda c, b: (0, x_col0 + c)),
        pl.BlockSpec((1, cb), lambda c, b: (0, x_col0 + c)),
        pl.BlockSpec((1, cb), lambda c, b: (0, c)),
        pl.BlockSpec((None, nwin, SPEC_ROWS, cb), lambda c, b: (spec_col0 + c, 0, 0, 0)),
        pl.BlockSpec(fwd.shape, lambda c, b: (0, 0)),
        pl.BlockSpec(inv.shape, lambda c, b: (0, 0)),
    ]
    return pl.pallas_call(
        functools.partial(_hy_kernel, n=n, conv_u=conv_u),
        out_shape=jax.ShapeDtypeStruct((ncb, nbatch * seq, cb), F32),
        grid=grid,
        in_specs=in_specs,
        out_specs=pl.BlockSpec((None, seq, cb), lambda c, b: (c, b, 0)),
        scratch_shapes=[pltpu.VMEM((seq, cb), F32), pltpu.VMEM((seq, cb), F32),
                        pltpu.VMEM((n, 2 * SPEC_PITCH, cb), F32), pltpu.VMEM((n, FFT_N, cb), F32)],
        compiler_params=_cparams(("parallel", "parallel")),
        name="hyena_conv",
    )(u_arr, x_arr, conv_w, conv_b, conv_w, conv_b, bias_row, spectra, fwd, inv)


def _hyena_mix(hy_in, conv_w, conv_b, bias, spectra, fwd, inv, row0, nbatch, seq, cb):
    ncb = bias.shape[1] // cb
    z1 = _hyena_order(hy_in, row0, 0, hy_in, row0, ncb, conv_w, conv_b, bias[0:1], spectra, 0,
                      fwd, inv, nbatch, seq, cb, True)
    return _hyena_order(z1, 0, 0, hy_in, row0, 2 * ncb, conv_w, conv_b, bias[1:2], spectra, ncb,
                        fwd, inv, nbatch, seq, cb, False)


def _ret_kernel(q_ref, k_ref, v_ref, g_ref, cos_ref, sin_ref, df_ref, db_ref, ng_ref, s0f_ref, s0b_ref,
                y_ref, sf_ref, sb_ref, q_scr, k_scr, sb_scr, *, n, rope, has_state):
    c = RET_CHUNK
    dk = cos_ref.shape[1]
    hps = df_ref.shape[0]
    ri = lax.broadcasted_iota(jnp.int32, (c, c), 0)
    ci = lax.broadcasted_iota(jnp.int32, (c, c), 1)
    diff = (ri - ci).astype(F32)
    pos = lax.broadcasted_iota(jnp.int32, (c, dk), 0).astype(F32)

    def decay_terms(hh):
        lgf = -jnp.exp(df_ref[hh])
        lgb = -jnp.exp(db_ref[hh])
        mask = (jnp.where(diff >= 0, jnp.exp(jnp.maximum(diff, 0.0) * lgf), 0.0)
                + jnp.where(diff <= 0, jnp.exp(jnp.maximum(-diff, 0.0) * lgb), 0.0))
        return dict(mask=mask,
                    xi_f=jnp.exp((pos + 1.0) * lgf), ze_f=jnp.exp((c - 1.0 - pos) * lgf), cd_f=jnp.exp(c * lgf),
                    xi_b=jnp.exp((c - pos) * lgb), ze_b=jnp.exp(pos * lgb), cd_b=jnp.exp(c * lgb))

    dec = [decay_terms(hh) for hh in range(hps)]
    cols = lambda hh: slice(hh * dk, (hh + 1) * dk)

    for hh in range(hps):
        q = q_ref[:, cols(hh)]
        k = k_ref[:, cols(hh)]
        if rope:
            cs = cos_ref[...]
            sn = sin_ref[...]
            q = q * cs + pltpu.roll(q, dk // 2, 1) * sn
            k = k * cs + pltpu.roll(k, dk // 2, 1) * sn
        q_scr[:, cols(hh)] = q
        k_scr[:, cols(hh)] = k * (dk ** -0.5)
        sf_ref[hh] = s0f_ref[hh] if has_state else jnp.zeros((dk, dk), F32)
        sb_ref[hh] = s0b_ref[hh] if has_state else jnp.zeros((dk, dk), F32)

    def dot_tn(a, b):
        return lax.dot_general(a, b, (((0,), (0,)), ((), ())), preferred_element_type=F32)

    def chunk(i):
        return pl.ds(pl.multiple_of(i * c, c), c)

    def bwd_body(j, carry):
        i = n - 1 - j
        rows = chunk(i)
        for hh in range(hps):
            s = sb_ref[hh]
            sb_scr[hh, i] = s
            kz = (k_scr[rows, cols(hh)] * dec[hh]["ze_b"]).astype(BF16)
            sb_ref[hh] = s * dec[hh]["cd_b"] + dot_tn(kz, v_ref[rows, cols(hh)].astype(BF16))
        return carry

    unroll = math.gcd(n, 8)
    lax.fori_loop(0, n, bwd_body, 0, unroll=unroll)

    def fwd_body(i, carry):
        rows = chunk(i)
        for hh in range(hps):
            d = dec[hh]
            s = sf_ref[hh]
            qb = q_scr[rows, cols(hh)].astype(BF16)
            kc = k_scr[rows, cols(hh)]
            vb = v_ref[rows, cols(hh)].astype(BF16)
            sc = lax.dot_general(qb, kc.astype(BF16), (((1,), (1,)), ((), ())), preferred_element_type=F32)
            o = _dot((sc * d["mask"]).astype(BF16), vb)
            o = o + _dot(qb, s.astype(BF16)) * d["xi_f"] + _dot(qb, sb_scr[hh, i].astype(BF16)) * d["xi_b"]
            gt = g_ref[rows, cols(hh)]
            y_ref[rows, cols(hh)] = _rms(o, ng_ref[hh]) * (gt * jax.nn.sigmoid(gt))
            sf_ref[hh] = s * d["cd_f"] + dot_tn((kc * d["ze_f"]).astype(BF16), vb)
        return carry

    lax.fori_loop(0, n, fwd_body, 0, unroll=unroll)


def _retention(q, k, v, g, cos, sin, dec_f, dec_b, norm_g, s0f, s0b, nbatch, seq, rope, has_state, hps):
    nh = N_RET_HEADS
    dk = q.shape[1] // nh
    n = seq // RET_CHUNK
    tspec = pl.BlockSpec((seq, hps * dk), lambda b, h: (b, h))
    rspec = pl.BlockSpec((seq, dk), lambda b, h: (0, 0))
    hspec = pl.BlockSpec((hps, 1, dk), lambda b, h: (h, 0, 0))
    sspec = pl.BlockSpec((None, None, hps, dk, dk), lambda b, h: (b, 0, h, 0, 0))
    s0spec = sspec if has_state else pl.BlockSpec((None, None, hps, dk, dk), lambda b, h: (0, 0, h, 0, 0))
    st_shape = jax.ShapeDtypeStruct((nbatch, 1, nh, dk, dk), F32)
    return pl.pallas_call(
        functools.partial(_ret_kernel, n=n, rope=rope, has_state=has_state),
        out_shape=[jax.ShapeDtypeStruct((nbatch * seq, nh * dk), F32), st_shape, st_shape],
        grid=(nbatch, nh // hps),
        in_specs=[tspec, tspec, tspec, tspec, rspec, rspec, hspec, hspec, hspec, s0spec, s0spec],
        out_specs=[tspec, sspec, sspec],
        scratch_shapes=[pltpu.VMEM((seq, hps * dk), F32), pltpu.VMEM((seq, hps * dk), F32),
                        pltpu.VMEM((hps, n, dk, dk), F32)],
        compiler_params=_cparams(("parallel", "parallel")),
        name="retention",
    )(q, k, v, g, cos, sin, dec_f, dec_b, norm_g, s0f, s0b)


def _outproj_kernel(yh_ref, yr_ref, x_ref, gt_ref, sh_ref, sc_ref, hg_ref, ng_ref, wo_ref, x1_ref, h2t_ref):
    ncb = yh_ref.shape[0]
    yh = yh_ref[0] if ncb == 1 else jnp.concatenate([yh_ref[c] for c in range(ncb)], axis=1)
    d_hy = yh.shape[1]
    nh = _rms(yh, hg_ref[...]).astype(BF16)
    y = _dot(nh, wo_ref[0:d_hy, :]) + _dot(yr_ref[...].astype(BF16), wo_ref[d_hy:, :])
    x1 = x_ref[...] + gt_ref[...] * y
    x1_ref[...] = x1
    h2 = _rms(x1, ng_ref[...]) * (1.0 + sc_ref[...]) + sh_ref[...]
    h2t_ref[...] = h2.T.astype(BF16)


def _outproj(y_hy, y_ret, x, gt1, sh2, sc2, hy_g, n2_g, wo_bf, grp, tm):
    t, d = x.shape
    row = lambda i: (i, 0)
    modspec = pl.BlockSpec((None, 1, d), lambda i: (grp(i), 0, 0))
    return pl.pallas_call(
        _outproj_kernel,
        out_shape=[jax.ShapeDtypeStruct((t, d), F32), jax.ShapeDtypeStruct((d, t), BF16)],
        grid=(t // tm,),
        in_specs=[pl.BlockSpec((y_hy.shape[0], tm, y_hy.shape[2]), lambda i: (0, i, 0)),
                  pl.BlockSpec((tm, y_ret.shape[1]), row),
                  pl.BlockSpec((tm, d), row), modspec, modspec, modspec,
                  pl.BlockSpec(hy_g.shape, lambda i: (0, 0)),
                  pl.BlockSpec((1, d), lambda i: (0, 0)),
                  pl.BlockSpec(wo_bf.shape, lambda i: (0, 0))],
        out_specs=[pl.BlockSpec((tm, d), row), pl.BlockSpec((d, tm), lambda i: (0, i))],
        compiler_params=_cparams(("parallel",)),
        name="outproj",
    )(y_hy, y_ret, x, gt1, sh2, sc2, hy_g, n2_g, wo_bf)


def _sort_desc(xs):
    xs = list(xs)
    n = len(xs)
    k = 2
    while k <= n:
        j = k // 2
        while j >= 1:
            for i in range(n):
                l = i ^ j
                if l > i:
                    hi, lo = jnp.maximum(xs[i], xs[l]), jnp.minimum(xs[i], xs[l])
                    xs[i], xs[l] = (hi, lo) if (i & k) == 0 else (lo, hi)
            j //= 2
        k *= 2
    return xs


def _merge_top(a, b):
    n = len(a)
    xs = [jnp.maximum(a[i], b[n - 1 - i]) for i in range(n)]
    j = n // 2
    while j >= 1:
        for i in range(n):
            l = i ^ j
            if l > i:
                xs[i], xs[l] = jnp.maximum(xs[i], xs[l]), jnp.minimum(xs[i], xs[l])
        j //= 2
    return xs


def _top_sorted(load, count, k):
    acc = None
    for g0 in range(0, count, k):
        grp = _sort_desc([load(i) for i in range(g0, g0 + k)])
        acc = grp if acc is None else _merge_top(acc, grp)
    return acc


def _staircase(k):
    return [(a, b) for a in range(k) for b in range(k) if (a + 1) * (b + 1) <= k + 1]


def _peer1_kernel(h2t_ref, wq_ref, keys_ref, s2_ref, e2_ref, low_ref, e1_ref, q_scr, s1_scr, s2_scr):
    h = pl.program_id(1)
    lt_n = s2_ref.shape[0]
    nk = N_KEYS
    kk = PEER_TOPK

    @pl.when(h == 0)
    def _():
        q_scr[...] = _dot(wq_ref[...], h2t_ref[...]).astype(BF16)

    dq = q_scr.shape[0] // (2 * PEER_HEADS)
    for p, scr in ((0, s1_scr), (1, s2_scr)):
        rows = pl.ds(pl.multiple_of((2 * h + p) * dq, dq), dq)
        s = _dot(keys_ref[2 * h + p], q_scr[rows, :])
        for lt in range(lt_n):
            scr[lt * TILE_PITCH:lt * TILE_PITCH + nk, :] = s[:, lt * LANES:(lt + 1) * LANES]

    ld1 = lambda i: s1_scr[pl.ds(i, lt_n, stride=TILE_PITCH), :]
    ld2 = lambda i: s2_scr[pl.ds(i, lt_n, stride=TILE_PITCH), :]
    v1 = _top_sorted(ld1, nk, kk)
    v2 = _top_sorted(ld2, nk, kk)
    pairs = _staircase(kk)
    cand = [v1[a] + v2[b] for a, b in pairs]
    neg = jnp.full_like(cand[0], -jnp.inf)
    padded = cand + [neg] * (-len(cand) % kk)
    top = _top_sorted(lambda i: padded[i], len(padded), kk)
    tau = top[kk - 1]
    nxt = neg
    for cnd in cand:
        nxt = jnp.maximum(nxt, jnp.where(cnd < tau, cnd, neg))
    thr = 0.5 * (tau + nxt)
    m = v1[0] + v2[0]
    z = jnp.zeros_like(tau)
    for cnd in cand:
        z = z + jnp.where(cnd >= tau, jnp.exp(cnd - m), 0.0)
    inv_z = 1.0 / z

    for i in range(nk):
        x1 = ld1(i)
        gap = thr - x1
        low = jnp.full_like(x1, jnp.inf)
        for b in range(kk):
            low = jnp.where(v2[b] > gap, v2[b], low)
        keep = x1 >= v1[kk - 1]
        low_ref[i] = jnp.where(keep, low, jnp.inf)
        e1_ref[i] = jnp.where(keep, jnp.exp(x1 - v1[0]) * inv_z, 0.0)
    for lt in range(lt_n):
        s2 = s2_scr[lt * TILE_PITCH:lt * TILE_PITCH + nk, :]
        s2_ref[lt] = s2
        e2_ref[lt] = jnp.where(s2 >= v2[kk - 1][lt:lt + 1, :], jnp.exp(s2 - v2[0][lt:lt + 1, :]), 0.0).astype(BF16)


def _peer1(h2t, wqt_bf, keys_bf):
    d, t = h2t.shape
    nh = PEER_HEADS
    tb = TOK_VREG
    lt_n = tb // LANES
    tile_shape = lambda dt: jax.ShapeDtypeStruct((nh, t // LANES, N_KEYS, LANES), dt)
    vreg_shape = jax.ShapeDtypeStruct((nh, t // tb, N_KEYS, lt_n, LANES), F32)
    tile_spec = pl.BlockSpec((None, lt_n, N_KEYS, LANES), lambda i, h: (h, i, 0, 0))
    vreg_spec = pl.BlockSpec((None, None, N_KEYS, lt_n, LANES), lambda i, h: (h, i, 0, 0, 0))
    return pl.pallas_call(
        _peer1_kernel,
        out_shape=[tile_shape(F32), tile_shape(BF16), vreg_shape, vreg_shape],
        grid=(t // tb, nh),
        in_specs=[pl.BlockSpec((d, tb), lambda i, h: (0, i)),
                  pl.BlockSpec(wqt_bf.shape, lambda i, h: (0, 0)),
                  pl.BlockSpec(keys_bf.shape, lambda i, h: (0, 0, 0))],
        out_specs=[tile_spec, tile_spec, vreg_spec, vreg_spec],
        scratch_shapes=[pltpu.VMEM((wqt_bf.shape[0], tb), BF16),
                        pltpu.VMEM((lt_n * TILE_PITCH, LANES), F32),
                        pltpu.VMEM((lt_n * TILE_PITCH, LANES), F32)],
        compiler_params=_cparams(("parallel", "arbitrary")),
        name="peer_select",
    )(h2t, wqt_bf, keys_bf)


def _gelu_tanh(x):
    k = -2.0 * math.sqrt(2.0 / math.pi) * math.log2(math.e)
    e = jnp.exp2(x * (x * x * (k * 0.044715) + k))
    return x / (1.0 + e)


GATE_ROWS = 2 * SUBLANES
SEL_PITCH = N_KEYS + SUBLANES


def _peer2_kernel(h2t_in, u_ref, vt_ref, s2_in, e2_in, low_ref, e1_ref, x1_ref, gt_ref, fg_ref, y_ref,
                  at_scr, zt_scr, sel_ref, h2t_ref, ot_ref):
    eb = pl.program_id(1)
    lt_n = s2_in.shape[1]
    ig_n = low_ref.shape[1]

    @pl.when(eb == 0)
    def _():
        ot_ref[...] = jnp.zeros_like(ot_ref)
        for hh in range(PEER_HEADS):
            sel_ref[:, 2 * hh, 0:N_KEYS, :] = s2_in[hh]
            sel_ref[:, 2 * hh + 1, 0:N_KEYS, :] = e2_in[hh].astype(F32)
        h2t_ref[...] = h2t_in[...]

    tcw = at_scr.shape[2]
    ntc = lt_n * LANES // tcw
    lt_per = tcw // LANES

    def row(ref, hh, ig, lt):
        return jnp.broadcast_to(ref[hh, ig, lt:lt + 1, :], (GATE_ROWS, LANES))

    def scores(tc):
        at_scr[tc % 2] = _dot(u_ref[...], h2t_ref[:, tc * tcw:(tc + 1) * tcw])

    def gates(tc):
        for ig in range(ig_n):
            for l in range(lt_per):
                lt = tc * lt_per + l
                low = [row(low_ref, hh, ig, lt) for hh in range(PEER_HEADS)]
                e1 = [row(e1_ref, hh, ig, lt) for hh in range(PEER_HEADS)]
                for c in range(N_KEYS // GATE_ROWS):
                    js = slice(c * GATE_ROWS, (c + 1) * GATE_ROWS)
                    w = jnp.zeros((GATE_ROWS, LANES), F32)
                    for hh in range(PEER_HEADS):
                        hit = sel_ref[lt, 2 * hh, js, :] >= low[hh]
                        w = w + jnp.where(hit, sel_ref[lt, 2 * hh + 1, js, :], 0.0) * e1[hh]
                    rows = slice(ig * N_KEYS + c * GATE_ROWS, ig * N_KEYS + (c + 1) * GATE_ROWS)
                    a = at_scr[tc % 2, rows, l * LANES:(l + 1) * LANES]
                    zt_scr[rows, lt * LANES:(lt + 1) * LANES] = (w * _gelu_tanh(a)).astype(BF16)

    def combine(tc):
        cols = slice(tc * tcw, (tc + 1) * tcw)
        ot_ref[:, cols] += _dot(vt_ref[...], zt_scr[:, cols])

    scores(0)
    for tc in range(ntc):
        if tc + 1 < ntc:
            scores(tc + 1)
        gates(tc)
        combine(tc)

    @pl.when(eb == pl.num_programs(1) - 1)
    def _():
        y_ref[...] = _rms(x1_ref[...] + gt_ref[...] * ot_ref[...].T, fg_ref[...])


def _peer2(h2t, u_bf, vt_bf, s2, e2, low, e1, x1, gt2, final_g, grp, eblk):
    d, t = h2t.shape
    ne = u_bf.shape[0]
    nh = PEER_HEADS
    tb = TOK_VREG
    lt_n = tb // LANES
    ig_n = eblk // N_KEYS
    tile_spec = pl.BlockSpec((nh, lt_n, N_KEYS, LANES), lambda i, e: (0, i, 0, 0))
    vreg_spec = pl.BlockSpec((nh, None, ig_n, lt_n, LANES), lambda i, e: (0, i, e, 0, 0))
    return pl.pallas_call(
        _peer2_kernel,
        out_shape=jax.ShapeDtypeStruct((t, d), F32),
        grid=(t // tb, ne // eblk),
        in_specs=[pl.BlockSpec((d, tb), lambda i, e: (0, i)),
                  pl.BlockSpec((eblk, d), lambda i, e: (e, 0)),
                  pl.BlockSpec((d, eblk), lambda i, e: (0, e)),
                  pl.BlockSpec((nh, lt_n, N_KEYS, LANES), lambda i, e: (0, i, 0, 0), pipeline_mode=pl.Buffered(1)),
                  tile_spec, vreg_spec, vreg_spec,
                  pl.BlockSpec((tb, d), lambda i, e: (i, 0), pipeline_mode=pl.Buffered(1)),
                  pl.BlockSpec((None, 1, d), lambda i, e: (grp(i), 0, 0)),
                  pl.BlockSpec((1, d), lambda i, e: (0, 0))],
        out_specs=pl.BlockSpec((tb, d), lambda i, e: (i, 0)),
        scratch_shapes=[pltpu.VMEM((2, eblk, PEER_TOKEN_CHUNK), F32), pltpu.VMEM((eblk, tb), BF16),
                        pltpu.VMEM((lt_n, 2 * nh, SEL_PITCH, LANES), F32), pltpu.VMEM((d, tb), BF16),
                        pltpu.VMEM((d, tb), F32)],
        compiler_params=_cparams(("parallel", "arbitrary")),
        name="peer_experts",
    )(h2t, u_bf, vt_bf, s2, e2, low, e1, x1, gt2, final_g)


def _rope_tables(seq, dk):
    rows = seq // GRID_W
    r, c = jnp.meshgrid(jnp.arange(rows, dtype=F32), jnp.arange(GRID_W, dtype=F32), indexing='ij')
    r = r.reshape(-1)
    c = c.reshape(-1)
    nf = dk // 4
    inv = ROPE_BASE ** (-jnp.arange(nf, dtype=F32) / nf)
    ang = jnp.concatenate([r[:, None] * inv, c[:, None] * inv], axis=-1)
    cos = jnp.concatenate([jnp.cos(ang), jnp.cos(ang)], axis=-1)
    sin = jnp.concatenate([-jnp.sin(ang), jnp.sin(ang)], axis=-1)
    return cos, sin


def _pad2(a, rows, cols):
    return jnp.pad(a, ((0, rows - a.shape[0]), (0, cols - a.shape[1])))


def kernel(x_prompt, x_sample, state_ret_fwd, state_ret_bwd, c, c_ctx, w_ada, b_ada, norm1_g, w_in, hy_conv_w, hy_conv_b, hy_w1, hy_b1, hy_w2, hy_b2, hy_w3, hy_b3, hy_w4, hy_freq, hy_bias, hy_norm_g, ret_decay_fwd, ret_decay_bwd, ret_norm_g, w_out, norm2_g, peer_wq, peer_keys, peer_u, peer_v, final_g):
    b_ctx, l_ctx, d = x_prompt.shape
    b_lat, l_lat, _ = x_sample.shape
    assert w_ada.shape[0] == 1, "one trunk layer"
    t_ctx, t_lat = b_ctx * l_ctx, b_lat * l_lat
    d_hy = hy_norm_g.shape[1]
    d_ret = ret_norm_g.shape[1]
    dk = d_ret // N_RET_HEADS
    tm = TOKEN_BLOCK
    assert l_ctx % FFT_P == 0 and l_lat % TOK_VREG == 0 and t_ctx % TOK_VREG == 0 and b_lat < SUBLANES

    ngrp = 1 + b_lat
    cond = jnp.concatenate([c_ctx[None, :], c], axis=0)
    cond8 = jnp.pad(cond, ((0, SUBLANES - ngrp), (0, 0)))
    mod = _ada(cond8, w_ada[0], b_ada[0])
    sh1, sc1, gt1, sh2, sc2, gt2 = [m.reshape(SUBLANES, 1, d) for m in jnp.split(mod, 6, axis=-1)]

    fw = hy_w1.shape[-1]
    f = jnp.linspace(1e-4, HY_BANDS - 1, HY_BANDS, dtype=F32)
    frow = jnp.zeros((1, LANES), F32).at[0, 1:1 + HY_BANDS].set(f).at[0, 1 + HY_BANDS:1 + 2 * HY_BANDS].set(f)
    min_decay = math.log(HY_DECAY_TARGET) / HY_SLOW_PCT
    max_decay = math.log(HY_DECAY_TARGET) / HY_FAST_PCT
    deltas = jnp.abs(jnp.linspace(min_decay, max_decay, d_hy, dtype=F32))
    w4 = hy_w4[0].reshape(fw, 2, 2 * d_hy).transpose(1, 0, 2)
    w4 = jnp.pad(w4, ((0, 0), (0, LANES - fw), (0, 0)))
    filt_w = (frow, _pad2(hy_w1[0], LANES, LANES), _pad2(hy_b1, 1, LANES), _pad2(hy_w2[0], LANES, LANES),
              _pad2(hy_b2, 1, LANES), _pad2(hy_w3[0], LANES, LANES), _pad2(hy_b3, 1, LANES),
              _pad2(hy_freq, 1, LANES), w4, jnp.tile(deltas, 2)[None, :])
    fwd_m, inv_m, flt_m = _dft_mats()
    cos_t, sin_t = _rope_tables(l_lat, dk)
    dec_f = jnp.broadcast_to(ret_decay_fwd[0][:, None, None], (N_RET_HEADS, 1, dk))
    dec_b = jnp.broadcast_to(ret_decay_bwd[0][:, None, None], (N_RET_HEADS, 1, dk))
    ng = ret_norm_g[0].reshape(N_RET_HEADS, 1, dk)
    w_in_bf = w_in[0].astype(BF16)
    w_out_bf = w_out[0].astype(BF16)
    wqt = peer_wq[0].T.astype(BF16)
    keys = peer_keys[0].reshape(2 * PEER_HEADS, N_KEYS, -1).astype(BF16)
    u_bf = peer_u[0].astype(BF16)
    vt_bf = peer_v[0].T.astype(BF16)
    fg = final_g.reshape(1, d)

    def trunk(x, nbatch, seq, latent):
        hy_cb, ret_hps = _path_blocks(seq, d_hy)
        grp = (lambda i: 1 + i // (seq // tm)) if latent else (lambda i: 0)
        hy_in, rq, rk, rv, rg = _inproj(x, sh1, sc1, norm1_g, w_in_bf, grp, tm, 3 * d_hy, d_ret, hy_cb)
        spectra = _hyena_spectra(seq, filt_w, flt_m, hy_cb)
        y_hy = _hyena_mix(hy_in, hy_conv_w[0], hy_conv_b, hy_bias[0], spectra, fwd_m, inv_m, 0, nbatch, seq, hy_cb)
        y_ret, s_f, s_b = _retention(rq, rk, rv, rg, cos_t, sin_t, dec_f, dec_b, ng, state_ret_fwd, state_ret_bwd,
                                     nbatch, seq, latent, latent, ret_hps)
        x1, h2t = _outproj(y_hy, y_ret, x, gt1, sh2, sc2, hy_norm_g, norm2_g, w_out_bf, grp, tm)
        s2, e2, low, e1 = _peer1(h2t, wqt, keys)
        grp_tok = (lambda i: 1 + i // (seq // TOK_VREG)) if latent else (lambda i: 0)
        y = _peer2(h2t, u_bf, vt_bf, s2, e2, low, e1, x1, gt2, fg, grp_tok, PEER_EXPERT_BLOCK)
        return y.reshape(nbatch, seq, d), s_f, s_b

    y_prompt, sf_new, sb_new = trunk(x_prompt.reshape(t_ctx, d), b_ctx, l_ctx, False)
    y_sample, _, _ = trunk(x_sample.reshape(t_lat, d), b_lat, l_lat, True)
    return (y_prompt, y_sample, sf_new, sb_new)
```

```python
import functools
import math

import numpy as np
import jax
import jax.numpy as jnp
from jax import lax
from jax.experimental import pallas as pl
from jax.experimental.pallas import tpu as pltpu

F32 = jnp.float32
BF16 = jnp.bfloat16

LANES = 128
SUBLANES = 8
VMEM_LIMIT = 56 * 1024 * 1024

EPS = 1e-6
GRID_W = 64
N_RET_HEADS = 4
RET_CHUNK = 128
ROPE_BASE = 10000.0
HY_EMB = 33
HY_BANDS = (HY_EMB - 1) // 2
HY_DECAY_TARGET = 1e-2
HY_FAST_PCT = 0.3
HY_SLOW_PCT = 1.5
N_KEYS = 128
PEER_HEADS = 8
PEER_TOPK = 16
FFT_P = 256
FFT_N = 2 * FFT_P
SPEC_PITCH = FFT_P + SUBLANES
SPEC_HEAD = 2 * SUBLANES
SPEC_ROWS = 2 * SPEC_PITCH + SPEC_HEAD
TOK_VREG = SUBLANES * LANES
TILE_PITCH = N_KEYS + SUBLANES

TOKEN_BLOCK = 512
ADA_COL_BLOCK = 1536
PEER_EXPERT_BLOCK = 1024
PEER_TOKEN_CHUNK = 2 * LANES


def _path_blocks(seq, d_hy):
    return (d_hy, N_RET_HEADS) if seq == FFT_P else (LANES, 1)


def _cparams(sem):
    return pltpu.CompilerParams(dimension_semantics=sem, vmem_limit_bytes=VMEM_LIMIT)


def _dot(a, b):
    return jnp.dot(a, b, preferred_element_type=F32)


def _split(a):
    hi = a.astype(BF16)
    lo = (a - hi.astype(F32)).astype(BF16)
    return hi, lo


def _dot3(a, b):
    ah, al = _split(a)
    bh, bl = _split(b)
    return _dot(ah, bh) + (_dot(al, bh) + _dot(ah, bl))


def _rms(x, g):
    return x * lax.rsqrt(jnp.mean(x * x, axis=-1, keepdims=True) + EPS) * g


def _ada_kernel(c_ref, w_ref, b_ref, o_ref):
    c = c_ref[...]
    s = c * jax.nn.sigmoid(c)
    o_ref[...] = _dot3(s, w_ref[...]) + b_ref[...]


def _ada(cond8, w_ada, b_ada):
    d, n = w_ada.shape
    tn = ADA_COL_BLOCK
    return pl.pallas_call(
        _ada_kernel,
        out_shape=jax.ShapeDtypeStruct((cond8.shape[0], n), F32),
        grid=(n // tn,),
        in_specs=[pl.BlockSpec((cond8.shape[0], d), lambda j: (0, 0)),
                  pl.BlockSpec((d, tn), lambda j: (0, j)),
                  pl.BlockSpec((1, tn), lambda j: (0, j))],
        out_specs=pl.BlockSpec((cond8.shape[0], tn), lambda j: (0, j)),
        compiler_params=_cparams(("parallel",)),
        name="ada",
    )(cond8, w_ada, b_ada.reshape(1, n))


def _inproj_kernel(x_ref, sh_ref, sc_ref, g_ref, w_ref, hy_ref, q_ref, k_ref, v_ref, gg_ref, *, d_hy3, d_ret):
    h = _rms(x_ref[...], g_ref[...]) * (1.0 + sc_ref[...]) + sh_ref[...]
    hb = h.astype(BF16)
    hy = _dot(hb, w_ref[:, 0:d_hy3])
    cb = hy_ref.shape[2]
    for c in range(hy_ref.shape[0]):
        hy_ref[c] = hy[:, c * cb:(c + 1) * cb]
    for i, o_ref in enumerate((q_ref, k_ref, v_ref, gg_ref)):
        o_ref[...] = _dot(hb, w_ref[:, d_hy3 + i * d_ret:d_hy3 + (i + 1) * d_ret])


def _inproj(x, sh, sc, g, w_bf, grp, tm, d_hy3, d_ret, hy_cb):
    t, d = x.shape
    row = lambda i: (i, 0)
    modspec = pl.BlockSpec((None, 1, d), lambda i: (grp(i), 0, 0))
    outs = [jax.ShapeDtypeStruct((d_hy3 // hy_cb, t, hy_cb), F32)] + [jax.ShapeDtypeStruct((t, d_ret), F32)] * 4
    return pl.pallas_call(
        functools.partial(_inproj_kernel, d_hy3=d_hy3, d_ret=d_ret),
        out_shape=outs,
        grid=(t // tm,),
        in_specs=[pl.BlockSpec((tm, d), row), modspec, modspec,
                  pl.BlockSpec((1, d), lambda i: (0, 0)),
                  pl.BlockSpec(w_bf.shape, lambda i: (0, 0))],
        out_specs=[pl.BlockSpec((d_hy3 // hy_cb, tm, hy_cb), lambda i: (0, i, 0))]
        + [pl.BlockSpec((tm, d_ret), row)] * 4,
        compiler_params=_cparams(("parallel",)),
        name="inproj",
    )(x, sh, sc, g, w_bf)


def _kf_kernel(frow_ref, w1_ref, b1_ref, w2_ref, b2_ref, w3_ref, b3_ref, fr_ref, w4a_ref, w4b_ref, dl_ref, o_ref,
               *, seq):
    r = pl.program_id(0)
    half = o_ref.shape[0] // 2
    lane = lax.broadcasted_iota(jnp.int32, (half, LANES), 1)
    upper = lane >= LANES // 2
    i = r * 2 * half + lax.broadcasted_iota(jnp.int32, (half, LANES), 0) + jnp.where(upper, half, 0)
    pos = jnp.abs(i - seq).astype(F32)
    t = pos / (seq - 1.0)
    ang = frow_ref[...] * ((2.0 * math.pi) * pos / seq)
    feat = jnp.where(upper, lane - LANES // 2, lane)
    z = jnp.where(feat == 0, t,
                  jnp.where(feat <= HY_BANDS, jnp.cos(ang),
                            jnp.where(feat <= 2 * HY_BANDS, -jnp.sin(ang), 0.0)))
    fr = fr_ref[...]
    h = jnp.sin(fr * (_dot3(z, w1_ref[...]) + b1_ref[...]))
    h = jnp.sin(fr * (_dot3(h, w2_ref[...]) + b2_ref[...]))
    h = jnp.sin(fr * (_dot3(h, w3_ref[...]) + b3_ref[...]))
    nc = o_ref.shape[1]
    for k, w4_ref in enumerate((w4a_ref, w4b_ref)):
        h4 = _dot3(h, w4_ref[...])
        iw = r * 2 * half + k * half + lax.broadcasted_iota(jnp.int32, (half, nc), 0)
        tw = jnp.abs(iw - seq).astype(F32) / (seq - 1.0)
        win = jnp.exp(-tw * dl_ref[...])
        o_ref[k * half:(k + 1) * half, :] = jnp.where(iw == 0, 0.0, h4 * win)


def _spec_kernel(k0_ref, k1_ref, a_ref, o_ref):
    a = a_ref[...]
    s = (_dot(a[:, :FFT_P], k0_ref[...].astype(BF16))
         + _dot(a[:, FFT_P:], k1_ref[...].astype(BF16)))
    cb = o_ref.shape[2]
    for c in range(o_ref.shape[0]):
        cols = slice(c * cb, (c + 1) * cb)
        for p in range(2):
            r0 = p * SPEC_PITCH
            o_ref[c, r0:r0 + FFT_P, :] = s[p * FFT_P:(p + 1) * FFT_P, cols]
            o_ref[c, r0 + FFT_P:r0 + SPEC_PITCH, :] = jnp.zeros((SPEC_PITCH - FFT_P, cb), F32)
        o_ref[c, 2 * SPEC_PITCH:SPEC_ROWS, :] = s[2 * FFT_P:2 * FFT_P + SPEC_HEAD, cols]


def _dft_mats():
    f = np.arange(FFT_P)[:, None].astype(np.float64)
    t = np.arange(FFT_P)[None, :].astype(np.float64)
    j = np.arange(FFT_N)[None, :].astype(np.float64)
    w = 2.0 * np.pi / FFT_N
    fwd = np.concatenate([np.cos(w * f * t), -np.sin(w * f * t)], axis=0)
    fwd[FFT_P] = np.cos(np.pi * t[0])
    scale = np.full((1, FFT_P), 2.0 / FFT_N)
    scale[0, 0] = 1.0 / FFT_N
    inv = np.concatenate([np.cos(w * t.T * f.T) * scale, -np.sin(w * t.T * f.T) * scale], axis=1)
    inv[:, FFT_P] = np.cos(np.pi * t[0]) / FFT_N
    sgn = np.where(np.arange(FFT_P) % 2 == 0, 1.0, -1.0)[:, None]
    re = sgn * np.cos(w * f * j)
    im = -sgn * np.sin(w * f * j)
    nyq = np.cos(np.pi * j[0])
    re2 = re.copy()
    re2[0] = nyq
    im[0] = 0.0
    flt = np.concatenate([re, im, re2[:SPEC_HEAD]], axis=0)
    flt[:, 0] = 0.0
    return (jnp.asarray(fwd, F32).astype(BF16), jnp.asarray(inv, F32).astype(BF16),
            jnp.asarray(flt, F32).astype(BF16))


def _hyena_spectra(seq, fw, flt_mat, cb):
    frow, w1, b1, w2, b2, w3, b3, fr, w4_lo, w4_hi, dl = fw
    nblk = 2 * seq // FFT_P
    nc = w4_lo.shape[-1]
    full = lambda a: pl.BlockSpec(a.shape, lambda r: (0,) * a.ndim)
    w4spec = lambda k: pl.BlockSpec((None,) + w4_lo.shape[1:],
                                    lambda r: (jnp.where((2 * r + k) * FFT_P >= seq, 0, 1), 0, 0))
    kf = pl.pallas_call(
        functools.partial(_kf_kernel, seq=seq),
        out_shape=jax.ShapeDtypeStruct((2 * seq, nc), F32),
        grid=(nblk // 2,),
        in_specs=[full(frow), full(w1), full(b1), full(w2), full(b2), full(w3), full(b3), full(fr),
                  w4spec(0), w4spec(1), full(dl)],
        out_specs=pl.BlockSpec((2 * FFT_P, nc), lambda r: (r, 0)),
        compiler_params=_cparams(("parallel",)),
        name="hyena_filter",
    )(frow, w1, b1, w2, b2, w3, b3, fr, w4_lo, w4_hi, dl)
    return pl.pallas_call(
        _spec_kernel,
        out_shape=jax.ShapeDtypeStruct((nc // cb, nblk - 1, SPEC_ROWS, cb), F32),
        grid=(nblk - 1,),
        in_specs=[pl.BlockSpec((FFT_P, nc), lambda w: (w, 0)),
                  pl.BlockSpec((FFT_P, nc), lambda w: (w + 1, 0)),
                  pl.BlockSpec(flt_mat.shape, lambda w: (0, 0))],
        out_specs=pl.BlockSpec((nc // cb, None, SPEC_ROWS, cb), lambda w: (0, w, 0, 0)),
        compiler_params=_cparams(("parallel",)),
        name="hyena_spectra",
    )(kf, kf, flt_mat)


def _hy_kernel(u_ref, x_ref, cwu_ref, cbu_ref, cwx_ref, cbx_ref, bias_ref, g_ref, fwd_ref, inv_ref,
               o_ref, u_scr, gate_scr, uf_scr, y_scr, *, n, conv_u):
    seq = n * FFT_P
    cb = o_ref.shape[1]

    def sconv(v, w_ref, b_ref):
        row = lax.broadcasted_iota(jnp.int32, v.shape, 0)
        prev = jnp.where(row == 0, 0.0, pltpu.roll(v, 1, 0))
        nxt = jnp.where(row == seq - 1, 0.0, pltpu.roll(v, seq - 1, 0))
        return b_ref[...] + prev * w_ref[0:1, :] + v * w_ref[1:2, :] + nxt * w_ref[2:3, :]

    u_scr[...] = sconv(u_ref[...], cwu_ref, cbu_ref) if conv_u else u_ref[...]
    gate_scr[...] = sconv(x_ref[...], cwx_ref, cbx_ref)
    fwd = fwd_ref[...]
    grp = 2 if (cb == LANES and n % 2 == 0) else 1
    for b0 in range(0, n, grp):
        ub = [u_scr[(b0 + k) * FFT_P:(b0 + k + 1) * FFT_P, :].astype(BF16) for k in range(grp)]
        uf = _dot(fwd, ub[0] if grp == 1 else jnp.concatenate(ub, axis=1))
        for k in range(grp):
            uf_scr[b0 + k, 0:FFT_P, :] = uf[:FFT_P, k * cb:(k + 1) * cb]
            uf_scr[b0 + k, SPEC_PITCH:SPEC_PITCH + FFT_P, :] = uf[FFT_P:, k * cb:(k + 1) * cb]

    rc = SPEC_HEAD if n == 1 else SUBLANES

    def freq_rows(r0, first):
        ure = [uf_scr[b, pl.ds(r0, rc), :] for b in range(n)]
        uim = [uf_scr[b, pl.ds(SPEC_PITCH + r0, rc), :] for b in range(n)]
        for a in range(n):
            yre = yim = None
            for b in range(n):
                w = a - b + (n - 1)
                gre = g_ref[w, pl.ds(r0, rc), :]
                gim = g_ref[w, pl.ds(SPEC_PITCH + r0, rc), :]
                gre2 = g_ref[w, 2 * SPEC_PITCH:2 * SPEC_PITCH + rc, :] if first else gre
                tre = gre * ure[b] - gim * uim[b]
                tim = gre2 * uim[b] + gim * ure[b]
                yre = tre if yre is None else yre + tre
                yim = tim if yim is None else yim + tim
            y_scr[a, pl.ds(r0, rc), :] = yre
            y_scr[a, pl.ds(FFT_P + r0, rc), :] = yim

    freq_rows(0, True)

    def rows_body(i, carry):
        freq_rows(pl.multiple_of(i * rc, rc), False)
        return carry

    lax.fori_loop(1, FFT_P // rc, rows_body, 0)

    def out_body(i, carry):
        ys = [y_scr[i * grp + k].astype(BF16) for k in range(grp)]
        y = _dot(inv_ref[...], ys[0] if grp == 1 else jnp.concatenate(ys, axis=1))
        for k in range(grp):
            rows = pl.ds(pl.multiple_of((i * grp + k) * FFT_P, FFT_P), FFT_P)
            ua = u_scr[rows, :]
            o_ref[rows, :] = gate_scr[rows, :] * (y[:, k * cb:(k + 1) * cb] + ua * bias_ref[...])
        return carry

    lax.fori_loop(0, n // grp, out_body, 0)


def _hyena_order(u_arr, u_row0, u_col0, x_arr, x_row0, x_col0, conv_w, conv_b, bias_row, spectra, spec_col0,
                 fwd, inv, nbatch, seq, cb, conv_u):
    n = seq // FFT_P
    d_hy = bias_row.shape[1]
    ncb = d_hy // cb
    nwin = spectra.shape[1]
    grid = (ncb, nbatch)
    in_specs = [
        pl.BlockSpec((None, seq, cb), lambda c, b: (u_col0 + c, u_row0 + b, 0)),
        pl.BlockSpec((None, seq, cb), lambda c, b: (x_col0 + c, x_row0 + b, 0)),
        pl.BlockSpec((3, cb), lambda c, b: (0, u_col0 + c)),
        pl.BlockSpec((1, cb), lambda c, b: (0, u_col0 + c)),
        pl.BlockSpec((3, cb), lambda c, b: (0, x_col0 + c)),
        pl.BlockSpec((1, cb), lambda c, b: (0, x_col0 + c)),
        pl.BlockSpec((1, cb), lambda c, b: (0, c)),
        pl.BlockSpec((None, nwin, SPEC_ROWS, cb), lambda c, b: (spec_col0 + c, 0, 0, 0)),
        pl.BlockSpec(fwd.shape, lambda c, b: (0, 0)),
        pl.BlockSpec(inv.shape, lambda c, b: (0, 0)),
    ]
    return pl.pallas_call(
        functools.partial(_hy_kernel, n=n, conv_u=conv_u),
        out_shape=jax.ShapeDtypeStruct((ncb, nbatch * seq, cb), F32),
        grid=grid,
        in_specs=in_specs,
        out_specs=pl.BlockSpec((None, seq, cb), lambda c, b: (c, b, 0)),
        scratch_shapes=[pltpu.VMEM((seq, cb), F32), pltpu.VMEM((seq, cb), F32),
                        pltpu.VMEM((n, 2 * SPEC_PITCH, cb), F32), pltpu.VMEM((n, FFT_N, cb), F32)],
        compiler_params=_cparams(("parallel", "parallel")),
        name="hyena_conv",
    )(u_arr, x_arr, conv_w, conv_b, conv_w, conv_b, bias_row, spectra, fwd, inv)


def _hyena_mix(hy_in, conv_w, conv_b, bias, spectra, fwd, inv, row0, nbatch, seq, cb):
    ncb = bias.shape[1] // cb
    z1 = _hyena_order(hy_in, row0, 0, hy_in, row0, ncb, conv_w, conv_b, bias[0:1], spectra, 0,
                      fwd, inv, nbatch, seq, cb, True)
    return _hyena_order(z1, 0, 0, hy_in, row0, 2 * ncb, conv_w, conv_b, bias[1:2], spectra, ncb,
                        fwd, inv, nbatch, seq, cb, False)


def _ret_kernel(q_ref, k_ref, v_ref, g_ref, cos_ref, sin_ref, df_ref, db_ref, ng_ref, s0f_ref, s0b_ref,
                y_ref, sf_ref, sb_ref, q_scr, k_scr, sb_scr, *, n, rope, has_state):
    c = RET_CHUNK
    dk = cos_ref.shape[1]
    hps = df_ref.shape[0]
    ri = lax.broadcasted_iota(jnp.int32, (c, c), 0)
    ci = lax.broadcasted_iota(jnp.int32, (c, c), 1)
    diff = (ri - ci).astype(F32)
    pos = lax.broadcasted_iota(jnp.int32, (c, dk), 0).astype(F32)

    def decay_terms(hh):
        lgf = -jnp.exp(df_ref[hh])
        lgb = -jnp.exp(db_ref[hh])
        mask = (jnp.where(diff >= 0, jnp.exp(jnp.maximum(diff, 0.0) * lgf), 0.0)
                + jnp.where(diff <= 0, jnp.exp(jnp.maximum(-diff, 0.0) * lgb), 0.0))
        return dict(mask=mask,
                    xi_f=jnp.exp((pos + 1.0) * lgf), ze_f=jnp.exp((c - 1.0 - pos) * lgf), cd_f=jnp.exp(c * lgf),
                    xi_b=jnp.exp((c - pos) * lgb), ze_b=jnp.exp(pos * lgb), cd_b=jnp.exp(c * lgb))

    dec = [decay_terms(hh) for hh in range(hps)]
    cols = lambda hh: slice(hh * dk, (hh + 1) * dk)

    for hh in range(hps):
        q = q_ref[:, cols(hh)]
        k = k_ref[:, cols(hh)]
        if rope:
            cs = cos_ref[...]
            sn = sin_ref[...]
            q = q * cs + pltpu.roll(q, dk // 2, 1) * sn
            k = k * cs + pltpu.roll(k, dk // 2, 1) * sn
        q_scr[:, cols(hh)] = q
        k_scr[:, cols(hh)] = k * (dk ** -0.5)
        sf_ref[hh] = s0f_ref[hh] if has_state else jnp.zeros((dk, dk), F32)
        sb_ref[hh] = s0b_ref[hh] if has_state else jnp.zeros((dk, dk), F32)

    def dot_tn(a, b):
        return lax.dot_general(a, b, (((0,), (0,)), ((), ())), preferred_element_type=F32)

    def chunk(i):
        return pl.ds(pl.multiple_of(i * c, c), c)

    def bwd_body(j, carry):
        i = n - 1 - j
        rows = chunk(i)
        for hh in range(hps):
            s = sb_ref[hh]
            sb_scr[hh, i] = s
            kz = (k_scr[rows, cols(hh)] * dec[hh]["ze_b"]).astype(BF16)
            sb_ref[hh] = s * dec[hh]["cd_b"] + dot_tn(kz, v_ref[rows, cols(hh)].astype(BF16))
        return carry

    unroll = math.gcd(n, 8)
    lax.fori_loop(0, n, bwd_body, 0, unroll=unroll)

    def fwd_body(i, carry):
        rows = chunk(i)
        for hh in range(hps):
            d = dec[hh]
            s = sf_ref[hh]
            qb = q_scr[rows, cols(hh)].astype(BF16)
            kc = k_scr[rows, cols(hh)]
            vb = v_ref[rows, cols(hh)].astype(BF16)
            sc = lax.dot_general(qb, kc.astype(BF16), (((1,), (1,)), ((), ())), preferred_element_type=F32)
            o = _dot((sc * d["mask"]).astype(BF16), vb)
            o = o + _dot(qb, s.astype(BF16)) * d["xi_f"] + _dot(qb, sb_scr[hh, i].astype(BF16)) * d["xi_b"]
            gt = g_ref[rows, cols(hh)]
            y_ref[rows, cols(hh)] = _rms(o, ng_ref[hh]) * (gt * jax.nn.sigmoid(gt))
            sf_ref[hh] = s * d["cd_f"] + dot_tn((kc * d["ze_f"]).astype(BF16), vb)
        return carry

    lax.fori_loop(0, n, fwd_body, 0, unroll=unroll)


def _retention(q, k, v, g, cos, sin, dec_f, dec_b, norm_g, s0f, s0b, nbatch, seq, rope, has_state, hps):
    nh = N_RET_HEADS
    dk = q.shape[1] // nh
    n = seq // RET_CHUNK
    tspec = pl.BlockSpec((seq, hps * dk), lambda b, h: (b, h))
    rspec = pl.BlockSpec((seq, dk), lambda b, h: (0, 0))
    hspec = pl.BlockSpec((hps, 1, dk), lambda b, h: (h, 0, 0))
    sspec = pl.BlockSpec((None, None, hps, dk, dk), lambda b, h: (b, 0, h, 0, 0))
    s0spec = sspec if has_state else pl.BlockSpec((None, None, hps, dk, dk), lambda b, h: (0, 0, h, 0, 0))
    st_shape = jax.ShapeDtypeStruct((nbatch, 1, nh, dk, dk), F32)
    return pl.pallas_call(
        functools.partial(_ret_kernel, n=n, rope=rope, has_state=has_state),
        out_shape=[jax.ShapeDtypeStruct((nbatch * seq, nh * dk), F32), st_shape, st_shape],
        grid=(nbatch, nh // hps),
        in_specs=[tspec, tspec, tspec, tspec, rspec, rspec, hspec, hspec, hspec, s0spec, s0spec],
        out_specs=[tspec, sspec, sspec],
        scratch_shapes=[pltpu.VMEM((seq, hps * dk), F32), pltpu.VMEM((seq, hps * dk), F32),
                        pltpu.VMEM((hps, n, dk, dk), F32)],
        compiler_params=_cparams(("parallel", "parallel")),
        name="retention",
    )(q, k, v, g, cos, sin, dec_f, dec_b, norm_g, s0f, s0b)


def _outproj_kernel(yh_ref, yr_ref, x_ref, gt_ref, sh_ref, sc_ref, hg_ref, ng_ref, wo_ref, x1_ref, h2t_ref):
    ncb = yh_ref.shape[0]
    yh = yh_ref[0] if ncb == 1 else jnp.concatenate([yh_ref[c] for c in range(ncb)], axis=1)
    d_hy = yh.shape[1]
    nh = _rms(yh, hg_ref[...]).astype(BF16)
    y = _dot(nh, wo_ref[0:d_hy, :]) + _dot(yr_ref[...].astype(BF16), wo_ref[d_hy:, :])
    x1 = x_ref[...] + gt_ref[...] * y
    x1_ref[...] = x1
    h2 = _rms(x1, ng_ref[...]) * (1.0 + sc_ref[...]) + sh_ref[...]
    h2t_ref[...] = h2.T.astype(BF16)


def _outproj(y_hy, y_ret, x, gt1, sh2, sc2, hy_g, n2_g, wo_bf, grp, tm):
    t, d = x.shape
    row = lambda i: (i, 0)
    modspec = pl.BlockSpec((None, 1, d), lambda i: (grp(i), 0, 0))
    return pl.pallas_call(
        _outproj_kernel,
        out_shape=[jax.ShapeDtypeStruct((t, d), F32), jax.ShapeDtypeStruct((d, t), BF16)],
        grid=(t // tm,),
        in_specs=[pl.BlockSpec((y_hy.shape[0], tm, y_hy.shape[2]), lambda i: (0, i, 0)),
                  pl.BlockSpec((tm, y_ret.shape[1]), row),
                  pl.BlockSpec((tm, d), row), modspec, modspec, modspec,
                  pl.BlockSpec(hy_g.shape, lambda i: (0, 0)),
                  pl.BlockSpec((1, d), lambda i: (0, 0)),
                  pl.BlockSpec(wo_bf.shape, lambda i: (0, 0))],
        out_specs=[pl.BlockSpec((tm, d), row), pl.BlockSpec((d, tm), lambda i: (0, i))],
        compiler_params=_cparams(("parallel",)),
        name="outproj",
    )(y_hy, y_ret, x, gt1, sh2, sc2, hy_g, n2_g, wo_bf)


def _sort_desc(xs):
    xs = list(xs)
    n = len(xs)
    k = 2
    while k <= n:
        j = k // 2
        while j >= 1:
            for i in range(n):
                l = i ^ j
                if l > i:
                    hi, lo = jnp.maximum(xs[i], xs[l]), jnp.minimum(xs[i], xs[l])
                    xs[i], xs[l] = (hi, lo) if (i & k) == 0 else (lo, hi)
            j //= 2
        k *= 2
    return xs


def _merge_top(a, b):
    n = len(a)
    xs = [jnp.maximum(a[i], b[n - 1 - i]) for i in range(n)]
    j = n // 2
    while j >= 1:
        for i in range(n):
            l = i ^ j
            if l > i:
                xs[i], xs[l] = jnp.maximum(xs[i], xs[l]), jnp.minimum(xs[i], xs[l])
        j //= 2
    return xs


def _top_sorted(load, count, k):
    acc = None
    for g0 in range(0, count, k):
        grp = _sort_desc([load(i) for i in range(g0, g0 + k)])
        acc = grp if acc is None else _merge_top(acc, grp)
    return acc


def _staircase(k):
    return [(a, b) for a in range(k) for b in range(k) if (a + 1) * (b + 1) <= k + 1]


def _peer1_kernel(h2t_ref, wq_ref, keys_ref, s2_ref, e2_ref, low_ref, e1_ref, q_scr, s1_scr, s2_scr):
    h = pl.program_id(1)
    lt_n = s2_ref.shape[0]
    nk = N_KEYS
    kk = PEER_TOPK

    @pl.when(h == 0)
    def _():
        q_scr[...] = _dot(wq_ref[...], h2t_ref[...]).astype(BF16)

    dq = q_scr.shape[0] // (2 * PEER_HEADS)
    for p, scr in ((0, s1_scr), (1, s2_scr)):
        rows = pl.ds(pl.multiple_of((2 * h + p) * dq, dq), dq)
        s = _dot(keys_ref[2 * h + p], q_scr[rows, :])
        for lt in range(lt_n):
            scr[lt * TILE_PITCH:lt * TILE_PITCH + nk, :] = s[:, lt * LANES:(lt + 1) * LANES]

    ld1 = lambda i: s1_scr[pl.ds(i, lt_n, stride=TILE_PITCH), :]
    ld2 = lambda i: s2_scr[pl.ds(i, lt_n, stride=TILE_PITCH), :]
    v1 = _top_sorted(ld1, nk, kk)
    v2 = _top_sorted(ld2, nk, kk)
    pairs = _staircase(kk)
    cand = [v1[a] + v2[b] for a, b in pairs]
    neg = jnp.full_like(cand[0], -jnp.inf)
    padded = cand + [neg] * (-len(cand) % kk)
    top = _top_sorted(lambda i: padded[i], len(padded), kk)
    tau = top[kk - 1]
    nxt = neg
    for cnd in cand:
        nxt = jnp.maximum(nxt, jnp.where(cnd < tau, cnd, neg))
    thr = 0.5 * (tau + nxt)
    m = v1[0] + v2[0]
    z = jnp.zeros_like(tau)
    for cnd in cand:
        z = z + jnp.where(cnd >= tau, jnp.exp(cnd - m), 0.0)
    inv_z = 1.0 / z

    for i in range(nk):
        x1 = ld1(i)
        gap = thr - x1
        low = jnp.full_like(x1, jnp.inf)
        for b in range(kk):
            low = jnp.where(v2[b] > gap, v2[b], low)
        keep = x1 >= v1[kk - 1]
        low_ref[i] = jnp.where(keep, low, jnp.inf)
        e1_ref[i] = jnp.where(keep, jnp.exp(x1 - v1[0]) * inv_z, 0.0)
    for lt in range(lt_n):
        s2 = s2_scr[lt * TILE_PITCH:lt * TILE_PITCH + nk, :]
        s2_ref[lt] = s2
        e2_ref[lt] = jnp.where(s2 >= v2[kk - 1][lt:lt + 1, :], jnp.exp(s2 - v2[0][lt:lt + 1, :]), 0.0).astype(BF16)


def _peer1(h2t, wqt_bf, keys_bf):
    d, t = h2t.shape
    nh = PEER_HEADS
    tb = TOK_VREG
    lt_n = tb // LANES
    tile_shape = lambda dt: jax.ShapeDtypeStruct((nh, t // LANES, N_KEYS, LANES), dt)
    vreg_shape = jax.ShapeDtypeStruct((nh, t // tb, N_KEYS, lt_n, LANES), F32)
    tile_spec = pl.BlockSpec((None, lt_n, N_KEYS, LANES), lambda i, h: (h, i, 0, 0))
    vreg_spec = pl.BlockSpec((None, None, N_KEYS, lt_n, LANES), lambda i, h: (h, i, 0, 0, 0))
    return pl.pallas_call(
        _peer1_kernel,
        out_shape=[tile_shape(F32), tile_shape(BF16), vreg_shape, vreg_shape],
        grid=(t // tb, nh),
        in_specs=[pl.BlockSpec((d, tb), lambda i, h: (0, i)),
                  pl.BlockSpec(wqt_bf.shape, lambda i, h: (0, 0)),
                  pl.BlockSpec(keys_bf.shape, lambda i, h: (0, 0, 0))],
        out_specs=[tile_spec, tile_spec, vreg_spec, vreg_spec],
        scratch_shapes=[pltpu.VMEM((wqt_bf.shape[0], tb), BF16),
                        pltpu.VMEM((lt_n * TILE_PITCH, LANES), F32),
                        pltpu.VMEM((lt_n * TILE_PITCH, LANES), F32)],
        compiler_params=_cparams(("parallel", "arbitrary")),
        name="peer_select",
    )(h2t, wqt_bf, keys_bf)


def _gelu_tanh(x):
    k = -2.0 * math.sqrt(2.0 / math.pi) * math.log2(math.e)
    e = jnp.exp2(x * (x * x * (k * 0.044715) + k))
    return x / (1.0 + e)


GATE_ROWS = 2 * SUBLANES
SEL_PITCH = N_KEYS + SUBLANES


def _peer2_kernel(h2t_in, u_ref, vt_ref, s2_in, e2_in, low_ref, e1_ref, x1_ref, gt_ref, fg_ref, y_ref,
                  at_scr, zt_scr, sel_ref, h2t_ref, ot_ref):
    eb = pl.program_id(1)
    lt_n = s2_in.shape[1]
    ig_n = low_ref.shape[1]

    @pl.when(eb == 0)
    def _():
        ot_ref[...] = jnp.zeros_like(ot_ref)
        for hh in range(PEER_HEADS):
            sel_ref[:, 2 * hh, 0:N_KEYS, :] = s2_in[hh]
            sel_ref[:, 2 * hh + 1, 0:N_KEYS, :] = e2_in[hh].astype(F32)
        h2t_ref[...] = h2t_in[...]

    tcw = at_scr.shape[2]
    ntc = lt_n * LANES // tcw
    lt_per = tcw // LANES

    def row(ref, hh, ig, lt):
        return jnp.broadcast_to(ref[hh, ig, lt:lt + 1, :], (GATE_ROWS, LANES))

    def scores(tc):
        at_scr[tc % 2] = _dot(u_ref[...], h2t_ref[:, tc * tcw:(tc + 1) * tcw])

    def gates(tc):
        for ig in range(ig_n):
            for l in range(lt_per):
                lt = tc * lt_per + l
                low = [row(low_ref, hh, ig, lt) for hh in range(PEER_HEADS)]
                e1 = [row(e1_ref, hh, ig, lt) for hh in range(PEER_HEADS)]
                for c in range(N_KEYS // GATE_ROWS):
                    js = slice(c * GATE_ROWS, (c + 1) * GATE_ROWS)
                    w = jnp.zeros((GATE_ROWS, LANES), F32)
                    for hh in range(PEER_HEADS):
                        hit = sel_ref[lt, 2 * hh, js, :] >= low[hh]
                        w = w + jnp.where(hit, sel_ref[lt, 2 * hh + 1, js, :], 0.0) * e1[hh]
                    rows = slice(ig * N_KEYS + c * GATE_ROWS, ig * N_KEYS + (c + 1) * GATE_ROWS)
                    a = at_scr[tc % 2, rows, l * LANES:(l + 1) * LANES]
                    zt_scr[rows, lt * LANES:(lt + 1) * LANES] = (w * _gelu_tanh(a)).astype(BF16)

    def combine(tc):
        cols = slice(tc * tcw, (tc + 1) * tcw)
        ot_ref[:, cols] += _dot(vt_ref[...], zt_scr[:, cols])

    scores(0)
    for tc in range(ntc):
        if tc + 1 < ntc:
            scores(tc + 1)
        gates(tc)
        combine(tc)

    @pl.when(eb == pl.num_programs(1) - 1)
    def _():
        y_ref[...] = _rms(x1_ref[...] + gt_ref[...] * ot_ref[...].T, fg_ref[...])


def _peer2(h2t, u_bf, vt_bf, s2, e2, low, e1, x1, gt2, final_g, grp, eblk):
    d, t = h2t.shape
    ne = u_bf.shape[0]
    nh = PEER_HEADS
    tb = TOK_VREG
    lt_n = tb // LANES
    ig_n = eblk // N_KEYS
    tile_spec = pl.BlockSpec((nh, lt_n, N_KEYS, LANES), lambda i, e: (0, i, 0, 0))
    vreg_spec = pl.BlockSpec((nh, None, ig_n, lt_n, LANES), lambda i, e: (0, i, e, 0, 0))
    return pl.pallas_call(
        _peer2_kernel,
        out_shape=jax.ShapeDtypeStruct((t, d), F32),
        grid=(t // tb, ne // eblk),
        in_specs=[pl.BlockSpec((d, tb), lambda i, e: (0, i)),
                  pl.BlockSpec((eblk, d), lambda i, e: (e, 0)),
                  pl.BlockSpec((d, eblk), lambda i, e: (0, e)),
                  pl.BlockSpec((nh, lt_n, N_KEYS, LANES), lambda i, e: (0, i, 0, 0), pipeline_mode=pl.Buffered(1)),
                  tile_spec, vreg_spec, vreg_spec,
                  pl.BlockSpec((tb, d), lambda i, e: (i, 0), pipeline_mode=pl.Buffered(1)),
                  pl.BlockSpec((None, 1, d), lambda i, e: (grp(i), 0, 0)),
                  pl.BlockSpec((1, d), lambda i, e: (0, 0))],
        out_specs=pl.BlockSpec((tb, d), lambda i, e: (i, 0)),
        scratch_shapes=[pltpu.VMEM((2, eblk, PEER_TOKEN_CHUNK), F32), pltpu.VMEM((eblk, tb), BF16),
                        pltpu.VMEM((lt_n, 2 * nh, SEL_PITCH, LANES), F32), pltpu.VMEM((d, tb), BF16),
                        pltpu.VMEM((d, tb), F32)],
        compiler_params=_cparams(("parallel", "arbitrary")),
        name="peer_experts",
    )(h2t, u_bf, vt_bf, s2, e2, low, e1, x1, gt2, final_g)


def _rope_tables(seq, dk):
    rows = seq // GRID_W
    r, c = jnp.meshgrid(jnp.arange(rows, dtype=F32), jnp.arange(GRID_W, dtype=F32), indexing='ij')
    r = r.reshape(-1)
    c = c.reshape(-1)
    nf = dk // 4
    inv = ROPE_BASE ** (-jnp.arange(nf, dtype=F32) / nf)
    ang = jnp.concatenate([r[:, None] * inv, c[:, None] * inv], axis=-1)
    cos = jnp.concatenate([jnp.cos(ang), jnp.cos(ang)], axis=-1)
    sin = jnp.concatenate([-jnp.sin(ang), jnp.sin(ang)], axis=-1)
    return cos, sin


def _pad2(a, rows, cols):
    return jnp.pad(a, ((0, rows - a.shape[0]), (0, cols - a.shape[1])))


def kernel(x_prompt, x_sample, state_ret_fwd, state_ret_bwd, c, c_ctx, w_ada, b_ada, norm1_g, w_in, hy_conv_w, hy_conv_b, hy_w1, hy_b1, hy_w2, hy_b2, hy_w3, hy_b3, hy_w4, hy_freq, hy_bias, hy_norm_g, ret_decay_fwd, ret_decay_bwd, ret_norm_g, w_out, norm2_g, peer_wq, peer_keys, peer_u, peer_v, final_g):
    b_ctx, l_ctx, d = x_prompt.shape
    b_lat, l_lat, _ = x_sample.shape
    assert w_ada.shape[0] == 1, "one trunk layer"
    t_ctx, t_lat = b_ctx * l_ctx, b_lat * l_lat
    d_hy = hy_norm_g.shape[1]
    d_ret = ret_norm_g.shape[1]
    dk = d_ret // N_RET_HEADS
    tm = TOKEN_BLOCK
    assert l_ctx % FFT_P == 0 and l_lat % TOK_VREG == 0 and t_ctx % TOK_VREG == 0 and b_lat < SUBLANES

    ngrp = 1 + b_lat
    cond = jnp.concatenate([c_ctx[None, :], c], axis=0)
    cond8 = jnp.pad(cond, ((0, SUBLANES - ngrp), (0, 0)))
    mod = _ada(cond8, w_ada[0], b_ada[0])
    sh1, sc1, gt1, sh2, sc2, gt2 = [m.reshape(SUBLANES, 1, d) for m in jnp.split(mod, 6, axis=-1)]

    fw = hy_w1.shape[-1]
    hl = LANES // 2
    assert fw <= hl and HY_EMB <= hl
    f = jnp.linspace(1e-4, HY_BANDS - 1, HY_BANDS, dtype=F32)
    frow = jnp.zeros((1, hl), F32).at[0, 1:1 + HY_BANDS].set(f).at[0, 1 + HY_BANDS:1 + 2 * HY_BANDS].set(f)
    min_decay = math.log(HY_DECAY_TARGET) / HY_SLOW_PCT
    max_decay = math.log(HY_DECAY_TARGET) / HY_FAST_PCT
    deltas = jnp.abs(jnp.linspace(min_decay, max_decay, d_hy, dtype=F32))
    twice = lambda a: jnp.tile(_pad2(a, 1, hl), (1, 2))
    diag2 = lambda a: jnp.kron(jnp.eye(2, dtype=F32), _pad2(a, hl, hl))
    w4 = hy_w4[0].reshape(fw, 2, 2 * d_hy).transpose(1, 0, 2)
    w4_lo = jnp.pad(w4, ((0, 0), (0, LANES - fw), (0, 0)))
    w4_hi = jnp.pad(w4, ((0, 0), (hl, hl - fw), (0, 0)))
    filt_w = (twice(frow), diag2(hy_w1[0]), twice(hy_b1), diag2(hy_w2[0]), twice(hy_b2), diag2(hy_w3[0]),
              twice(hy_b3), twice(hy_freq), w4_lo, w4_hi, jnp.tile(deltas, 2)[None, :])
    fwd_m, inv_m, flt_m = _dft_mats()
    cos_t, sin_t = _rope_tables(l_lat, dk)
    dec_f = jnp.broadcast_to(ret_decay_fwd[0][:, None, None], (N_RET_HEADS, 1, dk))
    dec_b = jnp.broadcast_to(ret_decay_bwd[0][:, None, None], (N_RET_HEADS, 1, dk))
    ng = ret_norm_g[0].reshape(N_RET_HEADS, 1, dk)
    w_in_bf = w_in[0].astype(BF16)
    w_out_bf = w_out[0].astype(BF16)
    wqt = peer_wq[0].T.astype(BF16)
    keys = peer_keys[0].reshape(2 * PEER_HEADS, N_KEYS, -1).astype(BF16)
    u_bf = peer_u[0].astype(BF16)
    vt_bf = peer_v[0].T.astype(BF16)
    fg = final_g.reshape(1, d)

    def trunk(x, nbatch, seq, latent):
        hy_cb, ret_hps = _path_blocks(seq, d_hy)
        grp = (lambda i: 1 + i // (seq // tm)) if latent else (lambda i: 0)
        hy_in, rq, rk, rv, rg = _inproj(x, sh1, sc1, norm1_g, w_in_bf, grp, tm, 3 * d_hy, d_ret, hy_cb)
        spectra = _hyena_spectra(seq, filt_w, flt_m, hy_cb)
        y_hy = _hyena_mix(hy_in, hy_conv_w[0], hy_conv_b, hy_bias[0], spectra, fwd_m, inv_m, 0, nbatch, seq, hy_cb)
        y_ret, s_f, s_b = _retention(rq, rk, rv, rg, cos_t, sin_t, dec_f, dec_b, ng, state_ret_fwd, state_ret_bwd,
                                     nbatch, seq, latent, latent, ret_hps)
        x1, h2t = _outproj(y_hy, y_ret, x, gt1, sh2, sc2, hy_norm_g, norm2_g, w_out_bf, grp, tm)
        s2, e2, low, e1 = _peer1(h2t, wqt, keys)
        grp_tok = (lambda i: 1 + i // (seq // TOK_VREG)) if latent else (lambda i: 0)
        y = _peer2(h2t, u_bf, vt_bf, s2, e2, low, e1, x1, gt2, fg, grp_tok, PEER_EXPERT_BLOCK)
        return y.reshape(nbatch, seq, d), s_f, s_b

    y_prompt, sf_new, sb_new = trunk(x_prompt.reshape(t_ctx, d), b_ctx, l_ctx, False)
    y_sample, _, _ = trunk(x_sample.reshape(t_lat, d), b_lat, l_lat, True)
    return (y_prompt, y_sample, sf_new, sb_new)
```

```python
import functools
import math

import numpy as np
import jax
import jax.numpy as jnp
from jax import lax
from jax.experimental import pallas as pl
from jax.experimental.pallas import tpu as pltpu

F32 = jnp.float32
BF16 = jnp.bfloat16

LANES = 128
SUBLANES = 8
VMEM_LIMIT = 56 * 1024 * 1024

EPS = 1e-6
GRID_W = 64
N_RET_HEADS = 4
RET_CHUNK = 128
RET_CHUNK_LONG = 256
ROPE_BASE = 10000.0
HY_EMB = 33
HY_BANDS = (HY_EMB - 1) // 2
HY_DECAY_TARGET = 1e-2
HY_FAST_PCT = 0.3
HY_SLOW_PCT = 1.5
N_KEYS = 128
PEER_HEADS = 8
PEER_TOPK = 16
FFT_P = 256
FFT_N = 2 * FFT_P
SPEC_PITCH = FFT_P + SUBLANES
SPEC_HEAD = 2 * SUBLANES
SPEC_ROWS = 2 * SPEC_PITCH + SPEC_HEAD
TOK_VREG = SUBLANES * LANES
TILE_PITCH = N_KEYS + SUBLANES

TOKEN_BLOCK = 512
ADA_COL_BLOCK = 1536
PEER_EXPERT_BLOCK = 1024
PEER_TOKEN_CHUNK = 2 * LANES


def _path_blocks(seq, d_hy):
    return (d_hy, N_RET_HEADS) if seq == FFT_P else (LANES, 1)


def _cparams(sem):
    return pltpu.CompilerParams(dimension_semantics=sem, vmem_limit_bytes=VMEM_LIMIT)


def _dot(a, b):
    return jnp.dot(a, b, preferred_element_type=F32)


def _split(a):
    hi = a.astype(BF16)
    lo = (a - hi.astype(F32)).astype(BF16)
    return hi, lo


def _dot3(a, b):
    ah, al = _split(a)
    bh, bl = _split(b)
    return _dot(ah, bh) + (_dot(al, bh) + _dot(ah, bl))


def _rms(x, g):
    return x * lax.rsqrt(jnp.mean(x * x, axis=-1, keepdims=True) + EPS) * g


def _ada_kernel(c_ref, w_ref, b_ref, o_ref):
    c = c_ref[...]
    s = c * jax.nn.sigmoid(c)
    o_ref[...] = _dot3(s, w_ref[...]) + b_ref[...]


def _ada(cond8, w_ada, b_ada):
    d, n = w_ada.shape
    tn = ADA_COL_BLOCK
    return pl.pallas_call(
        _ada_kernel,
        out_shape=jax.ShapeDtypeStruct((cond8.shape[0], n), F32),
        grid=(n // tn,),
        in_specs=[pl.BlockSpec((cond8.shape[0], d), lambda j: (0, 0)),
                  pl.BlockSpec((d, tn), lambda j: (0, j)),
                  pl.BlockSpec((1, tn), lambda j: (0, j))],
        out_specs=pl.BlockSpec((cond8.shape[0], tn), lambda j: (0, j)),
        compiler_params=_cparams(("parallel",)),
        name="ada",
    )(cond8, w_ada, b_ada.reshape(1, n))


def _inproj_kernel(x_ref, sh_ref, sc_ref, g_ref, w_ref, hy_ref, q_ref, k_ref, v_ref, gg_ref, *, d_hy3, d_ret):
    h = _rms(x_ref[...], g_ref[...]) * (1.0 + sc_ref[...]) + sh_ref[...]
    hb = h.astype(BF16)
    hy = _dot(hb, w_ref[:, 0:d_hy3])
    cb = hy_ref.shape[2]
    for c in range(hy_ref.shape[0]):
        hy_ref[c] = hy[:, c * cb:(c + 1) * cb]
    for i, o_ref in enumerate((q_ref, k_ref, v_ref, gg_ref)):
        o_ref[...] = _dot(hb, w_ref[:, d_hy3 + i * d_ret:d_hy3 + (i + 1) * d_ret])


def _inproj(x, sh, sc, g, w_bf, grp, tm, d_hy3, d_ret, hy_cb):
    t, d = x.shape
    row = lambda i: (i, 0)
    modspec = pl.BlockSpec((None, 1, d), lambda i: (grp(i), 0, 0))
    outs = [jax.ShapeDtypeStruct((d_hy3 // hy_cb, t, hy_cb), F32)] + [jax.ShapeDtypeStruct((t, d_ret), F32)] * 4
    return pl.pallas_call(
        functools.partial(_inproj_kernel, d_hy3=d_hy3, d_ret=d_ret),
        out_shape=outs,
        grid=(t // tm,),
        in_specs=[pl.BlockSpec((tm, d), row), modspec, modspec,
                  pl.BlockSpec((1, d), lambda i: (0, 0)),
                  pl.BlockSpec(w_bf.shape, lambda i: (0, 0))],
        out_specs=[pl.BlockSpec((d_hy3 // hy_cb, tm, hy_cb), lambda i: (0, i, 0))]
        + [pl.BlockSpec((tm, d_ret), row)] * 4,
        compiler_params=_cparams(("parallel",)),
        name="inproj",
    )(x, sh, sc, g, w_bf)


def _kf_kernel(frow_ref, w1_ref, b1_ref, w2_ref, b2_ref, w3_ref, b3_ref, fr_ref, w4a_ref, w4b_ref, dl_ref, o_ref,
               *, seq):
    r = pl.program_id(0)
    half = o_ref.shape[0] // 2
    lane = lax.broadcasted_iota(jnp.int32, (half, LANES), 1)
    upper = lane >= LANES // 2
    i = r * 2 * half + lax.broadcasted_iota(jnp.int32, (half, LANES), 0) + jnp.where(upper, half, 0)
    pos = jnp.abs(i - seq).astype(F32)
    t = pos / (seq - 1.0)
    ang = frow_ref[...] * ((2.0 * math.pi) * pos / seq)
    feat = jnp.where(upper, lane - LANES // 2, lane)
    z = jnp.where(feat == 0, t,
                  jnp.where(feat <= HY_BANDS, jnp.cos(ang),
                            jnp.where(feat <= 2 * HY_BANDS, -jnp.sin(ang), 0.0)))
    fr = fr_ref[...]
    h = jnp.sin(fr * (_dot3(z, w1_ref[...]) + b1_ref[...]))
    h = jnp.sin(fr * (_dot3(h, w2_ref[...]) + b2_ref[...]))
    h = jnp.sin(fr * (_dot3(h, w3_ref[...]) + b3_ref[...]))
    nc = o_ref.shape[1]
    for k, w4_ref in enumerate((w4a_ref, w4b_ref)):
        h4 = _dot3(h, w4_ref[...])
        iw = r * 2 * half + k * half + lax.broadcasted_iota(jnp.int32, (half, nc), 0)
        tw = jnp.abs(iw - seq).astype(F32) / (seq - 1.0)
        win = jnp.exp(-tw * dl_ref[...])
        o_ref[k * half:(k + 1) * half, :] = jnp.where(iw == 0, 0.0, h4 * win)


def _spec_kernel(k0_ref, k1_ref, a_ref, o_ref):
    a = a_ref[...]
    s = (_dot(a[:, :FFT_P], k0_ref[...].astype(BF16))
         + _dot(a[:, FFT_P:], k1_ref[...].astype(BF16)))
    cb = o_ref.shape[2]
    for c in range(o_ref.shape[0]):
        cols = slice(c * cb, (c + 1) * cb)
        for p in range(2):
            r0 = p * SPEC_PITCH
            o_ref[c, r0:r0 + FFT_P, :] = s[p * FFT_P:(p + 1) * FFT_P, cols]
            o_ref[c, r0 + FFT_P:r0 + SPEC_PITCH, :] = jnp.zeros((SPEC_PITCH - FFT_P, cb), F32)
        o_ref[c, 2 * SPEC_PITCH:SPEC_ROWS, :] = s[2 * FFT_P:2 * FFT_P + SPEC_HEAD, cols]


def _dft_mats():
    f = np.arange(FFT_P)[:, None].astype(np.float64)
    t = np.arange(FFT_P)[None, :].astype(np.float64)
    j = np.arange(FFT_N)[None, :].astype(np.float64)
    w = 2.0 * np.pi / FFT_N
    fwd = np.concatenate([np.cos(w * f * t), -np.sin(w * f * t)], axis=0)
    fwd[FFT_P] = np.cos(np.pi * t[0])
    scale = np.full((1, FFT_P), 2.0 / FFT_N)
    scale[0, 0] = 1.0 / FFT_N
    inv = np.concatenate([np.cos(w * t.T * f.T) * scale, -np.sin(w * t.T * f.T) * scale], axis=1)
    inv[:, FFT_P] = np.cos(np.pi * t[0]) / FFT_N
    sgn = np.where(np.arange(FFT_P) % 2 == 0, 1.0, -1.0)[:, None]
    re = sgn * np.cos(w * f * j)
    im = -sgn * np.sin(w * f * j)
    nyq = np.cos(np.pi * j[0])
    re2 = re.copy()
    re2[0] = nyq
    im[0] = 0.0
    flt = np.concatenate([re, im, re2[:SPEC_HEAD]], axis=0)
    flt[:, 0] = 0.0
    return (jnp.asarray(fwd, F32).astype(BF16), jnp.asarray(inv, F32).astype(BF16),
            jnp.asarray(flt, F32).astype(BF16))


def _hyena_spectra(seq, fw, flt_mat, cb):
    frow, w1, b1, w2, b2, w3, b3, fr, w4_lo, w4_hi, dl = fw
    nblk = 2 * seq // FFT_P
    nc = w4_lo.shape[-1]
    full = lambda a: pl.BlockSpec(a.shape, lambda r: (0,) * a.ndim)
    w4spec = lambda k: pl.BlockSpec((None,) + w4_lo.shape[1:],
                                    lambda r: (jnp.where((2 * r + k) * FFT_P >= seq, 0, 1), 0, 0))
    kf = pl.pallas_call(
        functools.partial(_kf_kernel, seq=seq),
        out_shape=jax.ShapeDtypeStruct((2 * seq, nc), F32),
        grid=(nblk // 2,),
        in_specs=[full(frow), full(w1), full(b1), full(w2), full(b2), full(w3), full(b3), full(fr),
                  w4spec(0), w4spec(1), full(dl)],
        out_specs=pl.BlockSpec((2 * FFT_P, nc), lambda r: (r, 0)),
        compiler_params=_cparams(("parallel",)),
        name="hyena_filter",
    )(frow, w1, b1, w2, b2, w3, b3, fr, w4_lo, w4_hi, dl)
    return pl.pallas_call(
        _spec_kernel,
        out_shape=jax.ShapeDtypeStruct((nc // cb, nblk - 1, SPEC_ROWS, cb), F32),
        grid=(nblk - 1,),
        in_specs=[pl.BlockSpec((FFT_P, nc), lambda w: (w, 0)),
                  pl.BlockSpec((FFT_P, nc), lambda w: (w + 1, 0)),
                  pl.BlockSpec(flt_mat.shape, lambda w: (0, 0))],
        out_specs=pl.BlockSpec((nc // cb, None, SPEC_ROWS, cb), lambda w: (0, w, 0, 0)),
        compiler_params=_cparams(("parallel",)),
        name="hyena_spectra",
    )(kf, kf, flt_mat)


def _hy_kernel(u_ref, x_ref, cwu_ref, cbu_ref, cwx_ref, cbx_ref, bias_ref, g_ref, fwd_ref, inv_ref,
               o_ref, u_scr, gate_scr, uf_scr, y_scr, *, n, conv_u):
    seq = n * FFT_P
    cb = o_ref.shape[1]

    def sconv(v, w_ref, b_ref):
        row = lax.broadcasted_iota(jnp.int32, v.shape, 0)
        prev = jnp.where(row == 0, 0.0, pltpu.roll(v, 1, 0))
        nxt = jnp.where(row == seq - 1, 0.0, pltpu.roll(v, seq - 1, 0))
        return b_ref[...] + prev * w_ref[0:1, :] + v * w_ref[1:2, :] + nxt * w_ref[2:3, :]

    u_scr[...] = sconv(u_ref[...], cwu_ref, cbu_ref) if conv_u else u_ref[...]
    gate_scr[...] = sconv(x_ref[...], cwx_ref, cbx_ref)
    fwd = fwd_ref[...]
    grp = 2 if (cb == LANES and n % 2 == 0) else 1
    for b0 in range(0, n, grp):
        ub = [u_scr[(b0 + k) * FFT_P:(b0 + k + 1) * FFT_P, :].astype(BF16) for k in range(grp)]
        uf = _dot(fwd, ub[0] if grp == 1 else jnp.concatenate(ub, axis=1))
        for k in range(grp):
            uf_scr[b0 + k, 0:FFT_P, :] = uf[:FFT_P, k * cb:(k + 1) * cb]
            uf_scr[b0 + k, SPEC_PITCH:SPEC_PITCH + FFT_P, :] = uf[FFT_P:, k * cb:(k + 1) * cb]

    rc = SPEC_HEAD if n == 1 else SUBLANES

    def freq_rows(r0, first):
        ure = [uf_scr[b, pl.ds(r0, rc), :] for b in range(n)]
        uim = [uf_scr[b, pl.ds(SPEC_PITCH + r0, rc), :] for b in range(n)]
        for a in range(n):
            yre = yim = None
            for b in range(n):
                w = a - b + (n - 1)
                gre = g_ref[w, pl.ds(r0, rc), :]
                gim = g_ref[w, pl.ds(SPEC_PITCH + r0, rc), :]
                gre2 = g_ref[w, 2 * SPEC_PITCH:2 * SPEC_PITCH + rc, :] if first else gre
                tre = gre * ure[b] - gim * uim[b]
                tim = gre2 * uim[b] + gim * ure[b]
                yre = tre if yre is None else yre + tre
                yim = tim if yim is None else yim + tim
            y_scr[a, pl.ds(r0, rc), :] = yre
            y_scr[a, pl.ds(FFT_P + r0, rc), :] = yim

    freq_rows(0, True)

    def rows_body(i, carry):
        freq_rows(pl.multiple_of(i * rc, rc), False)
        return carry

    lax.fori_loop(1, FFT_P // rc, rows_body, 0)

    def out_body(i, carry):
        ys = [y_scr[i * grp + k].astype(BF16) for k in range(grp)]
        y = _dot(inv_ref[...], ys[0] if grp == 1 else jnp.concatenate(ys, axis=1))
        for k in range(grp):
            rows = pl.ds(pl.multiple_of((i * grp + k) * FFT_P, FFT_P), FFT_P)
            ua = u_scr[rows, :]
            o_ref[rows, :] = gate_scr[rows, :] * (y[:, k * cb:(k + 1) * cb] + ua * bias_ref[...])
        return carry

    lax.fori_loop(0, n // grp, out_body, 0)


def _hyena_order(u_arr, u_row0, u_col0, x_arr, x_row0, x_col0, conv_w, conv_b, bias_row, spectra, spec_col0,
                 fwd, inv, nbatch, seq, cb, conv_u):
    n = seq // FFT_P
    d_hy = bias_row.shape[1]
    ncb = d_hy // cb
    nwin = spectra.shape[1]
    grid = (ncb, nbatch)
    in_specs = [
        pl.BlockSpec((None, seq, cb), lambda c, b: (u_col0 + c, u_row0 + b, 0)),
        pl.BlockSpec((None, seq, cb), lambda c, b: (x_col0 + c, x_row0 + b, 0)),
        pl.BlockSpec((3, cb), lambda c, b: (0, u_col0 + c)),
        pl.BlockSpec((1, cb), lambda c, b: (0, u_col0 + c)),
        pl.BlockSpec((3, cb), lambda c, b: (0, x_col0 + c)),
        pl.BlockSpec((1, cb), lambda c, b: (0, x_col0 + c)),
        pl.BlockSpec((1, cb), lambda c, b: (0, c)),
        pl.BlockSpec((None, nwin, SPEC_ROWS, cb), lambda c, b: (spec_col0 + c, 0, 0, 0)),
        pl.BlockSpec(fwd.shape, lambda c, b: (0, 0)),
        pl.BlockSpec(inv.shape, lambda c, b: (0, 0)),
    ]
    return pl.pallas_call(
        functools.partial(_hy_kernel, n=n, conv_u=conv_u),
        out_shape=jax.ShapeDtypeStruct((ncb, nbatch * seq, cb), F32),
        grid=grid,
        in_specs=in_specs,
        out_specs=pl.BlockSpec((None, seq, cb), lambda c, b: (c, b, 0)),
        scratch_shapes=[pltpu.VMEM((seq, cb), F32), pltpu.VMEM((seq, cb), F32),
                        pltpu.VMEM((n, 2 * SPEC_PITCH, cb), F32), pltpu.VMEM((n, FFT_N, cb), F32)],
        compiler_params=_cparams(("parallel", "parallel")),
        name="hyena_conv",
    )(u_arr, x_arr, conv_w, conv_b, conv_w, conv_b, bias_row, spectra, fwd, inv)


def _hyena_mix(hy_in, conv_w, conv_b, bias, spectra, fwd, inv, row0, nbatch, seq, cb):
    ncb = bias.shape[1] // cb
    z1 = _hyena_order(hy_in, row0, 0, hy_in, row0, ncb, conv_w, conv_b, bias[0:1], spectra, 0,
                      fwd, inv, nbatch, seq, cb, True)
    return _hyena_order(z1, 0, 0, hy_in, row0, 2 * ncb, conv_w, conv_b, bias[1:2], spectra, ncb,
                        fwd, inv, nbatch, seq, cb, False)


def _ret_kernel(q_ref, k_ref, v_ref, g_ref, cos_ref, sin_ref, df_ref, db_ref, ng_ref, s0f_ref, s0b_ref,
                y_ref, sf_ref, sb_ref, q_scr, k_scr, sb_scr, *, n, rope, has_state):
    c = q_ref.shape[0] // n
    dk = cos_ref.shape[1]
    hps = df_ref.shape[0]
    ri = lax.broadcasted_iota(jnp.int32, (c, c), 0)
    ci = lax.broadcasted_iota(jnp.int32, (c, c), 1)
    diff = (ri - ci).astype(F32)
    pos = lax.broadcasted_iota(jnp.int32, (c, dk), 0).astype(F32)

    def decay_terms(hh):
        lgf = -jnp.exp(df_ref[hh])
        lgb = -jnp.exp(db_ref[hh])
        mask = (jnp.where(diff >= 0, jnp.exp(jnp.maximum(diff, 0.0) * lgf[:, :1]), 0.0)
                + jnp.where(diff <= 0, jnp.exp(jnp.maximum(-diff, 0.0) * lgb[:, :1]), 0.0))
        return dict(mask=mask,
                    xi_f=jnp.exp((pos + 1.0) * lgf), ze_f=jnp.exp((c - 1.0 - pos) * lgf), cd_f=jnp.exp(c * lgf),
                    xi_b=jnp.exp((c - pos) * lgb), ze_b=jnp.exp(pos * lgb), cd_b=jnp.exp(c * lgb))

    dec = [decay_terms(hh) for hh in range(hps)]
    cols = lambda hh: slice(hh * dk, (hh + 1) * dk)

    for hh in range(hps):
        q = q_ref[:, cols(hh)]
        k = k_ref[:, cols(hh)]
        if rope:
            cs = cos_ref[...]
            sn = sin_ref[...]
            q = q * cs + pltpu.roll(q, dk // 2, 1) * sn
            k = k * cs + pltpu.roll(k, dk // 2, 1) * sn
        q_scr[:, cols(hh)] = q
        k_scr[:, cols(hh)] = k * (dk ** -0.5)
        sf_ref[hh] = s0f_ref[hh] if has_state else jnp.zeros((dk, dk), F32)
        sb_ref[hh] = s0b_ref[hh] if has_state else jnp.zeros((dk, dk), F32)

    def dot_tn(a, b):
        return lax.dot_general(a, b, (((0,), (0,)), ((), ())), preferred_element_type=F32)

    def chunk(i):
        return pl.ds(pl.multiple_of(i * c, c), c)

    def bwd_body(j, carry):
        i = n - 1 - j
        rows = chunk(i)
        for hh in range(hps):
            s = sb_ref[hh]
            sb_scr[hh, i] = s
            kz = (k_scr[rows, cols(hh)] * dec[hh]["ze_b"]).astype(BF16)
            sb_ref[hh] = s * dec[hh]["cd_b"] + dot_tn(kz, v_ref[rows, cols(hh)].astype(BF16))
        return carry

    unroll = math.gcd(n, 8)
    lax.fori_loop(0, n, bwd_body, 0, unroll=unroll)

    def fwd_body(i, carry):
        rows = chunk(i)
        for hh in range(hps):
            d = dec[hh]
            s = sf_ref[hh]
            qb = q_scr[rows, cols(hh)].astype(BF16)
            kc = k_scr[rows, cols(hh)]
            vb = v_ref[rows, cols(hh)].astype(BF16)
            sc = lax.dot_general(qb, kc.astype(BF16), (((1,), (1,)), ((), ())), preferred_element_type=F32)
            o = _dot((sc * d["mask"]).astype(BF16), vb)
            o = o + _dot(qb, s.astype(BF16)) * d["xi_f"] + _dot(qb, sb_scr[hh, i].astype(BF16)) * d["xi_b"]
            gt = g_ref[rows, cols(hh)]
            y_ref[rows, cols(hh)] = _rms(o, ng_ref[hh]) * (gt * jax.nn.sigmoid(gt))
            sf_ref[hh] = s * d["cd_f"] + dot_tn((kc * d["ze_f"]).astype(BF16), vb)
        return carry

    lax.fori_loop(0, n, fwd_body, 0, unroll=unroll)


def _retention(q, k, v, g, cos, sin, dec_f, dec_b, norm_g, s0f, s0b, nbatch, seq, rope, has_state, hps):
    nh = N_RET_HEADS
    dk = q.shape[1] // nh
    n = seq // (RET_CHUNK_LONG if seq > 2 * RET_CHUNK_LONG else RET_CHUNK)
    tspec = pl.BlockSpec((seq, hps * dk), lambda b, h: (b, h))
    rspec = pl.BlockSpec((seq, dk), lambda b, h: (0, 0))
    hspec = pl.BlockSpec((hps, 1, dk), lambda b, h: (h, 0, 0))
    sspec = pl.BlockSpec((None, None, hps, dk, dk), lambda b, h: (b, 0, h, 0, 0))
    s0spec = sspec if has_state else pl.BlockSpec((None, None, hps, dk, dk), lambda b, h: (0, 0, h, 0, 0))
    st_shape = jax.ShapeDtypeStruct((nbatch, 1, nh, dk, dk), F32)
    return pl.pallas_call(
        functools.partial(_ret_kernel, n=n, rope=rope, has_state=has_state),
        out_shape=[jax.ShapeDtypeStruct((nbatch * seq, nh * dk), F32), st_shape, st_shape],
        grid=(nbatch, nh // hps),
        in_specs=[tspec, tspec, tspec, tspec, rspec, rspec, hspec, hspec, hspec, s0spec, s0spec],
        out_specs=[tspec, sspec, sspec],
        scratch_shapes=[pltpu.VMEM((seq, hps * dk), F32), pltpu.VMEM((seq, hps * dk), F32),
                        pltpu.VMEM((hps, n, dk, dk), F32)],
        compiler_params=_cparams(("parallel", "parallel")),
        name="retention",
    )(q, k, v, g, cos, sin, dec_f, dec_b, norm_g, s0f, s0b)


def _outproj_kernel(yh_ref, yr_ref, x_ref, gt_ref, sh_ref, sc_ref, hg_ref, ng_ref, wo_ref, x1_ref, h2t_ref):
    ncb = yh_ref.shape[0]
    yh = yh_ref[0] if ncb == 1 else jnp.concatenate([yh_ref[c] for c in range(ncb)], axis=1)
    d_hy = yh.shape[1]
    nh = _rms(yh, hg_ref[...]).astype(BF16)
    y = _dot(nh, wo_ref[0:d_hy, :]) + _dot(yr_ref[...].astype(BF16), wo_ref[d_hy:, :])
    x1 = x_ref[...] + gt_ref[...] * y
    x1_ref[...] = x1
    h2 = _rms(x1, ng_ref[...]) * (1.0 + sc_ref[...]) + sh_ref[...]
    h2t_ref[...] = h2.T.astype(BF16)


def _outproj(y_hy, y_ret, x, gt1, sh2, sc2, hy_g, n2_g, wo_bf, grp, tm):
    t, d = x.shape
    row = lambda i: (i, 0)
    modspec = pl.BlockSpec((None, 1, d), lambda i: (grp(i), 0, 0))
    return pl.pallas_call(
        _outproj_kernel,
        out_shape=[jax.ShapeDtypeStruct((t, d), F32), jax.ShapeDtypeStruct((d, t), BF16)],
        grid=(t // tm,),
        in_specs=[pl.BlockSpec((y_hy.shape[0], tm, y_hy.shape[2]), lambda i: (0, i, 0)),
                  pl.BlockSpec((tm, y_ret.shape[1]), row),
                  pl.BlockSpec((tm, d), row), modspec, modspec, modspec,
                  pl.BlockSpec(hy_g.shape, lambda i: (0, 0)),
                  pl.BlockSpec((1, d), lambda i: (0, 0)),
                  pl.BlockSpec(wo_bf.shape, lambda i: (0, 0))],
        out_specs=[pl.BlockSpec((tm, d), row), pl.BlockSpec((d, tm), lambda i: (0, i))],
        compiler_params=_cparams(("parallel",)),
        name="outproj",
    )(y_hy, y_ret, x, gt1, sh2, sc2, hy_g, n2_g, wo_bf)


def _sort_desc(xs):
    xs = list(xs)
    n = len(xs)
    k = 2
    while k <= n:
        j = k // 2
        while j >= 1:
            for i in range(n):
                l = i ^ j
                if l > i:
                    hi, lo = jnp.maximum(xs[i], xs[l]), jnp.minimum(xs[i], xs[l])
                    xs[i], xs[l] = (hi, lo) if (i & k) == 0 else (lo, hi)
            j //= 2
        k *= 2
    return xs


def _merge_top(a, b):
    n = len(a)
    xs = [jnp.maximum(a[i], b[n - 1 - i]) for i in range(n)]
    j = n // 2
    while j >= 1:
        for i in range(n):
            l = i ^ j
            if l > i:
                xs[i], xs[l] = jnp.maximum(xs[i], xs[l]), jnp.minimum(xs[i], xs[l])
        j //= 2
    return xs


def _top_sorted(load, count, k):
    acc = None
    for g0 in range(0, count, k):
        grp = _sort_desc([load(i) for i in range(g0, g0 + k)])
        acc = grp if acc is None else _merge_top(acc, grp)
    return acc


def _staircase(k):
    return [(a, b) for a in range(k) for b in range(k) if (a + 1) * (b + 1) <= k + 1]


def _peer1_kernel(h2t_ref, wq_ref, keys_ref, s2_ref, e2_ref, low_ref, e1_ref, q_scr, s1_scr, s2_scr):
    h = pl.program_id(1)
    lt_n = s2_ref.shape[0]
    nk = N_KEYS
    kk = PEER_TOPK

    @pl.when(h == 0)
    def _():
        q_scr[...] = _dot(wq_ref[...], h2t_ref[...]).astype(BF16)

    dq = q_scr.shape[0] // (2 * PEER_HEADS)
    for p, scr in ((0, s1_scr), (1, s2_scr)):
        rows = pl.ds(pl.multiple_of((2 * h + p) * dq, dq), dq)
        s = _dot(keys_ref[2 * h + p], q_scr[rows, :])
        for lt in range(lt_n):
            scr[lt * TILE_PITCH:lt * TILE_PITCH + nk, :] = s[:, lt * LANES:(lt + 1) * LANES]

    ld1 = lambda i: s1_scr[pl.ds(i, lt_n, stride=TILE_PITCH), :]
    ld2 = lambda i: s2_scr[pl.ds(i, lt_n, stride=TILE_PITCH), :]
    v1 = _top_sorted(ld1, nk, kk)
    v2 = _top_sorted(ld2, nk, kk)
    pairs = _staircase(kk)
    cand = [v1[a] + v2[b] for a, b in pairs]
    neg = jnp.full_like(cand[0], -jnp.inf)
    padded = cand + [neg] * (-len(cand) % kk)
    top = _top_sorted(lambda i: padded[i], len(padded), kk)
    tau = top[kk - 1]
    nxt = neg
    for cnd in cand:
        nxt = jnp.maximum(nxt, jnp.where(cnd < tau, cnd, neg))
    thr = 0.5 * (tau + nxt)
    m = v1[0] + v2[0]
    z = jnp.zeros_like(tau)
    for cnd in cand:
        z = z + jnp.where(cnd >= tau, jnp.exp(cnd - m), 0.0)
    inv_z = 1.0 / z

    for i in range(nk):
        x1 = ld1(i)
        gap = thr - x1
        low = jnp.full_like(x1, jnp.inf)
        for b in range(kk):
            low = jnp.where(v2[b] > gap, v2[b], low)
        keep = x1 >= v1[kk - 1]
        low_ref[i] = jnp.where(keep, low, jnp.inf)
        e1_ref[i] = jnp.where(keep, jnp.exp(x1 - v1[0]) * inv_z, 0.0)
    for lt in range(lt_n):
        s2 = s2_scr[lt * TILE_PITCH:lt * TILE_PITCH + nk, :]
        s2_ref[lt] = s2
        e2_ref[lt] = jnp.where(s2 >= v2[kk - 1][lt:lt + 1, :], jnp.exp(s2 - v2[0][lt:lt + 1, :]), 0.0).astype(BF16)


def _peer1(h2t, wqt_bf, keys_bf):
    d, t = h2t.shape
    nh = PEER_HEADS
    tb = TOK_VREG
    lt_n = tb // LANES
    tile_shape = lambda dt: jax.ShapeDtypeStruct((nh, t // LANES, N_KEYS, LANES), dt)
    vreg_shape = jax.ShapeDtypeStruct((nh, t // tb, N_KEYS, lt_n, LANES), F32)
    tile_spec = pl.BlockSpec((None, lt_n, N_KEYS, LANES), lambda i, h: (h, i, 0, 0))
    vreg_spec = pl.BlockSpec((None, None, N_KEYS, lt_n, LANES), lambda i, h: (h, i, 0, 0, 0))
    return pl.pallas_call(
        _peer1_kernel,
        out_shape=[tile_shape(F32), tile_shape(BF16), vreg_shape, vreg_shape],
        grid=(t // tb, nh),
        in_specs=[pl.BlockSpec((d, tb), lambda i, h: (0, i)),
                  pl.BlockSpec(wqt_bf.shape, lambda i, h: (0, 0)),
                  pl.BlockSpec(keys_bf.shape, lambda i, h: (0, 0, 0))],
        out_specs=[tile_spec, tile_spec, vreg_spec, vreg_spec],
        scratch_shapes=[pltpu.VMEM((wqt_bf.shape[0], tb), BF16),
                        pltpu.VMEM((lt_n * TILE_PITCH, LANES), F32),
                        pltpu.VMEM((lt_n * TILE_PITCH, LANES), F32)],
        compiler_params=_cparams(("parallel", "arbitrary")),
        name="peer_select",
    )(h2t, wqt_bf, keys_bf)


def _gelu_tanh(x):
    k = -2.0 * math.sqrt(2.0 / math.pi) * math.log2(math.e)
    e = jnp.exp2(x * (x * x * (k * 0.044715) + k))
    return x / (1.0 + e)


GATE_ROWS = 2 * SUBLANES
SEL_PITCH = N_KEYS + SUBLANES


def _peer2_kernel(h2t_in, u_ref, vt_ref, s2_in, e2_in, low_ref, e1_ref, x1_ref, gt_ref, fg_ref, y_ref,
                  at_scr, zt_scr, sel_ref, h2t_ref, ot_ref):
    eb = pl.program_id(1)
    lt_n = s2_in.shape[1]
    ig_n = low_ref.shape[1]

    @pl.when(eb == 0)
    def _():
        ot_ref[...] = jnp.zeros_like(ot_ref)
        for hh in range(PEER_HEADS):
            sel_ref[:, 2 * hh, 0:N_KEYS, :] = s2_in[hh]
            sel_ref[:, 2 * hh + 1, 0:N_KEYS, :] = e2_in[hh].astype(F32)
        h2t_ref[...] = h2t_in[...]

    tcw = at_scr.shape[2]
    ntc = lt_n * LANES // tcw
    lt_per = tcw // LANES

    def row(ref, hh, ig, lt):
        return jnp.broadcast_to(ref[hh, ig, lt:lt + 1, :], (GATE_ROWS, LANES))

    def scores(tc):
        at_scr[tc % 2] = _dot(u_ref[...], h2t_ref[:, tc * tcw:(tc + 1) * tcw])

    def gates(tc):
        for ig in range(ig_n):
            for l in range(lt_per):
                lt = tc * lt_per + l
                low = [row(low_ref, hh, ig, lt) for hh in range(PEER_HEADS)]
                e1 = [row(e1_ref, hh, ig, lt) for hh in range(PEER_HEADS)]
                for c in range(N_KEYS // GATE_ROWS):
                    js = slice(c * GATE_ROWS, (c + 1) * GATE_ROWS)
                    w = jnp.zeros((GATE_ROWS, LANES), F32)
                    for hh in range(PEER_HEADS):
                        hit = sel_ref[lt, 2 * hh, js, :] >= low[hh]
                        w = w + jnp.where(hit, sel_ref[lt, 2 * hh + 1, js, :], 0.0) * e1[hh]
                    rows = slice(ig * N_KEYS + c * GATE_ROWS, ig * N_KEYS + (c + 1) * GATE_ROWS)
                    a = at_scr[tc % 2, rows, l * LANES:(l + 1) * LANES]
                    zt_scr[rows, lt * LANES:(lt + 1) * LANES] = (w * _gelu_tanh(a)).astype(BF16)

    def combine(tc):
        cols = slice(tc * tcw, (tc + 1) * tcw)
        ot_ref[:, cols] += _dot(vt_ref[...], zt_scr[:, cols])

    scores(0)
    for tc in range(ntc):
        if tc + 1 < ntc:
            scores(tc + 1)
        gates(tc)
        combine(tc)

    @pl.when(eb == pl.num_programs(1) - 1)
    def _():
        y_ref[...] = _rms(x1_ref[...] + gt_ref[...] * ot_ref[...].T, fg_ref[...])


def _peer2(h2t, u_bf, vt_bf, s2, e2, low, e1, x1, gt2, final_g, grp, eblk):
    d, t = h2t.shape
    ne = u_bf.shape[0]
    nh = PEER_HEADS
    tb = TOK_VREG
    lt_n = tb // LANES
    ig_n = eblk // N_KEYS
    tile_spec = pl.BlockSpec((nh, lt_n, N_KEYS, LANES), lambda i, e: (0, i, 0, 0))
    vreg_spec = pl.BlockSpec((nh, None, ig_n, lt_n, LANES), lambda i, e: (0, i, e, 0, 0))
    return pl.pallas_call(
        _peer2_kernel,
        out_shape=jax.ShapeDtypeStruct((t, d), F32),
        grid=(t // tb, ne // eblk),
        in_specs=[pl.BlockSpec((d, tb), lambda i, e: (0, i)),
                  pl.BlockSpec((eblk, d), lambda i, e: (e, 0)),
                  pl.BlockSpec((d, eblk), lambda i, e: (0, e)),
                  pl.BlockSpec((nh, lt_n, N_KEYS, LANES), lambda i, e: (0, i, 0, 0), pipeline_mode=pl.Buffered(1)),
                  tile_spec, vreg_spec, vreg_spec,
                  pl.BlockSpec((tb, d), lambda i, e: (i, 0), pipeline_mode=pl.Buffered(1)),
                  pl.BlockSpec((None, 1, d), lambda i, e: (grp(i), 0, 0)),
                  pl.BlockSpec((1, d), lambda i, e: (0, 0))],
        out_specs=pl.BlockSpec((tb, d), lambda i, e: (i, 0)),
        scratch_shapes=[pltpu.VMEM((2, eblk, PEER_TOKEN_CHUNK), F32), pltpu.VMEM((eblk, tb), BF16),
                        pltpu.VMEM((lt_n, 2 * nh, SEL_PITCH, LANES), F32), pltpu.VMEM((d, tb), BF16),
                        pltpu.VMEM((d, tb), F32)],
        compiler_params=_cparams(("parallel", "arbitrary")),
        name="peer_experts",
    )(h2t, u_bf, vt_bf, s2, e2, low, e1, x1, gt2, final_g)


def _rope_tables(seq, dk):
    rows = seq // GRID_W
    r, c = jnp.meshgrid(jnp.arange(rows, dtype=F32), jnp.arange(GRID_W, dtype=F32), indexing='ij')
    r = r.reshape(-1)
    c = c.reshape(-1)
    nf = dk // 4
    inv = ROPE_BASE ** (-jnp.arange(nf, dtype=F32) / nf)
    ang = jnp.concatenate([r[:, None] * inv, c[:, None] * inv], axis=-1)
    cos = jnp.concatenate([jnp.cos(ang), jnp.cos(ang)], axis=-1)
    sin = jnp.concatenate([-jnp.sin(ang), jnp.sin(ang)], axis=-1)
    return cos, sin


def _pad2(a, rows, cols):
    return jnp.pad(a, ((0, rows - a.shape[0]), (0, cols - a.shape[1])))


def kernel(x_prompt, x_sample, state_ret_fwd, state_ret_bwd, c, c_ctx, w_ada, b_ada, norm1_g, w_in, hy_conv_w, hy_conv_b, hy_w1, hy_b1, hy_w2, hy_b2, hy_w3, hy_b3, hy_w4, hy_freq, hy_bias, hy_norm_g, ret_decay_fwd, ret_decay_bwd, ret_norm_g, w_out, norm2_g, peer_wq, peer_keys, peer_u, peer_v, final_g):
    b_ctx, l_ctx, d = x_prompt.shape
    b_lat, l_lat, _ = x_sample.shape
    assert w_ada.shape[0] == 1, "one trunk layer"
    t_ctx, t_lat = b_ctx * l_ctx, b_lat * l_lat
    d_hy = hy_norm_g.shape[1]
    d_ret = ret_norm_g.shape[1]
    dk = d_ret // N_RET_HEADS
    tm = TOKEN_BLOCK
    assert l_ctx % FFT_P == 0 and l_lat % TOK_VREG == 0 and t_ctx % TOK_VREG == 0 and b_lat < SUBLANES

    ngrp = 1 + b_lat
    cond = jnp.concatenate([c_ctx[None, :], c], axis=0)
    cond8 = jnp.pad(cond, ((0, SUBLANES - ngrp), (0, 0)))
    mod = _ada(cond8, w_ada[0], b_ada[0])
    sh1, sc1, gt1, sh2, sc2, gt2 = [m.reshape(SUBLANES, 1, d) for m in jnp.split(mod, 6, axis=-1)]

    fw = hy_w1.shape[-1]
    hl = LANES // 2
    assert fw <= hl and HY_EMB <= hl
    f = jnp.linspace(1e-4, HY_BANDS - 1, HY_BANDS, dtype=F32)
    frow = jnp.zeros((1, hl), F32).at[0, 1:1 + HY_BANDS].set(f).at[0, 1 + HY_BANDS:1 + 2 * HY_BANDS].set(f)
    min_decay = math.log(HY_DECAY_TARGET) / HY_SLOW_PCT
    max_decay = math.log(HY_DECAY_TARGET) / HY_FAST_PCT
    deltas = jnp.abs(jnp.linspace(min_decay, max_decay, d_hy, dtype=F32))
    twice = lambda a: jnp.tile(_pad2(a, 1, hl), (1, 2))
    diag2 = lambda a: jnp.kron(jnp.eye(2, dtype=F32), _pad2(a, hl, hl))
    w4 = hy_w4[0].reshape(fw, 2, 2 * d_hy).transpose(1, 0, 2)
    w4_lo = jnp.pad(w4, ((0, 0), (0, LANES - fw), (0, 0)))
    w4_hi = jnp.pad(w4, ((0, 0), (hl, hl - fw), (0, 0)))
    filt_w = (twice(frow), diag2(hy_w1[0]), twice(hy_b1), diag2(hy_w2[0]), twice(hy_b2), diag2(hy_w3[0]),
              twice(hy_b3), twice(hy_freq), w4_lo, w4_hi, jnp.tile(deltas, 2)[None, :])
    fwd_m, inv_m, flt_m = _dft_mats()
    cos_t, sin_t = _rope_tables(l_lat, dk)
    dec_f = jnp.broadcast_to(ret_decay_fwd[0][:, None, None], (N_RET_HEADS, 1, dk))
    dec_b = jnp.broadcast_to(ret_decay_bwd[0][:, None, None], (N_RET_HEADS, 1, dk))
    ng = ret_norm_g[0].reshape(N_RET_HEADS, 1, dk)
    w_in_bf = w_in[0].astype(BF16)
    w_out_bf = w_out[0].astype(BF16)
    wqt = peer_wq[0].T.astype(BF16)
    keys = peer_keys[0].reshape(2 * PEER_HEADS, N_KEYS, -1).astype(BF16)
    u_bf = peer_u[0].astype(BF16)
    vt_bf = peer_v[0].T.astype(BF16)
    fg = final_g.reshape(1, d)

    def trunk(x, nbatch, seq, latent):
        hy_cb, ret_hps = _path_blocks(seq, d_hy)
        grp = (lambda i: 1 + i // (seq // tm)) if latent else (lambda i: 0)
        hy_in, rq, rk, rv, rg = _inproj(x, sh1, sc1, norm1_g, w_in_bf, grp, tm, 3 * d_hy, d_ret, hy_cb)
        spectra = _hyena_spectra(seq, filt_w, flt_m, hy_cb)
        y_hy = _hyena_mix(hy_in, hy_conv_w[0], hy_conv_b, hy_bias[0], spectra, fwd_m, inv_m, 0, nbatch, seq, hy_cb)
        y_ret, s_f, s_b = _retention(rq, rk, rv, rg, cos_t, sin_t, dec_f, dec_b, ng, state_ret_fwd, state_ret_bwd,
                                     nbatch, seq, latent, latent, ret_hps)
        x1, h2t = _outproj(y_hy, y_ret, x, gt1, sh2, sc2, hy_norm_g, norm2_g, w_out_bf, grp, tm)
        s2, e2, low, e1 = _peer1(h2t, wqt, keys)
        grp_tok = (lambda i: 1 + i // (seq // TOK_VREG)) if latent else (lambda i: 0)
        y = _peer2(h2t, u_bf, vt_bf, s2, e2, low, e1, x1, gt2, fg, grp_tok, PEER_EXPERT_BLOCK)
        return y.reshape(nbatch, seq, d), s_f, s_b

    y_prompt, sf_new, sb_new = trunk(x_prompt.reshape(t_ctx, d), b_ctx, l_ctx, False)
    y_sample, _, _ = trunk(x_sample.reshape(t_lat, d), b_lat, l_lat, True)
    return (y_prompt, y_sample, sf_new, sb_new)
```

```python
import functools
import math

import numpy as np
import jax
import jax.numpy as jnp
from jax import lax
from jax.experimental import pallas as pl
from jax.experimental.pallas import tpu as pltpu

F32 = jnp.float32
BF16 = jnp.bfloat16

LANES = 128
SUBLANES = 8
VMEM_LIMIT = 56 * 1024 * 1024

EPS = 1e-6
GRID_W = 64
N_RET_HEADS = 4
RET_CHUNK = 128
RET_CHUNK_LONG = 256
ROPE_BASE = 10000.0
HY_EMB = 33
HY_BANDS = (HY_EMB - 1) // 2
HY_DECAY_TARGET = 1e-2
HY_FAST_PCT = 0.3
HY_SLOW_PCT = 1.5
N_KEYS = 128
PEER_HEADS = 8
PEER_TOPK = 16
FFT_P = 256
FFT_N = 2 * FFT_P
SPEC_PITCH = FFT_P + SUBLANES
SPEC_HEAD = 2 * SUBLANES
SPEC_ROWS = 2 * SPEC_PITCH + SPEC_HEAD
TOK_VREG = SUBLANES * LANES
TILE_PITCH = N_KEYS + SUBLANES

TOKEN_BLOCK = 512
ADA_COL_BLOCK = 1536
PEER_EXPERT_BLOCK = 1024
PEER_TOKEN_CHUNK = 2 * LANES


def _path_blocks(seq, d_hy):
    return (d_hy, N_RET_HEADS) if seq == FFT_P else (LANES, 1)


def _cparams(sem):
    return pltpu.CompilerParams(dimension_semantics=sem, vmem_limit_bytes=VMEM_LIMIT)


def _dot(a, b):
    return jnp.dot(a, b, preferred_element_type=F32)


def _split(a):
    hi = a.astype(BF16)
    lo = (a - hi.astype(F32)).astype(BF16)
    return hi, lo


def _dot3(a, b):
    ah, al = _split(a)
    bh, bl = _split(b)
    return _dot(ah, bh) + (_dot(al, bh) + _dot(ah, bl))


def _rms(x, g):
    return x * lax.rsqrt(jnp.mean(x * x, axis=-1, keepdims=True) + EPS) * g


def _ada_kernel(c_ref, w_ref, b_ref, o_ref):
    c = c_ref[...]
    s = c * jax.nn.sigmoid(c)
    o_ref[...] = _dot3(s, w_ref[...]) + b_ref[...]


def _ada(cond8, w_ada, b_ada):
    d, n = w_ada.shape
    tn = ADA_COL_BLOCK
    return pl.pallas_call(
        _ada_kernel,
        out_shape=jax.ShapeDtypeStruct((cond8.shape[0], n), F32),
        grid=(n // tn,),
        in_specs=[pl.BlockSpec((cond8.shape[0], d), lambda j: (0, 0)),
                  pl.BlockSpec((d, tn), lambda j: (0, j)),
                  pl.BlockSpec((1, tn), lambda j: (0, j))],
        out_specs=pl.BlockSpec((cond8.shape[0], tn), lambda j: (0, j)),
        compiler_params=_cparams(("parallel",)),
        name="ada",
    )(cond8, w_ada, b_ada.reshape(1, n))


def _inproj_kernel(x_ref, sh_ref, sc_ref, g_ref, w_ref, hy_ref, q_ref, k_ref, v_ref, gg_ref, *, d_hy3, d_ret):
    h = _rms(x_ref[...], g_ref[...]) * (1.0 + sc_ref[...]) + sh_ref[...]
    hb = h.astype(BF16)
    hy = _dot(hb, w_ref[:, 0:d_hy3])
    cb = hy_ref.shape[2]
    for c in range(hy_ref.shape[0]):
        hy_ref[c] = hy[:, c * cb:(c + 1) * cb]
    for i, o_ref in enumerate((q_ref, k_ref, v_ref, gg_ref)):
        o_ref[...] = _dot(hb, w_ref[:, d_hy3 + i * d_ret:d_hy3 + (i + 1) * d_ret])


def _inproj(x, sh, sc, g, w_bf, grp, tm, d_hy3, d_ret, hy_cb):
    t, d = x.shape
    row = lambda i: (i, 0)
    modspec = pl.BlockSpec((None, 1, d), lambda i: (grp(i), 0, 0))
    outs = [jax.ShapeDtypeStruct((d_hy3 // hy_cb, t, hy_cb), F32)] + [jax.ShapeDtypeStruct((t, d_ret), F32)] * 4
    return pl.pallas_call(
        functools.partial(_inproj_kernel, d_hy3=d_hy3, d_ret=d_ret),
        out_shape=outs,
        grid=(t // tm,),
        in_specs=[pl.BlockSpec((tm, d), row), modspec, modspec,
                  pl.BlockSpec((1, d), lambda i: (0, 0)),
                  pl.BlockSpec(w_bf.shape, lambda i: (0, 0))],
        out_specs=[pl.BlockSpec((d_hy3 // hy_cb, tm, hy_cb), lambda i: (0, i, 0))]
        + [pl.BlockSpec((tm, d_ret), row)] * 4,
        compiler_params=_cparams(("parallel",)),
        name="inproj",
    )(x, sh, sc, g, w_bf)


def _kf_kernel(frow_ref, w1_ref, b1_ref, w2_ref, b2_ref, w3_ref, b3_ref, fr_ref, w4a_ref, w4b_ref, dl_ref, o_ref,
               *, seq):
    r = pl.program_id(0)
    half = o_ref.shape[0] // 2
    lane = lax.broadcasted_iota(jnp.int32, (half, LANES), 1)
    upper = lane >= LANES // 2
    i = r * 2 * half + lax.broadcasted_iota(jnp.int32, (half, LANES), 0) + jnp.where(upper, half, 0)
    pos = jnp.abs(i - seq).astype(F32)
    t = pos / (seq - 1.0)
    ang = frow_ref[...] * ((2.0 * math.pi) * pos / seq)
    feat = jnp.where(upper, lane - LANES // 2, lane)
    z = jnp.where(feat == 0, t,
                  jnp.where(feat <= HY_BANDS, jnp.cos(ang),
                            jnp.where(feat <= 2 * HY_BANDS, -jnp.sin(ang), 0.0)))
    fr = fr_ref[...]
    h = jnp.sin(fr * (_dot3(z, w1_ref[...]) + b1_ref[...]))
    h = jnp.sin(fr * (_dot3(h, w2_ref[...]) + b2_ref[...]))
    h = jnp.sin(fr * (_dot3(h, w3_ref[...]) + b3_ref[...]))
    nc = o_ref.shape[1]
    for k, w4_ref in enumerate((w4a_ref, w4b_ref)):
        h4 = _dot3(h, w4_ref[...])
        iw = r * 2 * half + k * half + lax.broadcasted_iota(jnp.int32, (half, nc), 0)
        tw = jnp.abs(iw - seq).astype(F32) / (seq - 1.0)
        win = jnp.exp(-tw * dl_ref[...])
        o_ref[k * half:(k + 1) * half, :] = jnp.where(iw == 0, 0.0, h4 * win)


def _spec_kernel(k0_ref, k1_ref, a_ref, o_ref):
    a = a_ref[...]
    s = (_dot(a[:, :FFT_P], k0_ref[...].astype(BF16))
         + _dot(a[:, FFT_P:], k1_ref[...].astype(BF16)))
    cb = o_ref.shape[2]
    for c in range(o_ref.shape[0]):
        cols = slice(c * cb, (c + 1) * cb)
        for p in range(2):
            r0 = p * SPEC_PITCH
            o_ref[c, r0:r0 + FFT_P, :] = s[p * FFT_P:(p + 1) * FFT_P, cols]
            o_ref[c, r0 + FFT_P:r0 + SPEC_PITCH, :] = jnp.zeros((SPEC_PITCH - FFT_P, cb), F32)
        o_ref[c, 2 * SPEC_PITCH:SPEC_ROWS, :] = s[2 * FFT_P:2 * FFT_P + SPEC_HEAD, cols]


def _dft_mats():
    f = np.arange(FFT_P)[:, None].astype(np.float64)
    t = np.arange(FFT_P)[None, :].astype(np.float64)
    j = np.arange(FFT_N)[None, :].astype(np.float64)
    w = 2.0 * np.pi / FFT_N
    fwd = np.concatenate([np.cos(w * f * t), -np.sin(w * f * t)], axis=0)
    fwd[FFT_P] = np.cos(np.pi * t[0])
    scale = np.full((1, FFT_P), 2.0 / FFT_N)
    scale[0, 0] = 1.0 / FFT_N
    inv = np.concatenate([np.cos(w * t.T * f.T) * scale, -np.sin(w * t.T * f.T) * scale], axis=1)
    inv[:, FFT_P] = np.cos(np.pi * t[0]) / FFT_N
    sgn = np.where(np.arange(FFT_P) % 2 == 0, 1.0, -1.0)[:, None]
    re = sgn * np.cos(w * f * j)
    im = -sgn * np.sin(w * f * j)
    nyq = np.cos(np.pi * j[0])
    re2 = re.copy()
    re2[0] = nyq
    im[0] = 0.0
    flt = np.concatenate([re, im, re2[:SPEC_HEAD]], axis=0)
    flt[:, 0] = 0.0
    return (jnp.asarray(fwd, F32).astype(BF16), jnp.asarray(inv, F32).astype(BF16),
            jnp.asarray(flt, F32).astype(BF16))


def _hyena_spectra(seq, fw, flt_mat, cb):
    frow, w1, b1, w2, b2, w3, b3, fr, w4_lo, w4_hi, dl = fw
    nblk = 2 * seq // FFT_P
    nc = w4_lo.shape[-1]
    full = lambda a: pl.BlockSpec(a.shape, lambda r: (0,) * a.ndim)
    w4spec = lambda k: pl.BlockSpec((None,) + w4_lo.shape[1:],
                                    lambda r: (jnp.where((2 * r + k) * FFT_P >= seq, 0, 1), 0, 0))
    kf = pl.pallas_call(
        functools.partial(_kf_kernel, seq=seq),
        out_shape=jax.ShapeDtypeStruct((2 * seq, nc), F32),
        grid=(nblk // 2,),
        in_specs=[full(frow), full(w1), full(b1), full(w2), full(b2), full(w3), full(b3), full(fr),
                  w4spec(0), w4spec(1), full(dl)],
        out_specs=pl.BlockSpec((2 * FFT_P, nc), lambda r: (r, 0)),
        compiler_params=_cparams(("parallel",)),
        name="hyena_filter",
    )(frow, w1, b1, w2, b2, w3, b3, fr, w4_lo, w4_hi, dl)
    return pl.pallas_call(
        _spec_kernel,
        out_shape=jax.ShapeDtypeStruct((nc // cb, nblk - 1, SPEC_ROWS, cb), F32),
        grid=(nblk - 1,),
        in_specs=[pl.BlockSpec((FFT_P, nc), lambda w: (w, 0)),
                  pl.BlockSpec((FFT_P, nc), lambda w: (w + 1, 0)),
                  pl.BlockSpec(flt_mat.shape, lambda w: (0, 0))],
        out_specs=pl.BlockSpec((nc // cb, None, SPEC_ROWS, cb), lambda w: (0, w, 0, 0)),
        compiler_params=_cparams(("parallel",)),
        name="hyena_spectra",
    )(kf, kf, flt_mat)


def _hy_kernel(u_ref, x_ref, cwu_ref, cbu_ref, cwx_ref, cbx_ref, bias_ref, g_ref, fwd_ref, inv_ref,
               o_ref, u_scr, gate_scr, uf_scr, y_scr, *, n, conv_u):
    seq = n * FFT_P
    cb = o_ref.shape[1]

    def sconv(v, w_ref, b_ref):
        row = lax.broadcasted_iota(jnp.int32, v.shape, 0)
        prev = jnp.where(row == 0, 0.0, pltpu.roll(v, 1, 0))
        nxt = jnp.where(row == seq - 1, 0.0, pltpu.roll(v, seq - 1, 0))
        return b_ref[...] + prev * w_ref[0:1, :] + v * w_ref[1:2, :] + nxt * w_ref[2:3, :]

    u_scr[...] = sconv(u_ref[...], cwu_ref, cbu_ref) if conv_u else u_ref[...]
    gate_scr[...] = sconv(x_ref[...], cwx_ref, cbx_ref)
    fwd = fwd_ref[...]
    grp = 2 if (cb == LANES and n % 2 == 0) else 1
    for b0 in range(0, n, grp):
        ub = [u_scr[(b0 + k) * FFT_P:(b0 + k + 1) * FFT_P, :].astype(BF16) for k in range(grp)]
        uf = _dot(fwd, ub[0] if grp == 1 else jnp.concatenate(ub, axis=1))
        for k in range(grp):
            uf_scr[b0 + k, 0:FFT_P, :] = uf[:FFT_P, k * cb:(k + 1) * cb]
            uf_scr[b0 + k, SPEC_PITCH:SPEC_PITCH + FFT_P, :] = uf[FFT_P:, k * cb:(k + 1) * cb]

    rc = SPEC_HEAD if n == 1 else SUBLANES

    def freq_rows(r0, first):
        ure = [uf_scr[b, pl.ds(r0, rc), :] for b in range(n)]
        uim = [uf_scr[b, pl.ds(SPEC_PITCH + r0, rc), :] for b in range(n)]
        for a in range(n):
            yre = yim = None
            for b in range(n):
                w = a - b + (n - 1)
                gre = g_ref[w, pl.ds(r0, rc), :]
                gim = g_ref[w, pl.ds(SPEC_PITCH + r0, rc), :]
                gre2 = g_ref[w, 2 * SPEC_PITCH:2 * SPEC_PITCH + rc, :] if first else gre
                tre = gre * ure[b] - gim * uim[b]
                tim = gre2 * uim[b] + gim * ure[b]
                yre = tre if yre is None else yre + tre
                yim = tim if yim is None else yim + tim
            y_scr[a, pl.ds(r0, rc), :] = yre
            y_scr[a, pl.ds(FFT_P + r0, rc), :] = yim

    freq_rows(0, True)

    def rows_body(i, carry):
        freq_rows(pl.multiple_of(i * rc, rc), False)
        return carry

    lax.fori_loop(1, FFT_P // rc, rows_body, 0)

    def out_body(i, carry):
        ys = [y_scr[i * grp + k].astype(BF16) for k in range(grp)]
        y = _dot(inv_ref[...], ys[0] if grp == 1 else jnp.concatenate(ys, axis=1))
        for k in range(grp):
            rows = pl.ds(pl.multiple_of((i * grp + k) * FFT_P, FFT_P), FFT_P)
            ua = u_scr[rows, :]
            o_ref[rows, :] = gate_scr[rows, :] * (y[:, k * cb:(k + 1) * cb] + ua * bias_ref[...])
        return carry

    lax.fori_loop(0, n // grp, out_body, 0)


def _hyena_order(u_arr, u_row0, u_col0, x_arr, x_row0, x_col0, conv_w, conv_b, bias_row, spectra, spec_col0,
                 fwd, inv, nbatch, seq, cb, conv_u):
    n = seq // FFT_P
    d_hy = bias_row.shape[1]
    ncb = d_hy // cb
    nwin = spectra.shape[1]
    grid = (ncb, nbatch)
    in_specs = [
        pl.BlockSpec((None, seq, cb), lambda c, b: (u_col0 + c, u_row0 + b, 0)),
        pl.BlockSpec((None, seq, cb), lambda c, b: (x_col0 + c, x_row0 + b, 0)),
        pl.BlockSpec((3, cb), lambda c, b: (0, u_col0 + c)),
        pl.BlockSpec((1, cb), lambda c, b: (0, u_col0 + c)),
        pl.BlockSpec((3, cb), lambda c, b: (0, x_col0 + c)),
        pl.BlockSpec((1, cb), lambda c, b: (0, x_col0 + c)),
        pl.BlockSpec((1, cb), lambda c, b: (0, c)),
        pl.BlockSpec((None, nwin, SPEC_ROWS, cb), lambda c, b: (spec_col0 + c, 0, 0, 0)),
        pl.BlockSpec(fwd.shape, lambda c, b: (0, 0)),
        pl.BlockSpec(inv.shape, lambda c, b: (0, 0)),
    ]
    return pl.pallas_call(
        functools.partial(_hy_kernel, n=n, conv_u=conv_u),
        out_shape=jax.ShapeDtypeStruct((ncb, nbatch * seq, cb), F32),
        grid=grid,
        in_specs=in_specs,
        out_specs=pl.BlockSpec((None, seq, cb), lambda c, b: (c, b, 0)),
        scratch_shapes=[pltpu.VMEM((seq, cb), F32), pltpu.VMEM((seq, cb), F32),
                        pltpu.VMEM((n, 2 * SPEC_PITCH, cb), F32), pltpu.VMEM((n, FFT_N, cb), F32)],
        compiler_params=_cparams(("parallel", "parallel")),
        name="hyena_conv",
    )(u_arr, x_arr, conv_w, conv_b, conv_w, conv_b, bias_row, spectra, fwd, inv)


def _hyena_mix(hy_in, conv_w, conv_b, bias, spectra, fwd, inv, row0, nbatch, seq, cb):
    ncb = bias.shape[1] // cb
    z1 = _hyena_order(hy_in, row0, 0, hy_in, row0, ncb, conv_w, conv_b, bias[0:1], spectra, 0,
                      fwd, inv, nbatch, seq, cb, True)
    return _hyena_order(z1, 0, 0, hy_in, row0, 2 * ncb, conv_w, conv_b, bias[1:2], spectra, ncb,
                        fwd, inv, nbatch, seq, cb, False)


def _ret_kernel(q_ref, k_ref, v_ref, g_ref, cos_ref, sin_ref, df_ref, db_ref, ng_ref, s0f_ref, s0b_ref,
                y_ref, sf_ref, sb_ref, q_scr, k_scr, sb_scr, *, n, rope, has_state):
    c = q_ref.shape[0] // n
    dk = cos_ref.shape[1]
    hps = df_ref.shape[0]
    ri = lax.broadcasted_iota(jnp.int32, (c, c), 0)
    ci = lax.broadcasted_iota(jnp.int32, (c, c), 1)
    diff = (ri - ci).astype(F32)
    pos = lax.broadcasted_iota(jnp.int32, (c, dk), 0).astype(F32)

    def decay_terms(hh):
        lgf = -jnp.exp(df_ref[hh])
        lgb = -jnp.exp(db_ref[hh])
        mask = (jnp.where(diff >= 0, jnp.exp(jnp.maximum(diff, 0.0) * lgf[:, :1]), 0.0)
                + jnp.where(diff <= 0, jnp.exp(jnp.maximum(-diff, 0.0) * lgb[:, :1]), 0.0))
        return dict(mask=mask,
                    xi_f=jnp.exp((pos + 1.0) * lgf), ze_f=jnp.exp((c - 1.0 - pos) * lgf), cd_f=jnp.exp(c * lgf),
                    xi_b=jnp.exp((c - pos) * lgb), ze_b=jnp.exp(pos * lgb), cd_b=jnp.exp(c * lgb))

    dec = [decay_terms(hh) for hh in range(hps)]
    cols = lambda hh: slice(hh * dk, (hh + 1) * dk)

    for hh in range(hps):
        q = q_ref[:, cols(hh)]
        k = k_ref[:, cols(hh)]
        if rope:
            cs = cos_ref[...]
            sn = sin_ref[...]
            q = q * cs + pltpu.roll(q, dk // 2, 1) * sn
            k = k * cs + pltpu.roll(k, dk // 2, 1) * sn
        q_scr[:, cols(hh)] = q
        k_scr[:, cols(hh)] = k * (dk ** -0.5)
        sf_ref[hh] = s0f_ref[hh] if has_state else jnp.zeros((dk, dk), F32)
        sb_ref[hh] = s0b_ref[hh] if has_state else jnp.zeros((dk, dk), F32)

    def dot_tn(a, b):
        return lax.dot_general(a, b, (((0,), (0,)), ((), ())), preferred_element_type=F32)

    def chunk(i):
        return pl.ds(pl.multiple_of(i * c, c), c)

    def bwd_body(j, carry):
        i = n - 1 - j
        rows = chunk(i)
        for hh in range(hps):
            s = sb_ref[hh]
            sb_scr[hh, i] = s
            kz = (k_scr[rows, cols(hh)] * dec[hh]["ze_b"]).astype(BF16)
            sb_ref[hh] = s * dec[hh]["cd_b"] + dot_tn(kz, v_ref[rows, cols(hh)].astype(BF16))
        return carry

    unroll = math.gcd(n, 8)
    lax.fori_loop(0, n, bwd_body, 0, unroll=unroll)

    def fwd_body(i, carry):
        rows = chunk(i)
        for hh in range(hps):
            d = dec[hh]
            s = sf_ref[hh]
            qb = q_scr[rows, cols(hh)].astype(BF16)
            kc = k_scr[rows, cols(hh)]
            vb = v_ref[rows, cols(hh)].astype(BF16)
            sc = lax.dot_general(qb, kc.astype(BF16), (((1,), (1,)), ((), ())), preferred_element_type=F32)
            o = _dot((sc * d["mask"]).astype(BF16), vb)
            o = o + _dot(qb, s.astype(BF16)) * d["xi_f"] + _dot(qb, sb_scr[hh, i].astype(BF16)) * d["xi_b"]
            gt = g_ref[rows, cols(hh)]
            y_ref[rows, cols(hh)] = _rms(o, ng_ref[hh]) * (gt * jax.nn.sigmoid(gt))
            sf_ref[hh] = s * d["cd_f"] + dot_tn((kc * d["ze_f"]).astype(BF16), vb)
        return carry

    lax.fori_loop(0, n, fwd_body, 0, unroll=unroll)


def _retention(q, k, v, g, cos, sin, dec_f, dec_b, norm_g, s0f, s0b, nbatch, seq, rope, has_state, hps):
    nh = N_RET_HEADS
    dk = q.shape[1] // nh
    n = seq // (RET_CHUNK_LONG if seq > 2 * RET_CHUNK_LONG else RET_CHUNK)
    tspec = pl.BlockSpec((seq, hps * dk), lambda b, h: (b, h))
    rspec = pl.BlockSpec((seq, dk), lambda b, h: (0, 0))
    hspec = pl.BlockSpec((hps, 1, dk), lambda b, h: (h, 0, 0))
    sspec = pl.BlockSpec((None, None, hps, dk, dk), lambda b, h: (b, 0, h, 0, 0))
    s0spec = sspec if has_state else pl.BlockSpec((None, None, hps, dk, dk), lambda b, h: (0, 0, h, 0, 0))
    st_shape = jax.ShapeDtypeStruct((nbatch, 1, nh, dk, dk), F32)
    return pl.pallas_call(
        functools.partial(_ret_kernel, n=n, rope=rope, has_state=has_state),
        out_shape=[jax.ShapeDtypeStruct((nbatch * seq, nh * dk), F32), st_shape, st_shape],
        grid=(nbatch, nh // hps),
        in_specs=[tspec, tspec, tspec, tspec, rspec, rspec, hspec, hspec, hspec, s0spec, s0spec],
        out_specs=[tspec, sspec, sspec],
        scratch_shapes=[pltpu.VMEM((seq, hps * dk), F32), pltpu.VMEM((seq, hps * dk), F32),
                        pltpu.VMEM((hps, n, dk, dk), F32)],
        compiler_params=_cparams(("parallel", "parallel")),
        name="retention",
    )(q, k, v, g, cos, sin, dec_f, dec_b, norm_g, s0f, s0b)


def _outproj_kernel(yh_ref, yr_ref, x_ref, gt_ref, sh_ref, sc_ref, hg_ref, ng_ref, wo_ref, x1_ref, h2t_ref):
    ncb = yh_ref.shape[0]
    yh = yh_ref[0] if ncb == 1 else jnp.concatenate([yh_ref[c] for c in range(ncb)], axis=1)
    d_hy = yh.shape[1]
    nh = _rms(yh, hg_ref[...]).astype(BF16)
    y = _dot(nh, wo_ref[0:d_hy, :]) + _dot(yr_ref[...].astype(BF16), wo_ref[d_hy:, :])
    x1 = x_ref[...] + gt_ref[...] * y
    x1_ref[...] = x1
    h2 = _rms(x1, ng_ref[...]) * (1.0 + sc_ref[...]) + sh_ref[...]
    h2t_ref[...] = h2.T.astype(BF16)


def _outproj(y_hy, y_ret, x, gt1, sh2, sc2, hy_g, n2_g, wo_bf, grp, tm):
    t, d = x.shape
    row = lambda i: (i, 0)
    modspec = pl.BlockSpec((None, 1, d), lambda i: (grp(i), 0, 0))
    return pl.pallas_call(
        _outproj_kernel,
        out_shape=[jax.ShapeDtypeStruct((t, d), F32), jax.ShapeDtypeStruct((d, t), BF16)],
        grid=(t // tm,),
        in_specs=[pl.BlockSpec((y_hy.shape[0], tm, y_hy.shape[2]), lambda i: (0, i, 0)),
                  pl.BlockSpec((tm, y_ret.shape[1]), row),
                  pl.BlockSpec((tm, d), row), modspec, modspec, modspec,
                  pl.BlockSpec(hy_g.shape, lambda i: (0, 0)),
                  pl.BlockSpec((1, d), lambda i: (0, 0)),
                  pl.BlockSpec(wo_bf.shape, lambda i: (0, 0))],
        out_specs=[pl.BlockSpec((tm, d), row), pl.BlockSpec((d, tm), lambda i: (0, i))],
        compiler_params=_cparams(("parallel",)),
        name="outproj",
    )(y_hy, y_ret, x, gt1, sh2, sc2, hy_g, n2_g, wo_bf)


def _sort_desc(xs):
    xs = list(xs)
    n = len(xs)
    k = 2
    while k <= n:
        j = k // 2
        while j >= 1:
            for i in range(n):
                l = i ^ j
                if l > i:
                    hi, lo = jnp.maximum(xs[i], xs[l]), jnp.minimum(xs[i], xs[l])
                    xs[i], xs[l] = (hi, lo) if (i & k) == 0 else (lo, hi)
            j //= 2
        k *= 2
    return xs


def _merge_top(a, b):
    n = len(a)
    xs = [jnp.maximum(a[i], b[n - 1 - i]) for i in range(n)]
    j = n // 2
    while j >= 1:
        for i in range(n):
            l = i ^ j
            if l > i:
                xs[i], xs[l] = jnp.maximum(xs[i], xs[l]), jnp.minimum(xs[i], xs[l])
        j //= 2
    return xs


def _top_sorted(load, count, k):
    acc = None
    for g0 in range(0, count, k):
        grp = _sort_desc([load(i) for i in range(g0, g0 + k)])
        acc = grp if acc is None else _merge_top(acc, grp)
    return acc


def _staircase(k):
    return [(a, b) for a in range(k) for b in range(k) if (a + 1) * (b + 1) <= k + 1]


def _peer1_kernel(h2t_ref, wq_ref, keys_ref, s2_ref, e2_ref, low_ref, e1_ref, q_scr, s1_scr, s2_scr):
    h = pl.program_id(1)
    lt_n = s2_ref.shape[0]
    nk = N_KEYS
    kk = PEER_TOPK

    @pl.when(h == 0)
    def _():
        q_scr[...] = _dot(wq_ref[...], h2t_ref[...]).astype(BF16)

    dq = q_scr.shape[0] // (2 * PEER_HEADS)
    for p, scr in ((0, s1_scr), (1, s2_scr)):
        rows = pl.ds(pl.multiple_of((2 * h + p) * dq, dq), dq)
        s = _dot(keys_ref[2 * h + p], q_scr[rows, :])
        for lt in range(lt_n):
            scr[lt * TILE_PITCH:lt * TILE_PITCH + nk, :] = s[:, lt * LANES:(lt + 1) * LANES]

    ld1 = lambda i: s1_scr[pl.ds(i, lt_n, stride=TILE_PITCH), :]
    ld2 = lambda i: s2_scr[pl.ds(i, lt_n, stride=TILE_PITCH), :]
    v1 = _top_sorted(ld1, nk, kk)
    v2 = _top_sorted(ld2, nk, kk)
    pairs = _staircase(kk)
    cand = [v1[a] + v2[b] for a, b in pairs]
    neg = jnp.full_like(cand[0], -jnp.inf)
    padded = cand + [neg] * (-len(cand) % kk)
    top = _top_sorted(lambda i: padded[i], len(padded), kk)
    tau = top[kk - 1]
    nxt = neg
    for cnd in cand:
        nxt = jnp.maximum(nxt, jnp.where(cnd < tau, cnd, neg))
    thr = 0.5 * (tau + nxt)
    m = v1[0] + v2[0]
    z = jnp.zeros_like(tau)
    for cnd in cand:
        z = z + jnp.where(cnd >= tau, jnp.exp(cnd - m), 0.0)
    inv_z = 1.0 / z

    for i in range(nk):
        x1 = ld1(i)
        gap = thr - x1
        low = jnp.full_like(x1, jnp.inf)
        for b in range(kk):
            low = jnp.where(v2[b] > gap, v2[b], low)
        keep = x1 >= v1[kk - 1]
        low_ref[i] = jnp.where(keep, low, jnp.inf)
        e1_ref[i] = jnp.where(keep, jnp.exp(x1 - v1[0]) * inv_z, 0.0)
    for lt in range(lt_n):
        s2 = s2_scr[lt * TILE_PITCH:lt * TILE_PITCH + nk, :]
        s2_ref[lt] = s2
        e2_ref[lt] = jnp.where(s2 >= v2[kk - 1][lt:lt + 1, :], jnp.exp(s2 - v2[0][lt:lt + 1, :]), 0.0).astype(BF16)


def _peer1(h2t, wqt_bf, keys_bf):
    d, t = h2t.shape
    nh = PEER_HEADS
    tb = TOK_VREG
    lt_n = tb // LANES
    tile_shape = lambda dt: jax.ShapeDtypeStruct((nh, t // LANES, N_KEYS, LANES), dt)
    vreg_shape = jax.ShapeDtypeStruct((nh, t // tb, N_KEYS, lt_n, LANES), F32)
    tile_spec = pl.BlockSpec((None, lt_n, N_KEYS, LANES), lambda i, h: (h, i, 0, 0))
    vreg_spec = pl.BlockSpec((None, None, N_KEYS, lt_n, LANES), lambda i, h: (h, i, 0, 0, 0))
    return pl.pallas_call(
        _peer1_kernel,
        out_shape=[tile_shape(F32), tile_shape(BF16), vreg_shape, vreg_shape],
        grid=(t // tb, nh),
        in_specs=[pl.BlockSpec((d, tb), lambda i, h: (0, i)),
                  pl.BlockSpec(wqt_bf.shape, lambda i, h: (0, 0)),
                  pl.BlockSpec(keys_bf.shape, lambda i, h: (0, 0, 0))],
        out_specs=[tile_spec, tile_spec, vreg_spec, vreg_spec],
        scratch_shapes=[pltpu.VMEM((wqt_bf.shape[0], tb), BF16),
                        pltpu.VMEM((lt_n * TILE_PITCH, LANES), F32),
                        pltpu.VMEM((lt_n * TILE_PITCH, LANES), F32)],
        compiler_params=_cparams(("parallel", "arbitrary")),
        name="peer_select",
    )(h2t, wqt_bf, keys_bf)


def _gelu_tanh(x):
    k = -2.0 * math.sqrt(2.0 / math.pi) * math.log2(math.e)
    e = jnp.exp2(x * (x * x * (k * 0.044715) + k))
    return x / (1.0 + e)


GATE_ROWS = 2 * SUBLANES
SEL_PITCH = N_KEYS + SUBLANES


def _peer2_kernel(h2t_in, u_ref, vt_ref, s2_in, e2_in, low_ref, e1_ref, x1_ref, gt_ref, fg_ref, y_ref,
                  at_scr, zt_scr, sel_ref, h2t_ref, ot_ref):
    eb = pl.program_id(1)
    lt_n = s2_in.shape[1]
    ig_n = low_ref.shape[1]

    @pl.when(eb == 0)
    def _():
        ot_ref[...] = jnp.zeros_like(ot_ref)
        for hh in range(PEER_HEADS):
            sel_ref[:, 2 * hh, 0:N_KEYS, :] = s2_in[hh]
            sel_ref[:, 2 * hh + 1, 0:N_KEYS, :] = e2_in[hh].astype(F32)
        h2t_ref[...] = h2t_in[...]

    tcw = at_scr.shape[2]
    ntc = lt_n * LANES // tcw
    lt_per = tcw // LANES

    def row(ref, hh, ig, lt):
        return jnp.broadcast_to(ref[hh, ig, lt:lt + 1, :], (GATE_ROWS, LANES))

    def scores(tc):
        at_scr[tc % 2] = _dot(u_ref[...], h2t_ref[:, tc * tcw:(tc + 1) * tcw])

    def gates(tc):
        for ig in range(ig_n):
            for l in range(lt_per):
                lt = tc * lt_per + l
                low = [row(low_ref, hh, ig, lt) for hh in range(PEER_HEADS)]
                e1 = [row(e1_ref, hh, ig, lt) for hh in range(PEER_HEADS)]
                for c in range(N_KEYS // GATE_ROWS):
                    js = slice(c * GATE_ROWS, (c + 1) * GATE_ROWS)
                    w = jnp.zeros((GATE_ROWS, LANES), F32)
                    for hh in range(PEER_HEADS):
                        hit = sel_ref[lt, 2 * hh, js, :] >= low[hh]
                        w = w + jnp.where(hit, sel_ref[lt, 2 * hh + 1, js, :], 0.0) * e1[hh]
                    rows = slice(ig * N_KEYS + c * GATE_ROWS, ig * N_KEYS + (c + 1) * GATE_ROWS)
                    a = at_scr[tc % 2, rows, l * LANES:(l + 1) * LANES]
                    zt_scr[rows, lt * LANES:(lt + 1) * LANES] = (w * _gelu_tanh(a)).astype(BF16)

    def combine(tc):
        cols = slice(tc * tcw, (tc + 1) * tcw)
        ot_ref[:, cols] += _dot(vt_ref[...], zt_scr[:, cols])

    for tc in range(ntc):
        scores(tc)
        gates(tc)
        combine(tc)

    @pl.when(eb == pl.num_programs(1) - 1)
    def _():
        y_ref[...] = _rms(x1_ref[...] + gt_ref[...] * ot_ref[...].T, fg_ref[...])


def _peer2(h2t, u_bf, vt_bf, s2, e2, low, e1, x1, gt2, final_g, grp, eblk):
    d, t = h2t.shape
    ne = u_bf.shape[0]
    nh = PEER_HEADS
    tb = TOK_VREG
    lt_n = tb // LANES
    ig_n = eblk // N_KEYS
    tile_spec = pl.BlockSpec((nh, lt_n, N_KEYS, LANES), lambda i, e: (0, i, 0, 0))
    vreg_spec = pl.BlockSpec((nh, None, ig_n, lt_n, LANES), lambda i, e: (0, i, e, 0, 0))
    return pl.pallas_call(
        _peer2_kernel,
        out_shape=jax.ShapeDtypeStruct((t, d), F32),
        grid=(t // tb, ne // eblk),
        in_specs=[pl.BlockSpec((d, tb), lambda i, e: (0, i)),
                  pl.BlockSpec((eblk, d), lambda i, e: (e, 0)),
                  pl.BlockSpec((d, eblk), lambda i, e: (0, e)),
                  pl.BlockSpec((nh, lt_n, N_KEYS, LANES), lambda i, e: (0, i, 0, 0), pipeline_mode=pl.Buffered(1)),
                  tile_spec, vreg_spec, vreg_spec,
                  pl.BlockSpec((tb, d), lambda i, e: (i, 0), pipeline_mode=pl.Buffered(1)),
                  pl.BlockSpec((None, 1, d), lambda i, e: (grp(i), 0, 0)),
                  pl.BlockSpec((1, d), lambda i, e: (0, 0))],
        out_specs=pl.BlockSpec((tb, d), lambda i, e: (i, 0)),
        scratch_shapes=[pltpu.VMEM((2, eblk, PEER_TOKEN_CHUNK), F32), pltpu.VMEM((eblk, tb), BF16),
                        pltpu.VMEM((lt_n, 2 * nh, SEL_PITCH, LANES), F32), pltpu.VMEM((d, tb), BF16),
                        pltpu.VMEM((d, tb), F32)],
        compiler_params=_cparams(("parallel", "arbitrary")),
        name="peer_experts",
    )(h2t, u_bf, vt_bf, s2, e2, low, e1, x1, gt2, final_g)


def _rope_tables(seq, dk):
    rows = seq // GRID_W
    r, c = jnp.meshgrid(jnp.arange(rows, dtype=F32), jnp.arange(GRID_W, dtype=F32), indexing='ij')
    r = r.reshape(-1)
    c = c.reshape(-1)
    nf = dk // 4
    inv = ROPE_BASE ** (-jnp.arange(nf, dtype=F32) / nf)
    ang = jnp.concatenate([r[:, None] * inv, c[:, None] * inv], axis=-1)
    cos = jnp.concatenate([jnp.cos(ang), jnp.cos(ang)], axis=-1)
    sin = jnp.concatenate([-jnp.sin(ang), jnp.sin(ang)], axis=-1)
    return cos, sin


def _pad2(a, rows, cols):
    return jnp.pad(a, ((0, rows - a.shape[0]), (0, cols - a.shape[1])))


def kernel(x_prompt, x_sample, state_ret_fwd, state_ret_bwd, c, c_ctx, w_ada, b_ada, norm1_g, w_in, hy_conv_w, hy_conv_b, hy_w1, hy_b1, hy_w2, hy_b2, hy_w3, hy_b3, hy_w4, hy_freq, hy_bias, hy_norm_g, ret_decay_fwd, ret_decay_bwd, ret_norm_g, w_out, norm2_g, peer_wq, peer_keys, peer_u, peer_v, final_g):
    b_ctx, l_ctx, d = x_prompt.shape
    b_lat, l_lat, _ = x_sample.shape
    assert w_ada.shape[0] == 1, "one trunk layer"
    t_ctx, t_lat = b_ctx * l_ctx, b_lat * l_lat
    d_hy = hy_norm_g.shape[1]
    d_ret = ret_norm_g.shape[1]
    dk = d_ret // N_RET_HEADS
    tm = TOKEN_BLOCK
    assert l_ctx % FFT_P == 0 and l_lat % TOK_VREG == 0 and t_ctx % TOK_VREG == 0 and b_lat < SUBLANES

    ngrp = 1 + b_lat
    cond = jnp.concatenate([c_ctx[None, :], c], axis=0)
    cond8 = jnp.pad(cond, ((0, SUBLANES - ngrp), (0, 0)))
    mod = _ada(cond8, w_ada[0], b_ada[0])
    sh1, sc1, gt1, sh2, sc2, gt2 = [m.reshape(SUBLANES, 1, d) for m in jnp.split(mod, 6, axis=-1)]

    fw = hy_w1.shape[-1]
    hl = LANES // 2
    assert fw <= hl and HY_EMB <= hl
    f = jnp.linspace(1e-4, HY_BANDS - 1, HY_BANDS, dtype=F32)
    frow = jnp.zeros((1, hl), F32).at[0, 1:1 + HY_BANDS].set(f).at[0, 1 + HY_BANDS:1 + 2 * HY_BANDS].set(f)
    min_decay = math.log(HY_DECAY_TARGET) / HY_SLOW_PCT
    max_decay = math.log(HY_DECAY_TARGET) / HY_FAST_PCT
    deltas = jnp.abs(jnp.linspace(min_decay, max_decay, d_hy, dtype=F32))
    twice = lambda a: jnp.tile(_pad2(a, 1, hl), (1, 2))
    diag2 = lambda a: jnp.kron(jnp.eye(2, dtype=F32), _pad2(a, hl, hl))
    w4 = hy_w4[0].reshape(fw, 2, 2 * d_hy).transpose(1, 0, 2)
    w4_lo = jnp.pad(w4, ((0, 0), (0, LANES - fw), (0, 0)))
    w4_hi = jnp.pad(w4, ((0, 0), (hl, hl - fw), (0, 0)))
    filt_w = (twice(frow), diag2(hy_w1[0]), twice(hy_b1), diag2(hy_w2[0]), twice(hy_b2), diag2(hy_w3[0]),
              twice(hy_b3), twice(hy_freq), w4_lo, w4_hi, jnp.tile(deltas, 2)[None, :])
    fwd_m, inv_m, flt_m = _dft_mats()
    cos_t, sin_t = _rope_tables(l_lat, dk)
    dec_f = jnp.broadcast_to(ret_decay_fwd[0][:, None, None], (N_RET_HEADS, 1, dk))
    dec_b = jnp.broadcast_to(ret_decay_bwd[0][:, None, None], (N_RET_HEADS, 1, dk))
    ng = ret_norm_g[0].reshape(N_RET_HEADS, 1, dk)
    w_in_bf = w_in[0].astype(BF16)
    w_out_bf = w_out[0].astype(BF16)
    wqt = peer_wq[0].T.astype(BF16)
    keys = peer_keys[0].reshape(2 * PEER_HEADS, N_KEYS, -1).astype(BF16)
    u_bf = peer_u[0].astype(BF16)
    vt_bf = peer_v[0].T.astype(BF16)
    fg = final_g.reshape(1, d)

    def trunk(x, nbatch, seq, latent):
        hy_cb, ret_hps = _path_blocks(seq, d_hy)
        grp = (lambda i: 1 + i // (seq // tm)) if latent else (lambda i: 0)
        hy_in, rq, rk, rv, rg = _inproj(x, sh1, sc1, norm1_g, w_in_bf, grp, tm, 3 * d_hy, d_ret, hy_cb)
        spectra = _hyena_spectra(seq, filt_w, flt_m, hy_cb)
        y_hy = _hyena_mix(hy_in, hy_conv_w[0], hy_conv_b, hy_bias[0], spectra, fwd_m, inv_m, 0, nbatch, seq, hy_cb)
        y_ret, s_f, s_b = _retention(rq, rk, rv, rg, cos_t, sin_t, dec_f, dec_b, ng, state_ret_fwd, state_ret_bwd,
                                     nbatch, seq, latent, latent, ret_hps)
        x1, h2t = _outproj(y_hy, y_ret, x, gt1, sh2, sc2, hy_norm_g, norm2_g, w_out_bf, grp, tm)
        s2, e2, low, e1 = _peer1(h2t, wqt, keys)
        grp_tok = (lambda i: 1 + i // (seq // TOK_VREG)) if latent else (lambda i: 0)
        y = _peer2(h2t, u_bf, vt_bf, s2, e2, low, e1, x1, gt2, fg, grp_tok, PEER_EXPERT_BLOCK)
        return y.reshape(nbatch, seq, d), s_f, s_b

    y_prompt, sf_new, sb_new = trunk(x_prompt.reshape(t_ctx, d), b_ctx, l_ctx, False)
    y_sample, _, _ = trunk(x_sample.reshape(t_lat, d), b_lat, l_lat, True)
    return (y_prompt, y_sample, sf_new, sb_new)
```
